```python
import math
import jax, jax.numpy as jnp
from jax import lax
import numpy as np

D_MODEL = 1024
BATCH = 8
SEQ = 8192
DEPTH = 1
DEC_BATCH = 128
DEC_SEQ = 1
PAST_LEN = 8192
PAGE_SIZE = 128

HEAD_DIM = 64
HA = D_MODEL // (2 * HEAD_DIM)
HB = D_MODEL // (2 * HEAD_DIM)
A_WIDTH = HA * HEAD_DIM
B_WIDTH = HB * HEAD_DIM
MIX_WIDTH = A_WIDTH + B_WIDTH
CONV_K = 4
GDN_CHUNK = 64
ROT_DIM = HEAD_DIM // 4
ROPE_THETA = 500000.0
DILATIONS = ((128, 1), (512, 4), (2048, 16))
MAX_WINDOW = 2048
N_GROUPS = 4
EXPERTS_PER_GROUP = 8
N_EXPERTS = N_GROUPS * EXPERTS_PER_GROUP
TOP_K = 2
D_EXPERT = D_MODEL // 2
MOE_BLOCK = 128
PLE_DIM = 256
NORM_EPS = 1e-6
OFF_Z = 3 * A_WIDTH
OFF_A = OFF_Z + A_WIDTH
OFF_B = OFF_A + HA
OFF_WIN = OFF_B + HA
IN_WIDTH = OFF_WIN + 3 * B_WIDTH

kernel_name = 'hymba_gdn_dilated_hmoe_step'

F32 = jnp.float32


def rmsnorm(x, g):
    xf = x.astype(F32)
    y = xf * lax.rsqrt(jnp.mean(xf * xf, axis=-1, keepdims=True) + NORM_EPS)
    return (y * g.astype(F32)).astype(x.dtype)


def l2norm(x):
    xf = x.astype(F32)
    return xf * lax.rsqrt(jnp.sum(xf * xf, axis=-1, keepdims=True) + NORM_EPS)


def rope_partial(x, pos):
    half = ROT_DIM // 2
    inv = ROPE_THETA ** (-jnp.arange(half, dtype=F32) * (2.0 / ROT_DIM))
    ang = pos.astype(F32)[:, None] * inv[None, :]
    cos = jnp.cos(ang)[None, :, None, :]
    sin = jnp.sin(ang)[None, :, None, :]
    xr = x[..., :ROT_DIM].astype(F32)
    x1, x2 = xr[..., :half], xr[..., half:]
    rot = jnp.concatenate([x1 * cos - x2 * sin, x2 * cos + x1 * sin], axis=-1)
    return jnp.concatenate([rot.astype(x.dtype), x[..., ROT_DIM:]], axis=-1)


def causal_conv(x, prefix, w):
    T = x.shape[1]
    xp = jnp.concatenate([prefix.astype(x.dtype), x], axis=1)
    y = sum(xp[:, j:j + T] * w[j] for j in range(CONV_K))
    return jax.nn.silu(y), xp[:, -(CONV_K - 1):]


def gated_delta_chunked(q, k, v, log_a, beta, s0):
    N, T, H, DK = q.shape
    C = GDN_CHUNK
    Tp = -(-T // C) * C
    nc = Tp // C

    def blocks(t):
        t = jnp.pad(t.astype(F32), [(0, 0), (0, Tp - T)] + [(0, 0)] * (t.ndim - 2))
        t = t.reshape((N, nc, C) + t.shape[2:])
        return jnp.moveaxis(t, 2, 3)

    q = blocks(q) * (DK ** -0.5)
    k = blocks(k)
    v = blocks(v)
    g = jnp.cumsum(blocks(log_a), axis=-1)
    b = blocks(beta)
    incl = jnp.tril(jnp.ones((C, C), bool))
    strict = jnp.tril(jnp.ones((C, C), bool), -1)
    decay = jnp.exp(jnp.where(incl, g[..., :, None] - g[..., None, :], -jnp.inf))
    kb = k * b[..., None]
    A = jnp.where(strict, jnp.einsum('nchid,nchjd->nchij', kb, k) * decay, 0.0)
    eye = jnp.eye(C, dtype=F32)
    Tm = lax.linalg.triangular_solve(eye + A, jnp.broadcast_to(eye, A.shape), left_side=True, lower=True)
    u = Tm @ (v * b[..., None])
    w = Tm @ (kb * jnp.exp(g)[..., None])
    qk = jnp.einsum('nchid,nchjd->nchij', q, k) * decay
    q_dec = q * jnp.exp(g)[..., None]
    k_dec = k * jnp.exp(g[..., -1:] - g)[..., None]
    a_last = jnp.exp(g[..., -1])

    def step(S, xs):
        u_c, w_c, qk_c, qd_c, kd_c, al_c = xs
        v_new = u_c - w_c @ S
        o_c = qd_c @ S + qk_c @ v_new
        S = S * al_c[..., None, None] + jnp.swapaxes(kd_c, -1, -2) @ v_new
        return S, o_c

    xs = tuple(jnp.moveaxis(t, 1, 0) for t in (u, w, qk, q_dec, k_dec, a_last))
    S, o = lax.scan(step, s0.astype(F32), xs)
    o = jnp.swapaxes(jnp.moveaxis(o, 0, 1), 2, 3).reshape(N, Tp, H, -1)[:, :T]
    return o, S


def delta_heads(qkv, z, a_logit, b_logit, conv_prefix, s0, w_conv, a_log, dt_bias, g_out):
    N, T, _ = qkv.shape
    c, conv_new = causal_conv(qkv, conv_prefix, w_conv)
    c = c.reshape(N, T, 3, HA, HEAD_DIM)
    q, k, v = l2norm(c[:, :, 0]), l2norm(c[:, :, 1]), c[:, :, 2]
    log_a = -jnp.exp(a_log.astype(F32)) * jax.nn.softplus(a_logit.astype(F32) + dt_bias.astype(F32))
    beta = jax.nn.sigmoid(b_logit.astype(F32))
    o, s_new = gated_delta_chunked(q, k, v, log_a, beta, s0)
    o = rmsnorm(o, g_out) * jax.nn.silu(z.astype(F32).reshape(N, T, HA, HEAD_DIM))
    return o.reshape(N, T, A_WIDTH).astype(qkv.dtype), conv_new, s_new.astype(s0.dtype)


def window_qkv(qkv, pos):
    N, T, _ = qkv.shape
    qkv = qkv.reshape(N, T, 3, HB, HEAD_DIM)
    return rope_partial(qkv[:, :, 0], pos), rope_partial(qkv[:, :, 1], pos), qkv[:, :, 2]


def dilated_band(q, k, v, window, dil):
    N, S, H, D = q.shape
    band = window // dil
    unit = band * dil
    Sp = -(-S // unit) * unit
    L = Sp // dil
    nb = L // band

    def split(t):
        t = jnp.pad(t, ((0, 0), (0, Sp - S), (0, 0), (0, 0)))
        t = t.reshape(N, L, dil, H, D).transpose(0, 2, 1, 3, 4)
        return t.reshape(N * dil, nb, band, H, D)

    def with_prev(t):
        prev = jnp.concatenate([jnp.zeros_like(t[:, :1]), t[:, :-1]], axis=1)
        return jnp.concatenate([prev, t], axis=2)

    qs = split(q)
    ks = with_prev(split(k))
    vs = with_prev(split(v))
    s = jnp.einsum('mbqhd,mbkhd->mbhqk', qs, ks).astype(F32) * (D ** -0.5)
    a = jnp.arange(band)[:, None]
    c = jnp.arange(2 * band)[None, :]
    dist = band + a - c
    valid = (dist >= 0) & (dist <= band)
    valid = valid[None] & ((jnp.arange(nb)[:, None, None] > 0) | (c[None] >= band))
    s = jnp.where(valid[None, :, None], s, -jnp.inf)
    lse = jax.nn.logsumexp(s, axis=-1)
    p = jnp.exp(s - lse[..., None])
    o = jnp.einsum('mbhqk,mbkhd->mbqhd', p, vs.astype(F32))
    o = o.reshape(N, dil, L, H, D).transpose(0, 2, 1, 3, 4).reshape(N, Sp, H, D)[:, :S]
    lse = jnp.swapaxes(lse, -1, -2).reshape(N, dil, L, H).transpose(0, 2, 1, 3).reshape(N, Sp, H)[:, :S]
    return o, lse


def dilated_gather(q, k_all, v_all, n_past):
    N, T, H, D = q.shape
    qf = q.astype(F32) * (D ** -0.5)
    results = []
    for window, dil in DILATIONS:
        n = window // dil + 1
        idx = n_past + jnp.arange(T)[:, None] - dil * jnp.arange(n)[None, :]
        valid = idx >= 0
        idx = jnp.maximum(idx, 0)
        kg = k_all[:, idx].astype(F32)
        vg = v_all[:, idx].astype(F32)
        s = jnp.einsum('nthd,ntjhd->nthj', qf, kg)
        s = jnp.where(valid[None, :, None, :], s, -jnp.inf)
        lse = jax.nn.logsumexp(s, axis=-1)
        p = jnp.exp(s - lse[..., None])
        results.append((jnp.einsum('nthj,ntjhd->nthd', p, vg), lse))
    return results


def merge_dilations(results, g_out):
    o = jnp.stack([r[0] for r in results])
    lse = jnp.stack([r[1] for r in results])
    wgt = jax.nn.softmax(lse, axis=0)
    o = jnp.sum(wgt[..., None] * o, axis=0)
    N, T = o.shape[:2]
    return rmsnorm(o, g_out).reshape(N, T, B_WIDTH)


def hier_moe(x, w_rg, b_rg, w_re, b_re, w_gate, w_up, w_down):
    Tn, D = x.shape
    g_logits = (x @ w_rg).astype(F32) + b_rg.astype(F32)
    grp = jnp.argmax(g_logits, axis=-1)
    p_grp = jnp.take_along_axis(jax.nn.softmax(g_logits, axis=-1), grp[:, None], axis=-1)
    e_logits = ((x @ w_re.reshape(D, N_EXPERTS)).astype(F32) + b_re.reshape(N_EXPERTS).astype(F32))
    e_logits = e_logits.reshape(Tn, N_GROUPS, EXPERTS_PER_GROUP)
    e_in = jnp.take_along_axis(e_logits, grp[:, None, None], axis=1)[:, 0]
    top_v, top_i = lax.top_k(e_in, TOP_K)
    gate = p_grp * jax.nn.softmax(top_v, axis=-1)
    eid = (grp[:, None] * EXPERTS_PER_GROUP + top_i).reshape(-1)
    M = Tn * TOP_K
    order = jnp.argsort(eid)
    e_sorted = eid[order]
    tok = order // TOP_K
    sizes = jnp.bincount(eid, length=N_EXPERTS)
    padded = (sizes + MOE_BLOCK - 1) // MOE_BLOCK * MOE_BLOCK
    start = jnp.cumsum(sizes) - sizes
    pend = jnp.cumsum(padded)
    pstart = pend - padded
    dest = pstart[e_sorted] + jnp.arange(M) - start[e_sorted]
    n_blk = -(-M // MOE_BLOCK) + N_EXPERTS
    x_buf = jnp.zeros((n_blk * MOE_BLOCK, D), x.dtype).at[dest].set(x[tok])
    blk_expert = jnp.minimum(jnp.searchsorted(pend, jnp.arange(n_blk) * MOE_BLOCK, side='right'), N_EXPERTS - 1)

    def expert_block(args):
        xb, e = args
        hb = jax.nn.silu(xb @ w_gate[e]) * (xb @ w_up[e])
        return hb @ w_down[e]

    y_buf = lax.map(expert_block, (x_buf.reshape(n_blk, MOE_BLOCK, D), blk_expert)).reshape(n_blk * MOE_BLOCK, D)
    contrib = y_buf[dest] * gate.reshape(-1)[order][:, None].astype(x.dtype)
    return jnp.zeros_like(x).at[tok].add(contrib)


def setup_inputs(seed: int = 0) -> dict:
    key = jax.random.key(seed)
    ks = iter(jax.random.split(key, 40))

    def nrm(shape, scale):
        return jax.random.normal(next(ks), shape, F32) * scale

    win_buf = min(MAX_WINDOW, PAST_LEN)
    x_prompt = nrm((BATCH, SEQ, D_MODEL), 1.0)
    x_sample = nrm((DEC_BATCH, DEC_SEQ, D_MODEL), 1.0)
    cache_win_k = nrm((DEPTH, DEC_BATCH, win_buf, HB, HEAD_DIM), 1.0)
    cache_win_v = nrm((DEPTH, DEC_BATCH, win_buf, HB, HEAD_DIM), 1.0)
    state_conv = nrm((DEPTH, DEC_BATCH, CONV_K - 1, 3 * A_WIDTH), 1.0)
    state_delta = nrm((DEPTH, DEC_BATCH, HA, HEAD_DIM, HEAD_DIM), 0.1)
    p_prompt = nrm((DEPTH, BATCH, SEQ, PLE_DIM), 1.0)
    p_sample = nrm((DEPTH, DEC_BATCH, DEC_SEQ, PLE_DIM), 1.0)
    g_attn_norm = 1.0 + nrm((DEPTH, D_MODEL), 0.02)
    w_in = nrm((DEPTH, D_MODEL, IN_WIDTH), D_MODEL ** -0.5)
    w_conv = nrm((DEPTH, CONV_K, 3 * A_WIDTH), CONV_K ** -0.5)
    a_log = jnp.log(jax.random.uniform(next(ks), (DEPTH, HA), F32, 1.0, 16.0))
    dt = jnp.exp(jax.random.uniform(next(ks), (DEPTH, HA), F32, math.log(1e-3), math.log(1e-1)))
    dt_bias = dt + jnp.log(-jnp.expm1(-dt))
    g_a_out = 1.0 + nrm((DEPTH, HEAD_DIM), 0.02)
    g_b_out = 1.0 + nrm((DEPTH, HEAD_DIM), 0.02)
    w_out = nrm((DEPTH, MIX_WIDTH, D_MODEL), MIX_WIDTH ** -0.5)
    g_ffn_norm = 1.0 + nrm((DEPTH, D_MODEL), 0.02)
    w_router_group = nrm((DEPTH, D_MODEL, N_GROUPS), D_MODEL ** -0.5)
    b_router_group = nrm((DEPTH, N_GROUPS), 0.01)
    w_router_expert = nrm((DEPTH, D_MODEL, N_GROUPS, EXPERTS_PER_GROUP), D_MODEL ** -0.5)
    b_router_expert = nrm((DEPTH, N_GROUPS, EXPERTS_PER_GROUP), 0.01)
    w_exp_gate = nrm((DEPTH, N_EXPERTS, D_MODEL, D_EXPERT), D_MODEL ** -0.5)
    w_exp_up = nrm((DEPTH, N_EXPERTS, D_MODEL, D_EXPERT), D_MODEL ** -0.5)
    w_exp_down = nrm((DEPTH, N_EXPERTS, D_EXPERT, D_MODEL), D_EXPERT ** -0.5)
    w_ple_gate = nrm((DEPTH, D_MODEL, D_MODEL), D_MODEL ** -0.5)
    w_ple_proj = nrm((DEPTH, PLE_DIM, D_MODEL), PLE_DIM ** -0.5)
    g_final = 1.0 + nrm((D_MODEL,), 0.02)
    return {'x_prompt': x_prompt, 'x_sample': x_sample, 'cache_win_k': cache_win_k, 'cache_win_v': cache_win_v,
            'state_conv': state_conv, 'state_delta': state_delta, 'p_prompt': p_prompt, 'p_sample': p_sample,
            'g_attn_norm': g_attn_norm, 'w_in': w_in, 'w_conv': w_conv, 'a_log': a_log, 'dt_bias': dt_bias,
            'g_a_out': g_a_out, 'g_b_out': g_b_out, 'w_out': w_out, 'g_ffn_norm': g_ffn_norm,
            'w_router_group': w_router_group, 'b_router_group': b_router_group,
            'w_router_expert': w_router_expert, 'b_router_expert': b_router_expert,
            'w_exp_gate': w_exp_gate, 'w_exp_up': w_exp_up, 'w_exp_down': w_exp_down,
            'w_ple_gate': w_ple_gate, 'w_ple_proj': w_ple_proj, 'g_final': g_final}


def reference(x_prompt, x_sample, cache_win_k, cache_win_v, state_conv, state_delta, p_prompt, p_sample,
              g_attn_norm, w_in, w_conv, a_log, dt_bias, g_a_out, g_b_out, w_out, g_ffn_norm,
              w_router_group, b_router_group, w_router_expert, b_router_expert,
              w_exp_gate, w_exp_up, w_exp_down, w_ple_gate, w_ple_proj, g_final):
    NP, S, _ = x_prompt.shape
    NS, T, _ = x_sample.shape
    n_past = cache_win_k.shape[2]
    pos_p = jnp.arange(S, dtype=jnp.int32)
    pos_s = PAST_LEN + jnp.arange(T, dtype=jnp.int32)
    keep = min(MAX_WINDOW, S)

    def mixer_in(h, i):
        u = rmsnorm(h, g_attn_norm[i]) @ w_in[i]
        return (u[..., :OFF_Z], u[..., OFF_Z:OFF_A], u[..., OFF_A:OFF_B], u[..., OFF_B:OFF_WIN], u[..., OFF_WIN:])

    def tail(h, oa, ob, p, i):
        h = h + jnp.concatenate([oa.astype(h.dtype), ob.astype(h.dtype)], axis=-1) @ w_out[i]
        N, L, _ = h.shape
        m = rmsnorm(h, g_ffn_norm[i]).reshape(N * L, D_MODEL)
        h = h + hier_moe(m, w_router_group[i], b_router_group[i], w_router_expert[i], b_router_expert[i],
                         w_exp_gate[i], w_exp_up[i], w_exp_down[i]).reshape(N, L, D_MODEL)
        return h + jax.nn.sigmoid(h @ w_ple_gate[i]) * (p.astype(h.dtype) @ w_ple_proj[i])

    hp, hs = x_prompt, x_sample
    wk_p, wv_p, cv_p, dl_p = [], [], [], []
    wk_s, wv_s, cv_s, dl_s = [], [], [], []
    for i in range(DEPTH):
        qkv_a, z_a, a_a, b_a, qkv_w = mixer_in(hp, i)
        oa, conv_new, s_new = delta_heads(qkv_a, z_a, a_a, b_a,
                                          jnp.zeros((NP, CONV_K - 1, 3 * A_WIDTH), hp.dtype),
                                          jnp.zeros((NP, HA, HEAD_DIM, HEAD_DIM), hp.dtype),
                                          w_conv[i], a_log[i], dt_bias[i], g_a_out[i])
        q, k, v = window_qkv(qkv_w, pos_p)
        ob = merge_dilations([dilated_band(q, k, v, w, d) for w, d in DILATIONS], g_b_out[i])
        hp = tail(hp, oa, ob, p_prompt[i], i)
        wk_p.append(k[:, S - keep:])
        wv_p.append(v[:, S - keep:])
        cv_p.append(conv_new)
        dl_p.append(s_new)

        qkv_a, z_a, a_a, b_a, qkv_w = mixer_in(hs, i)
        oa, conv_new, s_new = delta_heads(qkv_a, z_a, a_a, b_a, state_conv[i], state_delta[i],
                                          w_conv[i], a_log[i], dt_bias[i], g_a_out[i])
        q, k, v = window_qkv(qkv_w, pos_s)
        k_all = jnp.concatenate([cache_win_k[i].astype(k.dtype), k], axis=1)
        v_all = jnp.concatenate([cache_win_v[i].astype(v.dtype), v], axis=1)
        ob = merge_dilations(dilated_gather(q, k_all, v_all, n_past), g_b_out[i])
        hs = tail(hs, oa, ob, p_sample[i], i)
        wk_s.append(k)
        wv_s.append(v)
        cv_s.append(conv_new)
        dl_s.append(s_new)

    y_prompt = rmsnorm(hp, g_final)
    y_sample = rmsnorm(hs, g_final)
    return (y_prompt, y_sample, jnp.stack(wk_p), jnp.stack(wv_p), jnp.stack(cv_p), jnp.stack(dl_p),
            jnp.stack(wk_s), jnp.stack(wv_s), jnp.stack(cv_s), jnp.stack(dl_s))
```

```python
import functools
import math

import jax
import jax.numpy as jnp
from jax import lax
from jax.experimental import pallas as pl
from jax.experimental.pallas import tpu as pltpu

F32 = jnp.float32
BF16 = jnp.bfloat16

HEAD_DIM = 64
GDN_CHUNK = 64
ROT_DIM = HEAD_DIM // 4
ROPE_THETA = 500000.0
PAST_LEN = 8192
DILATIONS = ((128, 1), (512, 4), (2048, 16))
BAND = 128
N_GROUPS = 4
EXPERTS_PER_GROUP = 8
N_EXPERTS = N_GROUPS * EXPERTS_PER_GROUP
NORM_EPS = 1e-6
NEG = -1e30
LANES = 128
VMEM_LIMIT = 56 * 1024 * 1024


def _cparams(sem):
    return pltpu.CompilerParams(dimension_semantics=sem, vmem_limit_bytes=VMEM_LIMIT)


def _dot(a, b):
    return jnp.dot(a, b, preferred_element_type=F32)


def _dot_nt(a, b):
    return lax.dot_general(a, b, (((1,), (1,)), ((), ())), preferred_element_type=F32)


def _dot_tn(a, b):
    return lax.dot_general(a, b, (((0,), (0,)), ((), ())), preferred_element_type=F32)


def _split3(x):
    hi = x.astype(BF16)
    r = x - hi.astype(F32)
    mid = r.astype(BF16)
    lo = (r - mid.astype(F32)).astype(BF16)
    return hi, mid, lo


def _dot_f32_lhs(x, w_bf16):
    hi, mid, lo = _split3(x)
    return _dot(hi, w_bf16) + _dot(mid, w_bf16) + _dot(lo, w_bf16)


def _dot_f32_rhs(w_bf16, x):
    hi, mid, lo = _split3(x)
    return _dot(w_bf16, hi) + _dot(w_bf16, mid) + _dot(w_bf16, lo)


HEAD_SHIFT = HEAD_DIM.bit_length() - 1


def _lane_head(i):
    return lax.shift_right_logical(i, HEAD_SHIFT)


def _lane_in_head(i):
    return lax.bitwise_and(i, HEAD_DIM - 1)


def _sigmoid(x):
    return 1.0 / (1.0 + jnp.exp(-x))


def _silu(x):
    return x * _sigmoid(x)


def _softplus(x):
    return jnp.maximum(x, 0.0) + jnp.log1p(jnp.exp(-jnp.abs(x)))


def _rms(x, g):
    return x * lax.rsqrt(jnp.mean(x * x, axis=-1, keepdims=True) + NORM_EPS) * g


def _head_rms(x, bones, g):
    ss = _dot((x * x).astype(BF16), bones)
    return x * lax.rsqrt(ss * (1.0 / HEAD_DIM) + NORM_EPS) * g


def _proj_kernel(*refs, tm, aw, sample):
    if sample:
        (x_ref, g_ref, w_ref, wc_ref, alog_ref, dtb_ref, esel_ref, cos_ref, s1_ref, s2_ref, bones_ref,
         st0_ref, st1_ref, st2_ref,
         qa_ref, ka_ref, va_ref, z_ref, la_ref, bt_ref, qb_ref, kb_ref, vb_ref, conv_ref) = refs
        buf = None
    else:
        (x_ref, g_ref, w_ref, wc_ref, alog_ref, dtb_ref, esel_ref, cos_ref, s1_ref, s2_ref, bones_ref,
         qa_ref, ka_ref, va_ref, z_ref, la_ref, bt_ref, qb_ref, kb_ref, vb_ref, conv_ref, buf) = refs
        t = pl.program_id(1)

        @pl.when(t == 0)
        def _():
            buf[0:8, :] = jnp.zeros((8, 3 * aw), F32)

    xn = _rms(x_ref[0], g_ref[...]).astype(BF16)
    bones = bones_ref[...]

    outs_a = (qa_ref, ka_ref, va_ref)
    for c in range(3):
        cs = slice(c * aw, (c + 1) * aw)
        u = _dot(xn, w_ref[:, cs])
        if sample:
            conv_ref[0, :, cs] = u
            y = (wc_ref[3:4, cs] * u + wc_ref[2:3, cs] * st2_ref[0, :, cs]
                 + wc_ref[1:2, cs] * st1_ref[0, :, cs] + wc_ref[0:1, cs] * st0_ref[0, :, cs])
        else:
            buf[8:8 + tm, cs] = u
            y = (wc_ref[3:4, cs] * u + wc_ref[2:3, cs] * buf[7:7 + tm, cs]
                 + wc_ref[1:2, cs] * buf[6:6 + tm, cs] + wc_ref[0:1, cs] * buf[5:5 + tm, cs])
        y = _silu(y)
        if c < 2:
            ss = _dot((y * y).astype(BF16), bones)
            y = y * lax.rsqrt(ss + NORM_EPS)
        outs_a[c][0] = y
    if not sample:
        tail = buf[tm:tm + 8, :]
        conv_ref[0] = tail
        buf[0:8, :] = tail

    z_ref[0] = _dot(xn, w_ref[:, 3 * aw:4 * aw])

    ab = _dot(xn, w_ref[:, 7 * aw:7 * aw + LANES])
    lane = lax.broadcasted_iota(jnp.int32, ab.shape, 1)
    log_a = -jnp.exp(alog_ref[...]) * _softplus(ab + dtb_ref[...])
    comb = jnp.where(lane < aw // HEAD_DIM, log_a, _sigmoid(ab))
    ex = _dot_f32_lhs(comb, esel_ref[...])
    la_ref[0] = ex[:, :aw]
    bt_ref[0] = ex[:, aw:]

    cosv, s1v, s2v = cos_ref[...], s1_ref[...], s2_ref[...]
    for c, oref in ((0, qb_ref), (1, kb_ref)):
        for gq in range(aw // LANES):
            cs = slice(4 * aw + c * aw + gq * LANES, 4 * aw + c * aw + (gq + 1) * LANES)
            u = _dot(xn, w_ref[:, cs])
            r = (u * cosv + pltpu.roll(u, LANES - ROT_DIM // 2, 1) * s1v
                 + pltpu.roll(u, ROT_DIM // 2, 1) * s2v)
            oref[0, :, gq * LANES:(gq + 1) * LANES] = r
    vb_ref[0] = _dot(xn, w_ref[:, 6 * aw:7 * aw])


def _proj(x3, g, w_cat, wconv, alog, dtb, esel, cos, s1, s2, bones, states, *, tm, sample):
    n, t, d = x3.shape
    aw = bones.shape[0]
    nt = t // tm
    const = lambda shape: pl.BlockSpec(shape, lambda i, j: (0,) * len(shape))
    row = lambda width: pl.BlockSpec((1, tm, width), lambda i, j: (i, j, 0))
    tab = (pl.BlockSpec((1, LANES), lambda i, j: (0, 0)) if sample
           else pl.BlockSpec((tm, LANES), lambda i, j: (j, 0)))
    in_specs = [row(d), const((1, d)), const(w_cat.shape), const(wconv.shape), const((1, LANES)),
                const((1, LANES)), const(esel.shape), tab, tab, tab, const(bones.shape)]
    args = [x3, g, w_cat, wconv, alog, dtb, esel, cos, s1, s2, bones]
    if sample:
        in_specs += [row(3 * aw)] * 3
        args += list(states)
        conv_shape = jax.ShapeDtypeStruct((n, t, 3 * aw), F32)
        conv_spec = row(3 * aw)
        scratch = []
    else:
        conv_shape = jax.ShapeDtypeStruct((n, 8, 3 * aw), F32)
        conv_spec = pl.BlockSpec((1, 8, 3 * aw), lambda i, j: (i, 0, 0))
        scratch = [pltpu.VMEM((tm + 8, 3 * aw), F32)]
    o = jax.ShapeDtypeStruct((n, t, aw), F32)
    return pl.pallas_call(
        functools.partial(_proj_kernel, tm=tm, aw=aw, sample=sample),
        grid=(n, nt),
        in_specs=in_specs,
        out_specs=[row(aw)] * 9 + [conv_spec],
        out_shape=[o] * 9 + [conv_shape],
        scratch_shapes=scratch,
        compiler_params=_cparams(("arbitrary", "arbitrary")),
        name="proj_sample" if sample else "proj_prompt",
    )(*args)


def _rowstack(x, hms):
    return jnp.concatenate([jnp.where(hm, x, 0.0) for hm in hms], axis=0)


def _gdn_kernel(q_ref, k_ref, v_ref, la_ref, bt_ref, ltri_ref, ones_ref, o_ref, s_out_ref,
                S, U, W, QD, KD, QK, AL, *, tt, hw):
    C = GDN_CHUNK
    qw = 4 * HEAD_DIM
    nq = hw // qw
    t = pl.program_id(1)

    @pl.when(t == 0)
    def _():
        S[...] = jnp.zeros_like(S)

    rowi = lax.broadcasted_iota(jnp.int32, (C, hw), 0)
    colj = _lane_in_head(lax.broadcasted_iota(jnp.int32, (C, hw), 1))
    m_incl = colj <= rowi
    m_strict = colj < rowi
    eye = (colj == rowi).astype(F32)
    lane_q = _lane_head(lax.broadcasted_iota(jnp.int32, (1, qw), 1))
    hms = [lane_q == h for h in range(4)]
    ltri = ltri_ref[...]
    ones = ones_ref[...]

    def pm(l, r):
        return _dot(l.astype(BF16), _rowstack(r, hms).astype(BF16))

    def intra(c, carry):
        r0 = pl.multiple_of(c * C, C)
        rows = pl.ds(r0, C)
        q = q_ref[0, rows, :] * (HEAD_DIM ** -0.5)
        k = k_ref[0, rows, :]
        v = v_ref[0, rows, :]
        la = la_ref[0, rows, :]
        b = bt_ref[0, rows, :]
        g = _dot_f32_rhs(ltri, la)
        grow = _dot_f32_rhs(ones, g * eye)
        dec = jnp.exp(jnp.where(m_incl, g - grow, NEG))
        eg = jnp.exp(g)
        glast = g[C - 1:C, :]
        kdec = jnp.exp(glast - g)
        AL[pl.ds(pl.multiple_of(c * 8, 8), 8), :] = jnp.broadcast_to(jnp.exp(glast), (8, hw))
        kb = k * b
        vb = v * b
        kbe = kb * eg
        for qi in range(nq):
            sl = slice(qi * qw, (qi + 1) * qw)
            kst = _rowstack(k[:, sl], hms).astype(BF16)
            a = jnp.where(m_strict[:, sl], _dot_nt(kb[:, sl].astype(BF16), kst) * dec[:, sl], 0.0)
            x = -a
            p = eye[:, sl] + x
            for _ in range(int(math.log2(C)) - 1):
                x = pm(x, x)
                p = p + pm(p, x)
            U[rows, sl] = pm(p, vb[:, sl])
            W[rows, sl] = pm(p, kbe[:, sl])
            QK[rows, sl] = jnp.where(m_incl[:, sl], _dot_nt(q[:, sl].astype(BF16), kst) * dec[:, sl], 0.0)
            QD[rows, sl] = q[:, sl] * eg[:, sl]
            KD[rows, sl] = k[:, sl] * kdec[:, sl]
        return carry

    lax.fori_loop(0, tt // C, intra, 0)

    def inter(c, carry):
        r0 = pl.multiple_of(c * C, C)
        rows = pl.ds(r0, C)
        al = AL[pl.ds(pl.multiple_of(c * 8, 8), 1), :]
        for qi in range(nq):
            sl = slice(qi * qw, (qi + 1) * qw)
            s = S[:, sl]
            bs = _rowstack(s, hms).astype(BF16)
            vn = U[rows, sl] - _dot(W[rows, sl].astype(BF16), bs)
            o = _dot(QD[rows, sl].astype(BF16), bs) + pm(QK[rows, sl], vn)
            tfull = _dot_tn(KD[rows, sl].astype(BF16), vn.astype(BF16))
            upd = sum(jnp.where(hms[h], tfull[h * HEAD_DIM:(h + 1) * HEAD_DIM, :], 0.0) for h in range(4))
            S[:, sl] = s * al[:, sl] + upd
            o_ref[0, rows, sl] = o
        return carry

    lax.fori_loop(0, tt // C, inter, 0)
    s_out_ref[0] = S[...]


def _gdn_prompt(q, k, v, la, bt, *, tt):
    n, t, hw = q.shape
    C = GDN_CHUNK
    ltri = jnp.tril(jnp.ones((C, C), F32)).astype(BF16)
    ones = jnp.ones((C, C), BF16)
    row = pl.BlockSpec((1, tt, hw), lambda i, j: (i, j, 0))
    cst = pl.BlockSpec((C, C), lambda i, j: (0, 0))
    return pl.pallas_call(
        functools.partial(_gdn_kernel, tt=tt, hw=hw),
        grid=(n, t // tt),
        in_specs=[row] * 5 + [cst, cst],
        out_specs=[row, pl.BlockSpec((1, HEAD_DIM, hw), lambda i, j: (i, 0, 0))],
        out_shape=[jax.ShapeDtypeStruct((n, t, hw), F32), jax.ShapeDtypeStruct((n, HEAD_DIM, hw), F32)],
        scratch_shapes=[pltpu.VMEM((HEAD_DIM, hw), F32)] + [pltpu.VMEM((tt, hw), F32)] * 5
        + [pltpu.VMEM((tt // C * 8, hw), F32)],
        compiler_params=_cparams(("arbitrary", "arbitrary")),
        name="gdn_prompt",
    )(q, k, v, la, bt, ltri, ones)


def _gdn_step_kernel(q_ref, k_ref, v_ref, la_ref, bt_ref, s_ref, bones_ref, o_ref, s_out_ref, *, gs, hw):
    rowi = lax.broadcasted_iota(jnp.int32, (HEAD_DIM, hw), 0)
    colj = _lane_in_head(lax.broadcasted_iota(jnp.int32, (HEAD_DIM, hw), 1))
    eye = (colj == rowi).astype(F32)
    bones = bones_ref[...]
    for i in range(gs):
        r = slice(i, i + 1)
        q = q_ref[r, :] * (HEAD_DIM ** -0.5)
        k = k_ref[r, :]
        v = v_ref[r, :]
        eg = jnp.exp(la_ref[r, :])
        b = bt_ref[r, :]
        s = s_ref[i]
        kbc = _dot_f32_lhs(eye * k, bones)
        qbc = _dot_f32_lhs(eye * q, bones)
        ks = jnp.sum(kbc * s, axis=0, keepdims=True)
        qs = jnp.sum(qbc * s, axis=0, keepdims=True)
        qk = jnp.sum(kbc * qbc, axis=0, keepdims=True)
        vn = b * v - b * eg * ks
        o_ref[r, :] = eg * qs + qk * vn
        s_out_ref[i] = s * eg + kbc * vn


def _gdn_sample(q, k, v, la, bt, s_packed, bones, *, gs):
    m, hw = q.shape
    row = pl.BlockSpec((gs, hw), lambda i: (i, 0))
    st = pl.BlockSpec((gs, HEAD_DIM, hw), lambda i: (i, 0, 0))
    return pl.pallas_call(
        functools.partial(_gdn_step_kernel, gs=gs, hw=hw),
        grid=(m // gs,),
        in_specs=[row] * 5 + [st, pl.BlockSpec(bones.shape, lambda i: (0, 0))],
        out_specs=[row, st],
        out_shape=[jax.ShapeDtypeStruct((m, hw), F32), jax.ShapeDtypeStruct(s_packed.shape, F32)],
        compiler_params=_cparams(("arbitrary",)),
        name="gdn_sample",
    )(q, k, v, la, bt, s_packed, bones)


def _band_kernel(q_ref, k_ref, v_ref, kh_ref, vh_ref, o_ref, l_ref, kf, vf, *, tl):
    lt = pl.program_id(2)
    kf[0:BAND, :] = kh_ref[0].astype(BF16)
    kf[BAND:, :] = k_ref[0].astype(BF16)
    vf[0:BAND, :] = vh_ref[0].astype(BF16)
    vf[BAND:, :] = v_ref[0].astype(BF16)
    a = lax.broadcasted_iota(jnp.int32, (BAND, 2 * BAND), 0)
    c = lax.broadcasted_iota(jnp.int32, (BAND, 2 * BAND), 1)
    band = (c >= a) & (c <= a + BAND)
    h0 = lax.broadcasted_iota(jnp.int32, (BAND, LANES), 1) < HEAD_DIM

    def body(b, carry):
        r0 = pl.multiple_of(b * BAND, BAND)
        qb = q_ref[0, pl.ds(r0, BAND), :] * (HEAD_DIM ** -0.5)
        kk = kf[pl.ds(r0, 2 * BAND), :]
        vv = vf[pl.ds(r0, 2 * BAND), :]
        first = jnp.logical_and(lt == 0, b == 0)
        valid = band & (c >= jnp.where(first, BAND, 0))
        res = []
        for hh in range(2):
            qm = jnp.where(h0 if hh == 0 else jnp.logical_not(h0), qb, 0.0).astype(BF16)
            s = jnp.where(valid, _dot_nt(qm, kk), NEG)
            m = jnp.max(s, axis=-1, keepdims=True)
            p = jnp.exp(s - m)
            l = jnp.sum(p, axis=-1, keepdims=True)
            pv = _dot(p.astype(BF16), vv)
            res.append((pv / l, m + jnp.log(l)))
        o_ref[0, pl.ds(r0, BAND), :] = jnp.where(h0, res[0][0], res[1][0])
        l_ref[0, pl.ds(r0, BAND), :] = jnp.where(h0, res[0][1], res[1][1])
        return carry

    lax.fori_loop(0, tl // BAND, body, 0)


def _band(q, k, v, dil, *, tl_max=1024):
    n, t, hw = q.shape
    ln = t // dil
    tl = min(ln, tl_max)
    view = lambda x: x.reshape(n, ln, dil * hw)
    ncol = dil * hw // LANES
    main = pl.BlockSpec((1, tl, LANES), lambda i, c, j: (i, j, c))
    halo = pl.BlockSpec((1, BAND, LANES), lambda i, c, j: (i, jnp.maximum(j * (tl // BAND) - 1, 0), c))
    o, l = pl.pallas_call(
        functools.partial(_band_kernel, tl=tl),
        grid=(n, ncol, ln // tl),
        in_specs=[main, main, main, halo, halo],
        out_specs=[main, main],
        out_shape=[jax.ShapeDtypeStruct((n, ln, dil * hw), F32)] * 2,
        scratch_shapes=[pltpu.VMEM((tl + BAND, LANES), BF16)] * 2,
        compiler_params=_cparams(("arbitrary", "arbitrary", "arbitrary")),
        name=f"band_d{dil}",
    )(view(q), view(k), view(v), view(k), view(v))
    return o.reshape(n, t, hw), l.reshape(n, t, hw)


def _cache_attn_kernel(q_ref, kn_ref, vn_ref, k1, v1, k2, v2, k3, v3, o_ref, *, gs, hw):
    nh = hw // HEAD_DIM
    hmask = (_lane_head(lax.broadcasted_iota(jnp.int32, (nh, hw), 1))
             == lax.broadcasted_iota(jnp.int32, (nh, hw), 0))
    nd = len(DILATIONS)
    for i in range(gs):
        r = slice(i, i + 1)
        qt = jnp.where(hmask, q_ref[r, :] * (HEAD_DIM ** -0.5), 0.0)
        qtb = qt.astype(BF16)
        kn = kn_ref[r, :]
        vn = vn_ref[r, :]
        s0 = jnp.sum(qt * kn, axis=-1, keepdims=True)
        ss = [_dot_nt(qtb, kr[i].astype(BF16)) for kr in (k1, k2, k3)]
        m = s0
        for s in ss:
            m = jnp.maximum(m, jnp.max(s, axis=-1, keepdims=True))
        p0 = jnp.exp(s0 - m)
        den = nd * p0
        num = nd * p0 * vn
        for s, vr in zip(ss, (v1, v2, v3)):
            p = jnp.exp(s - m)
            den = den + jnp.sum(p, axis=-1, keepdims=True)
            num = num + _dot(p.astype(BF16), vr[i].astype(BF16))
        o_ref[r, :] = jnp.sum(jnp.where(hmask, num / den, 0.0), axis=0, keepdims=True)


def _cache_attn(q, kn, vn, ck, cv, *, gs):
    m, hw = q.shape
    n_past = ck.shape[1]
    row = pl.BlockSpec((gs, hw), lambda i: (i, 0))
    args, specs = [], []
    for window, dil in DILATIONS:
        assert window // dil == BAND and n_past % dil == 0 and n_past >= window
        ln = n_past // dil
        spec = pl.BlockSpec((gs, BAND, hw), lambda i, _b=ln // BAND - 1: (i, _b, 0))
        for cache in (ck, cv):
            args.append(cache.reshape(m, ln, dil * hw))
            specs.append(spec)
    return pl.pallas_call(
        functools.partial(_cache_attn_kernel, gs=gs, hw=hw),
        grid=(m // gs,),
        in_specs=[row] * 3 + specs,
        out_specs=row,
        out_shape=jax.ShapeDtypeStruct((m, hw), F32),
        compiler_params=_cparams(("arbitrary",)),
        name="cache_attn",
    )(q, kn, vn, *args)


def _post_kernel(*refs, tm, n_att):
    x_ref, og_ref, z_ref = refs[:3]
    att = refs[3:3 + 2 * n_att] if n_att > 1 else refs[3:4]
    rest = refs[3 + (2 * n_att if n_att > 1 else 1):]
    (ga_ref, gb_ref, bones_ref, wo_ref, gf_ref, wrh_ref, wrl_ref, br_ref,
     h_ref, route_ref, cnt_ref, cnt) = rest
    i = pl.program_id(0)

    @pl.when(i == 0)
    def _():
        cnt[...] = jnp.zeros_like(cnt)

    bones = bones_ref[...]
    oa = _head_rms(og_ref[...], bones, ga_ref[...]) * _silu(z_ref[...])
    if n_att > 1:
        os_, ls_ = att[:n_att], att[n_att:]
        lse = [l[...] for l in ls_]
        mx = functools.reduce(jnp.maximum, lse)
        es = [jnp.exp(l - mx) for l in lse]
        ob = sum(e * o[...] for e, o in zip(es, os_)) / sum(es)
    else:
        ob = att[0][...]
    ob = _head_rms(ob, bones, gb_ref[...])
    mix = jnp.concatenate([oa, ob], axis=-1).astype(BF16)
    h = x_ref[...] + _dot(mix, wo_ref[...])
    h_ref[...] = h

    mrow = _rms(h, gf_ref[...])
    mh = mrow.astype(BF16)
    ml = (mrow - mh.astype(F32)).astype(BF16)
    logit = _dot(mh, wrh_ref[...]) + _dot(mh, wrl_ref[...]) + _dot(ml, wrh_ref[...]) + br_ref[...]
    lane = lax.broadcasted_iota(jnp.int32, logit.shape, 1).astype(F32)
    gl = jnp.where(lane < N_GROUPS, logit, NEG)
    gmax = jnp.max(gl, axis=-1, keepdims=True)
    grp = jnp.min(jnp.where(gl == gmax, lane, 1e9), axis=-1, keepdims=True)
    pg = 1.0 / jnp.sum(jnp.exp(gl - gmax), axis=-1, keepdims=True)
    lo = N_GROUPS + grp * EXPERTS_PER_GROUP
    el = jnp.where((lane >= lo) & (lane < lo + EXPERTS_PER_GROUP), logit, NEG)
    v1 = jnp.max(el, axis=-1, keepdims=True)
    i1 = jnp.min(jnp.where(el == v1, lane, 1e9), axis=-1, keepdims=True)
    el2 = jnp.where(lane == i1, NEG, el)
    v2 = jnp.max(el2, axis=-1, keepdims=True)
    i2 = jnp.min(jnp.where(el2 == v2, lane, 1e9), axis=-1, keepdims=True)
    e = jnp.exp(v2 - v1)
    g1 = pg / (1.0 + e)
    g2 = pg * e / (1.0 + e)
    e1 = i1 - N_GROUPS
    e2 = i2 - N_GROUPS

    oh1 = lane == e1
    oh2 = lane == e2
    onehot = jnp.where(oh1 | oh2, 1.0, 0.0)
    ri = lax.broadcasted_iota(jnp.int32, (tm, tm), 0)
    ci = lax.broadcasted_iota(jnp.int32, (tm, tm), 1)
    tri = jnp.where(ci < ri, 1.0, 0.0).astype(BF16)
    before = _dot(tri, onehot.astype(BF16)) + cnt[...]
    r1 = jnp.sum(jnp.where(oh1, before, 0.0), axis=-1, keepdims=True)
    r2 = jnp.sum(jnp.where(oh2, before, 0.0), axis=-1, keepdims=True)
    cnt[...] = cnt[...] + jnp.sum(onehot, axis=0, keepdims=True)
    cnt_ref[...] = cnt[...]
    route = jnp.zeros_like(logit)
    for j, val in enumerate((e1, e2, r1, r2, g1, g2)):
        route = jnp.where(lane == j, val, route)
    route_ref[...] = route


def _post(x2, og, z, att, ga, gb, bones, wo, gf, wrh, wrl, br, *, tm):
    m, d = x2.shape
    hw = og.shape[1]
    n_att = len(att) // 2 if len(att) > 1 else 1
    row = lambda w: pl.BlockSpec((tm, w), lambda i: (i, 0))
    const = lambda a: pl.BlockSpec(a.shape, lambda i: (0,) * a.ndim)
    consts = [ga, gb, bones, wo, gf, wrh, wrl, br]
    return pl.pallas_call(
        functools.partial(_post_kernel, tm=tm, n_att=n_att),
        grid=(m // tm,),
        in_specs=[row(d), row(hw), row(hw)] + [row(hw)] * len(att) + [const(a) for a in consts],
        out_specs=[row(d), row(LANES), pl.BlockSpec((1, LANES), lambda i: (0, 0))],
        out_shape=[jax.ShapeDtypeStruct((m, d), F32), jax.ShapeDtypeStruct((m, LANES), F32),
                   jax.ShapeDtypeStruct((1, LANES), F32)],
        scratch_shapes=[pltpu.VMEM((1, LANES), F32)],
        compiler_params=_cparams(("arbitrary",)),
        name=f"post_{m}",
    )(x2, og, z, *att, *consts)


def _dispatch_kernel(h_ref, gf_ref, dest_ref, xin_ref, xbuf_ref, mrow, dsm, sem, dsem, *, tm):
    del xin_ref
    i = pl.program_id(0)
    cp = pltpu.make_async_copy(dest_ref.at[pl.ds(i, 1)], dsm, dsem)
    cp.start()
    mrow[...] = _rms(h_ref[...], gf_ref[...])
    cp.wait()

    def body(r, carry):
        for kk in range(2):
            d = dsm[0, 2 * r + kk]
            pltpu.make_async_copy(mrow.at[pl.ds(r, 1)], xbuf_ref.at[pl.ds(d, 1)], sem).start()
        return carry

    lax.fori_loop(0, tm, body, 0)
    for _ in range(2):
        pltpu.make_async_copy(mrow, xbuf_ref.at[pl.ds(0, tm)], sem).wait()


def _dispatch(h, gf, dest, n_slots, *, tm):
    m, d = h.shape
    xzero = jnp.zeros((n_slots, d), F32)
    return pl.pallas_call(
        functools.partial(_dispatch_kernel, tm=tm),
        grid=(m // tm,),
        in_specs=[pl.BlockSpec((tm, d), lambda i: (i, 0)), pl.BlockSpec((1, d), lambda i: (0, 0)),
                  pl.BlockSpec(memory_space=pl.ANY), pl.BlockSpec(memory_space=pl.ANY)],
        out_specs=pl.BlockSpec(memory_space=pl.ANY),
        out_shape=jax.ShapeDtypeStruct((n_slots, d), F32),
        scratch_shapes=[pltpu.VMEM((tm, d), F32), pltpu.SMEM((1, 2 * tm), jnp.int32),
                        pltpu.SemaphoreType.DMA, pltpu.SemaphoreType.DMA],
        input_output_aliases={3: 0},
        compiler_params=_cparams(("arbitrary",)),
        name=f"dispatch_{m}",
    )(h, gf, dest, xzero)


def _expert_kernel(be_ref, nu_ref, x_ref, wg_ref, wu_ref, wd_ref, y_ref):
    b = pl.program_id(0)

    @pl.when(b < nu_ref[0])
    def _():
        x = x_ref[...].astype(BF16)
        hid = _silu(_dot(x, wg_ref[0])) * _dot(x, wu_ref[0])
        y_ref[...] = _dot(hid.astype(BF16), wd_ref[0])

    @pl.when(b >= nu_ref[0])
    def _():
        y_ref[...] = jnp.zeros_like(y_ref)


def _experts(xbuf, blk_expert, n_used, wg, wu, wd, *, bm):
    n_slots, d = xbuf.shape
    de = wg.shape[2]
    grid_spec = pltpu.PrefetchScalarGridSpec(
        num_scalar_prefetch=2,
        grid=(n_slots // bm,),
        in_specs=[pl.BlockSpec((bm, d), lambda b, be, nu: (b, 0)),
                  pl.BlockSpec((1, d, de), lambda b, be, nu: (be[b], 0, 0)),
                  pl.BlockSpec((1, d, de), lambda b, be, nu: (be[b], 0, 0)),
                  pl.BlockSpec((1, de, d), lambda b, be, nu: (be[b], 0, 0))],
        out_specs=pl.BlockSpec((bm, d), lambda b, be, nu: (b, 0)),
    )
    return pl.pallas_call(
        _expert_kernel,
        grid_spec=grid_spec,
        out_shape=jax.ShapeDtypeStruct((n_slots, d), F32),
        compiler_params=_cparams(("arbitrary",)),
        name=f"experts_{n_slots}",
    )(blk_expert, n_used, xbuf, wg, wu, wd)


def _combine_kernel(h_ref, route_ref, p_ref, wpg_ref, wpp_ref, gfin_ref, dest_ref, ybuf_ref, y_ref,
                    y1, y2, dsm, sem, dsem, *, tm):
    i = pl.program_id(0)
    cp = pltpu.make_async_copy(dest_ref.at[pl.ds(i, 1)], dsm, dsem)
    cp.start()
    cp.wait()

    def body(r, carry):
        for kk, dst in enumerate((y1, y2)):
            d = dsm[0, 2 * r + kk]
            pltpu.make_async_copy(ybuf_ref.at[pl.ds(d, 1)], dst.at[pl.ds(r, 1)], sem).start()
        return carry

    lax.fori_loop(0, tm, body, 0)
    for dst in (y1, y2):
        pltpu.make_async_copy(ybuf_ref.at[pl.ds(0, tm)], dst, sem).wait()

    route = route_ref[...]
    g1 = route[:, 4:5]
    g2 = route[:, 5:6]
    h = h_ref[...] + (g1 * y1[...] + g2 * y2[...])
    gate = _sigmoid(_dot(h.astype(BF16), wpg_ref[...]))
    out = h + gate * _dot(p_ref[...].astype(BF16), wpp_ref[...])
    y_ref[...] = _rms(out, gfin_ref[...])


def _combine(h, route, p, wpg, wpp, gfin, dest, ybuf, *, tm):
    m, d = h.shape
    row = lambda w: pl.BlockSpec((tm, w), lambda i: (i, 0))
    const = lambda a: pl.BlockSpec(a.shape, lambda i: (0,) * a.ndim)
    return pl.pallas_call(
        functools.partial(_combine_kernel, tm=tm),
        grid=(m // tm,),
        in_specs=[row(d), row(LANES), row(p.shape[1]), const(wpg), const(wpp), const(gfin),
                  pl.BlockSpec(memory_space=pl.ANY), pl.BlockSpec(memory_space=pl.ANY)],
        out_specs=row(d),
        out_shape=jax.ShapeDtypeStruct((m, d), F32),
        scratch_shapes=[pltpu.VMEM((tm, d), F32), pltpu.VMEM((tm, d), F32), pltpu.SMEM((1, 2 * tm), jnp.int32),
                        pltpu.SemaphoreType.DMA, pltpu.SemaphoreType.DMA],
        compiler_params=_cparams(("arbitrary",)),
        name=f"combine_{m}",
    )(h, route, p, wpg, wpp, gfin, dest, ybuf)


def _tail(x2, og, z, att, p2, wts, *, tm, bm):
    m, d = x2.shape
    h, route, counts = _post(x2, og, z, att, wts["ga"], wts["gb"], wts["bones"], wts["wo"], wts["gf"],
                             wts["wrh"], wts["wrl"], wts["br"], tm=tm)
    eid = route[:, 0:2].astype(jnp.int32)
    rank = route[:, 2:4].astype(jnp.int32)
    sizes = counts[0, :N_EXPERTS].astype(jnp.int32)
    padded = (sizes + bm - 1) // bm * bm
    pend = jnp.cumsum(padded)
    pstart = pend - padded
    dest = (pstart[eid] + rank).reshape(m // tm, 2 * tm)
    n_blk = (2 * m) // bm + N_EXPERTS
    blk_expert = jnp.minimum(jnp.searchsorted(pend, jnp.arange(n_blk, dtype=jnp.int32) * bm, side="right"),
                             N_EXPERTS - 1).astype(jnp.int32)
    n_used = (pend[-1:] // bm).astype(jnp.int32)
    xbuf = _dispatch(h, wts["gf"], dest, n_blk * bm, tm=tm)
    ybuf = _experts(xbuf, blk_expert, n_used, wts["wg"], wts["wu"], wts["wd"], bm=bm)
    return _combine(h, route, p2, wts["wpg"], wts["wpp"], wts["gfin"], dest, ybuf, tm=tm)


def _rope_tables(pos):
    half = ROT_DIM // 2
    inv = ROPE_THETA ** (-jnp.arange(half, dtype=F32) * (2.0 / ROT_DIM))
    ang = pos.astype(F32)[:, None] * inv[None, :]
    cos, sin = jnp.cos(ang), jnp.sin(ang)
    j = jnp.arange(LANES) % HEAD_DIM
    first, second = j < half, (j >= half) & (j < ROT_DIM)
    jj = jnp.where(second, j - half, jnp.where(first, j, 0))
    c = jnp.where((first | second)[None, :], cos[:, jj], 1.0)
    s1 = jnp.where(first[None, :], -sin[:, jj], 0.0)
    s2 = jnp.where(second[None, :], sin[:, jj], 0.0)
    return c, s1, s2


def kernel(x_prompt, x_sample, cache_win_k, cache_win_v, state_conv, state_delta, p_prompt, p_sample, g_attn_norm, w_in, w_conv, a_log, dt_bias, g_a_out, g_b_out, w_out, g_ffn_norm, w_router_group, b_router_group, w_router_expert, b_router_expert, w_exp_gate, w_exp_up, w_exp_down, w_ple_gate, w_ple_proj, g_final):
    n, t, d = x_prompt.shape
    ns = x_sample.shape[0]
    assert w_in.shape[0] == 1 and x_sample.shape[1] == 1
    ha = a_log.shape[1]
    aw = ha * HEAD_DIM
    off_a = 4 * aw
    off_win = off_a + 2 * ha
    n_past = cache_win_k.shape[2]
    keep = min(DILATIONS[-1][0], t)
    hi = lambda a: a.astype(F32)

    w = w_in[0]
    w_ab = jnp.pad(w[:, off_a:off_win], ((0, 0), (0, LANES - 2 * ha)))
    w_cat = jnp.concatenate([w[:, :off_a], w[:, off_win:], w_ab], axis=1).astype(BF16)
    pad_l = lambda v: jnp.pad(hi(v), (0, LANES - v.shape[0]))[None, :]
    alog = pad_l(a_log[0])
    dtb = pad_l(dt_bias[0])
    lane_head = jnp.arange(2 * aw) // HEAD_DIM
    esel = (jnp.arange(LANES)[:, None] == lane_head[None, :]).astype(BF16)
    hd = jnp.arange(aw) // HEAD_DIM
    bones = (hd[:, None] == hd[None, :]).astype(BF16)
    wr = jnp.concatenate([hi(w_router_group[0]), hi(w_router_expert[0]).reshape(d, N_EXPERTS)], axis=1)
    wr = jnp.pad(wr, ((0, 0), (0, LANES - wr.shape[1])))
    wrh = wr.astype(BF16)
    wrl = (wr - wrh.astype(F32)).astype(BF16)
    br = jnp.concatenate([hi(b_router_group[0]), hi(b_router_expert[0]).reshape(N_EXPERTS)])
    wts = dict(
        ga=jnp.tile(hi(g_a_out[0]), ha)[None, :], gb=jnp.tile(hi(g_b_out[0]), ha)[None, :], bones=bones,
        wo=w_out[0].astype(BF16), gf=hi(g_ffn_norm[0])[None, :], wrh=wrh, wrl=wrl, br=pad_l(br),
        wg=w_exp_gate[0].astype(BF16), wu=w_exp_up[0].astype(BF16), wd=w_exp_down[0].astype(BF16),
        wpg=w_ple_gate[0].astype(BF16), wpp=w_ple_proj[0].astype(BF16), gfin=hi(g_final)[None, :])
    g_attn = hi(g_attn_norm[0])[None, :]
    wconv = hi(w_conv[0])

    tm_p = min(256, t)
    cos, s1, s2 = _rope_tables(jnp.arange(t, dtype=jnp.int32))
    qa, ka, va, z, la, bt, qb, kb, vb, conv_p = _proj(
        x_prompt, g_attn, w_cat, wconv, alog, dtb, esel, cos, s1, s2, bones, None, tm=tm_p, sample=False)
    og, s_fin = _gdn_prompt(qa, ka, va, la, bt, tt=min(512, t))
    att_o, att_l = [], []
    for _, dil in DILATIONS:
        o_c, l_c = _band(qb, kb, vb, dil)
        att_o.append(o_c.reshape(n * t, aw))
        att_l.append(l_c.reshape(n * t, aw))
    flat = lambda a: a.reshape(n * t, a.shape[-1])
    y_prompt = _tail(flat(x_prompt), flat(og), flat(z), att_o + att_l, flat(p_prompt[0]), wts,
                     tm=tm_p, bm=256).reshape(n, t, d)

    cos, s1, s2 = _rope_tables(jnp.full((1,), PAST_LEN, jnp.int32))
    st = [state_conv[0][None, :, j, :] for j in range(state_conv.shape[2])]
    xs3 = x_sample.reshape(1, ns, d)
    qa_s, ka_s, va_s, z_s, la_s, bt_s, qb_s, kb_s, vb_s, ua_s = [
        a[0] for a in _proj(xs3, g_attn, w_cat, wconv, alog, dtb, esel, cos, s1, s2, bones, st,
                            tm=ns, sample=True)]
    s_packed = state_delta[0].transpose(0, 2, 1, 3).reshape(ns, HEAD_DIM, aw)
    og_s, s_new = _gdn_sample(qa_s, ka_s, va_s, la_s, bt_s, s_packed, bones, gs=8)
    ob_s = _cache_attn(qb_s, kb_s, vb_s, cache_win_k[0].reshape(ns, n_past, aw),
                       cache_win_v[0].reshape(ns, n_past, aw), gs=8)
    y_sample = _tail(x_sample.reshape(ns, d), og_s, z_s, [ob_s], p_sample[0].reshape(ns, -1), wts,
                     tm=ns, bm=128).reshape(ns, 1, d)

    unpack = lambda s: s.reshape(-1, HEAD_DIM, ha, HEAD_DIM).transpose(0, 2, 1, 3)[None]
    heads = lambda a: a.reshape(a.shape[0], -1, ha, HEAD_DIM)
    return (y_prompt, y_sample,
            heads(kb[:, t - keep:])[None], heads(vb[:, t - keep:])[None],
            conv_p[:, 8 - state_conv.shape[2]:][None], unpack(s_fin),
            heads(kb_s[:, None])[None], heads(vb_s[:, None])[None],
            jnp.concatenate([state_conv[0][:, 1:], ua_s[:, None]], axis=1)[None], unpack(s_new))
```

```python
import functools
import math

import jax
import jax.numpy as jnp
from jax import lax
from jax.experimental import pallas as pl
from jax.experimental.pallas import tpu as pltpu

F32 = jnp.float32
BF16 = jnp.bfloat16

HEAD_DIM = 64
GDN_CHUNK = 64
ROT_DIM = HEAD_DIM // 4
ROPE_THETA = 500000.0
PAST_LEN = 8192
DILATIONS = ((128, 1), (512, 4), (2048, 16))
BAND = 128
N_GROUPS = 4
EXPERTS_PER_GROUP = 8
N_EXPERTS = N_GROUPS * EXPERTS_PER_GROUP
NORM_EPS = 1e-6
NEG = -1e30
LANES = 128
VMEM_LIMIT = 56 * 1024 * 1024


def _cparams(sem, row_dma=False):
    return pltpu.CompilerParams(dimension_semantics=sem, vmem_limit_bytes=VMEM_LIMIT,
                                disable_bounds_checks=row_dma)


def _dot(a, b):
    return jnp.dot(a, b, preferred_element_type=F32)


def _dot_nt(a, b):
    return lax.dot_general(a, b, (((1,), (1,)), ((), ())), preferred_element_type=F32)


def _dot_tn(a, b):
    return lax.dot_general(a, b, (((0,), (0,)), ((), ())), preferred_element_type=F32)


def _split3(x):
    hi = x.astype(BF16)
    r = x - hi.astype(F32)
    mid = r.astype(BF16)
    lo = (r - mid.astype(F32)).astype(BF16)
    return hi, mid, lo


def _dot_f32_lhs(x, w_bf16):
    hi, mid, lo = _split3(x)
    return _dot(hi, w_bf16) + _dot(mid, w_bf16) + _dot(lo, w_bf16)


def _dot_f32_rhs(w_bf16, x):
    hi, mid, lo = _split3(x)
    return _dot(w_bf16, hi) + _dot(w_bf16, mid) + _dot(w_bf16, lo)


HEAD_SHIFT = HEAD_DIM.bit_length() - 1


def _lane_head(i):
    return lax.shift_right_logical(i, HEAD_SHIFT)


def _lane_in_head(i):
    return lax.bitwise_and(i, HEAD_DIM - 1)


def _sigmoid(x):
    return 1.0 / (1.0 + jnp.exp(-x))


def _silu(x):
    return x * _sigmoid(x)


def _softplus(x):
    return jnp.maximum(x, 0.0) + jnp.log1p(jnp.exp(-jnp.abs(x)))


def _rms(x, g):
    return x * lax.rsqrt(jnp.mean(x * x, axis=-1, keepdims=True) + NORM_EPS) * g


def _head_rms(x, bones, g):
    ss = _dot((x * x).astype(BF16), bones)
    return x * lax.rsqrt(ss * (1.0 / HEAD_DIM) + NORM_EPS) * g


def _proj_kernel(*refs, tm, aw, sample):
    if sample:
        (x_ref, g_ref, w_ref, wc_ref, alog_ref, dtb_ref, esel_ref, cos_ref, s1_ref, s2_ref, bones_ref,
         st0_ref, st1_ref, st2_ref,
         qa_ref, ka_ref, va_ref, z_ref, la_ref, bt_ref, qb_ref, kb_ref, vb_ref, conv_ref) = refs
        buf = None
    else:
        (x_ref, g_ref, w_ref, wc_ref, alog_ref, dtb_ref, esel_ref, cos_ref, s1_ref, s2_ref, bones_ref,
         qa_ref, ka_ref, va_ref, z_ref, la_ref, bt_ref, qb_ref, kb_ref, vb_ref, conv_ref, buf) = refs
        t = pl.program_id(1)

        @pl.when(t == 0)
        def _():
            buf[0:8, :] = jnp.zeros((8, 3 * aw), F32)

    xn = _rms(x_ref[0], g_ref[...]).astype(BF16)
    bones = bones_ref[...]

    outs_a = (qa_ref, ka_ref, va_ref)
    for c in range(3):
        cs = slice(c * aw, (c + 1) * aw)
        u = _dot(xn, w_ref[:, cs])
        if sample:
            conv_ref[0, :, cs] = u
            y = (wc_ref[3:4, cs] * u + wc_ref[2:3, cs] * st2_ref[0, :, cs]
                 + wc_ref[1:2, cs] * st1_ref[0, :, cs] + wc_ref[0:1, cs] * st0_ref[0, :, cs])
        else:
            buf[8:8 + tm, cs] = u
            y = (wc_ref[3:4, cs] * u + wc_ref[2:3, cs] * buf[7:7 + tm, cs]
                 + wc_ref[1:2, cs] * buf[6:6 + tm, cs] + wc_ref[0:1, cs] * buf[5:5 + tm, cs])
        y = _silu(y)
        if c < 2:
            ss = _dot((y * y).astype(BF16), bones)
            y = y * lax.rsqrt(ss + NORM_EPS)
        outs_a[c][0] = y
    if not sample:
        tail = buf[tm:tm + 8, :]
        conv_ref[0] = tail
        buf[0:8, :] = tail

    z_ref[0] = _dot(xn, w_ref[:, 3 * aw:4 * aw])

    ab = _dot(xn, w_ref[:, 7 * aw:7 * aw + LANES])
    lane = lax.broadcasted_iota(jnp.int32, ab.shape, 1)
    log_a = -jnp.exp(alog_ref[...]) * _softplus(ab + dtb_ref[...])
    comb = jnp.where(lane < aw // HEAD_DIM, log_a, _sigmoid(ab))
    ex = _dot_f32_lhs(comb, esel_ref[...])
    la_ref[0] = ex[:, :aw]
    bt_ref[0] = ex[:, aw:]

    cosv, s1v, s2v = cos_ref[...], s1_ref[...], s2_ref[...]
    for c, oref in ((0, qb_ref), (1, kb_ref)):
        for gq in range(aw // LANES):
            cs = slice(4 * aw + c * aw + gq * LANES, 4 * aw + c * aw + (gq + 1) * LANES)
            u = _dot(xn, w_ref[:, cs])
            r = (u * cosv + pltpu.roll(u, LANES - ROT_DIM // 2, 1) * s1v
                 + pltpu.roll(u, ROT_DIM // 2, 1) * s2v)
            oref[0, :, gq * LANES:(gq + 1) * LANES] = r
    vb_ref[0] = _dot(xn, w_ref[:, 6 * aw:7 * aw])


def _proj(x3, g, w_cat, wconv, alog, dtb, esel, cos, s1, s2, bones, states, *, tm, sample):
    n, t, d = x3.shape
    aw = bones.shape[0]
    nt = t // tm
    const = lambda shape: pl.BlockSpec(shape, lambda i, j: (0,) * len(shape))
    row = lambda width: pl.BlockSpec((1, tm, width), lambda i, j: (i, j, 0))
    tab = (pl.BlockSpec((1, LANES), lambda i, j: (0, 0)) if sample
           else pl.BlockSpec((tm, LANES), lambda i, j: (j, 0)))
    in_specs = [row(d), const((1, d)), const(w_cat.shape), const(wconv.shape), const((1, LANES)),
                const((1, LANES)), const(esel.shape), tab, tab, tab, const(bones.shape)]
    args = [x3, g, w_cat, wconv, alog, dtb, esel, cos, s1, s2, bones]
    if sample:
        in_specs += [row(3 * aw)] * 3
        args += list(states)
        conv_shape = jax.ShapeDtypeStruct((n, t, 3 * aw), F32)
        conv_spec = row(3 * aw)
        scratch = []
    else:
        conv_shape = jax.ShapeDtypeStruct((n, 8, 3 * aw), F32)
        conv_spec = pl.BlockSpec((1, 8, 3 * aw), lambda i, j: (i, 0, 0))
        scratch = [pltpu.VMEM((tm + 8, 3 * aw), F32)]
    o = jax.ShapeDtypeStruct((n, t, aw), F32)
    return pl.pallas_call(
        functools.partial(_proj_kernel, tm=tm, aw=aw, sample=sample),
        grid=(n, nt),
        in_specs=in_specs,
        out_specs=[row(aw)] * 9 + [conv_spec],
        out_shape=[o] * 9 + [conv_shape],
        scratch_shapes=scratch,
        compiler_params=_cparams(("arbitrary", "arbitrary")),
        name="proj_sample" if sample else "proj_prompt",
    )(*args)


def _rowstack(x, hms):
    return jnp.concatenate([jnp.where(hm, x, 0.0) for hm in hms], axis=0)


def _gdn_kernel(q_ref, k_ref, v_ref, la_ref, bt_ref, ltri_ref, ones_ref, o_ref, s_out_ref,
                S, KW, NN, QP, OU, AL, *, tt, hw, unroll):
    C = GDN_CHUNK
    qw = 4 * HEAD_DIM
    nq = hw // qw
    t = pl.program_id(1)

    @pl.when(t == 0)
    def _():
        S[...] = jnp.zeros_like(S)

    rowi = lax.broadcasted_iota(jnp.int32, (C, hw), 0)
    colj = _lane_in_head(lax.broadcasted_iota(jnp.int32, (C, hw), 1))
    m_incl = colj <= rowi
    m_strict = colj < rowi
    eye = (colj == rowi).astype(F32)
    lane_q = _lane_head(lax.broadcasted_iota(jnp.int32, (1, qw), 1))
    hms = [lane_q == h for h in range(4)]
    ltri = ltri_ref[...]
    ones = ones_ref[...]

    def pm(l, r):
        return _dot(l.astype(BF16), _rowstack(r, hms).astype(BF16))

    def diag_blocks(full):
        return sum(jnp.where(hms[h], full[h * HEAD_DIM:(h + 1) * HEAD_DIM, :], 0.0) for h in range(4))

    sls = [slice(qi * qw, (qi + 1) * qw) for qi in range(nq)]

    def intra(it, carry):
        cs = [it * unroll + uu for uu in range(unroll)]
        rws = [pl.ds(pl.multiple_of(c * C, C), C) for c in cs]
        qs = [q_ref[0, r, :] * (HEAD_DIM ** -0.5) for r in rws]
        ks = [k_ref[0, r, :] for r in rws]
        bs = [bt_ref[0, r, :] for r in rws]
        gs = [_dot_f32_rhs(ltri, la_ref[0, r, :]) for r in rws]
        grs = [_dot_f32_rhs(ones, g * eye) for g in gs]
        decs = [jnp.exp(jnp.where(m_incl, g - gr, NEG)) for g, gr in zip(gs, grs)]
        egs = [jnp.exp(g) for g in gs]
        kbs = [k * b for k, b in zip(ks, bs)]
        for c, g in zip(cs, gs):
            AL[pl.ds(pl.multiple_of(c * 8, 8), 8), :] = jnp.broadcast_to(jnp.exp(g[C - 1:C, :]), (8, hw))
        ch = [(ui, sl) for ui in range(unroll) for sl in sls]
        kst = [_rowstack(ks[ui][:, sl], hms).astype(BF16) for ui, sl in ch]
        kk = [_dot_nt(kbs[ui][:, sl].astype(BF16), kst[i]) for i, (ui, sl) in enumerate(ch)]
        x = [-jnp.where(m_strict[:, sl], kk[i] * decs[ui][:, sl], 0.0) for i, (ui, sl) in enumerate(ch)]
        p = [eye[:, sl] + x[i] for i, (ui, sl) in enumerate(ch)]
        for _ in range(int(math.log2(C)) - 1):
            x = [pm(xi, xi) for xi in x]
            px = [pm(pi, xi) for pi, xi in zip(p, x)]
            p = [pi + pxi for pi, pxi in zip(p, px)]
        u = [pm(p[i], (v_ref[0, rws[ui], sl] * bs[ui][:, sl])) for i, (ui, sl) in enumerate(ch)]
        w = [pm(p[i], kbs[ui][:, sl] * egs[ui][:, sl]) for i, (ui, sl) in enumerate(ch)]
        qk = [jnp.where(m_incl[:, sl], _dot_nt(qs[ui][:, sl].astype(BF16), kst[i]) * decs[ui][:, sl], 0.0)
              .astype(BF16) for i, (ui, sl) in enumerate(ch)]
        kuw = []
        for i, (ui, sl) in enumerate(ch):
            g = gs[ui][:, sl]
            kd = (ks[ui][:, sl] * jnp.exp(g[C - 1:C, :] - g)).astype(BF16)
            uw = jnp.concatenate([u[i], w[i]], axis=1).astype(BF16)
            kuw.append(_dot_tn(kd, uw))
        ou = [_dot(qk[i], _rowstack(u[i], hms).astype(BF16)) for i in range(len(ch))]
        qw_ = [_dot(qk[i], _rowstack(w[i], hms).astype(BF16)) for i in range(len(ch))]
        for i, (ui, sl) in enumerate(ch):
            krows = pl.ds(pl.multiple_of(cs[ui] * HEAD_DIM, HEAD_DIM), HEAD_DIM)
            NN[krows, sl] = diag_blocks(kuw[i][:, :qw])
            KW[krows, sl] = diag_blocks(kuw[i][:, qw:])
            OU[rws[ui], sl] = ou[i]
            QP[rws[ui], sl] = qs[ui][:, sl] * egs[ui][:, sl] - qw_[i]
        return carry

    lax.fori_loop(0, tt // C // unroll, intra, 0)

    def inter(c, carry):
        rows = pl.ds(pl.multiple_of(c * C, C), C)
        krows = pl.ds(pl.multiple_of(c * HEAD_DIM, HEAD_DIM), HEAD_DIM)
        al = AL[pl.ds(pl.multiple_of(c * 8, 8), 1), :]
        ss = [S[:, sl] for sl in sls]
        lhs = [jnp.concatenate([KW[krows, sl], QP[rows, sl]], axis=0).astype(BF16) for sl in sls]
        rs = [_dot(l, _rowstack(s, hms).astype(BF16)) for l, s in zip(lhs, ss)]
        for sl, s, r in zip(sls, ss, rs):
            S[:, sl] = s * al[:, sl] - r[:HEAD_DIM] + NN[krows, sl]
            o_ref[0, rows, sl] = r[HEAD_DIM:] + OU[rows, sl]
        return carry

    lax.fori_loop(0, tt // C, inter, 0)
    s_out_ref[0] = S[...]


def _gdn_prompt(q, k, v, la, bt, *, tt, unroll):
    n, t, hw = q.shape
    C = GDN_CHUNK
    ltri = jnp.tril(jnp.ones((C, C), F32)).astype(BF16)
    ones = jnp.ones((C, C), BF16)
    row = pl.BlockSpec((1, tt, hw), lambda i, j: (i, j, 0))
    cst = pl.BlockSpec((C, C), lambda i, j: (0, 0))
    nch = tt // C
    return pl.pallas_call(
        functools.partial(_gdn_kernel, tt=tt, hw=hw, unroll=unroll),
        grid=(n, t // tt),
        in_specs=[row] * 5 + [cst, cst],
        out_specs=[row, pl.BlockSpec((1, HEAD_DIM, hw), lambda i, j: (i, 0, 0))],
        out_shape=[jax.ShapeDtypeStruct((n, t, hw), F32), jax.ShapeDtypeStruct((n, HEAD_DIM, hw), F32)],
        scratch_shapes=[pltpu.VMEM((HEAD_DIM, hw), F32), pltpu.VMEM((nch * HEAD_DIM, hw), F32),
                        pltpu.VMEM((nch * HEAD_DIM, hw), F32), pltpu.VMEM((tt, hw), F32),
                        pltpu.VMEM((tt, hw), F32), pltpu.VMEM((nch * 8, hw), F32)],
        compiler_params=_cparams(("arbitrary", "arbitrary")),
        name="gdn_prompt",
    )(q, k, v, la, bt, ltri, ones)


def _gdn_step_kernel(q_ref, k_ref, v_ref, la_ref, bt_ref, s_ref, bones_ref, o_ref, s_out_ref, *, gs, hw):
    rowi = lax.broadcasted_iota(jnp.int32, (HEAD_DIM, hw), 0)
    colj = _lane_in_head(lax.broadcasted_iota(jnp.int32, (HEAD_DIM, hw), 1))
    eye = (colj == rowi).astype(F32)
    bones = bones_ref[...]
    for i in range(gs):
        r = slice(i, i + 1)
        q = q_ref[r, :] * (HEAD_DIM ** -0.5)
        k = k_ref[r, :]
        v = v_ref[r, :]
        eg = jnp.exp(la_ref[r, :])
        b = bt_ref[r, :]
        s = s_ref[i]
        kbc = _dot_f32_lhs(eye * k, bones)
        qbc = _dot_f32_lhs(eye * q, bones)
        ks = jnp.sum(kbc * s, axis=0, keepdims=True)
        qs = jnp.sum(qbc * s, axis=0, keepdims=True)
        qk = jnp.sum(kbc * qbc, axis=0, keepdims=True)
        vn = b * v - b * eg * ks
        o_ref[r, :] = eg * qs + qk * vn
        s_out_ref[i] = s * eg + kbc * vn


def _gdn_sample(q, k, v, la, bt, s_packed, bones, *, gs):
    m, hw = q.shape
    row = pl.BlockSpec((gs, hw), lambda i: (i, 0))
    st = pl.BlockSpec((gs, HEAD_DIM, hw), lambda i: (i, 0, 0))
    return pl.pallas_call(
        functools.partial(_gdn_step_kernel, gs=gs, hw=hw),
        grid=(m // gs,),
        in_specs=[row] * 5 + [st, pl.BlockSpec(bones.shape, lambda i: (0, 0))],
        out_specs=[row, st],
        out_shape=[jax.ShapeDtypeStruct((m, hw), F32), jax.ShapeDtypeStruct(s_packed.shape, F32)],
        compiler_params=_cparams(("arbitrary",)),
        name="gdn_sample",
    )(q, k, v, la, bt, s_packed, bones)


def _band_kernel(q_ref, k_ref, v_ref, o_ref, kf, vf, oc, lc, *, tt, unroll):
    j = pl.program_id(2)

    @pl.when(j == 0)
    def _():
        kf[0:tt, :] = jnp.zeros((tt, LANES), F32)
        vf[0:tt, :] = jnp.zeros((tt, LANES), F32)

    kf[tt:, :] = k_ref[0]
    vf[tt:, :] = v_ref[0]
    a = lax.bitwise_and(lax.broadcasted_iota(jnp.int32, (2 * BAND, 2 * BAND), 0), BAND - 1)
    c = lax.broadcasted_iota(jnp.int32, (2 * BAND, 2 * BAND), 1)
    band = (c >= a) & (c <= a + BAND)
    row_h1 = lax.broadcasted_iota(jnp.int32, (2 * BAND, LANES), 0) >= BAND
    lane_h1 = lax.broadcasted_iota(jnp.int32, (2 * BAND, LANES), 1) >= HEAD_DIM
    own = row_h1 == lane_h1
    out_h1 = lax.broadcasted_iota(jnp.int32, (BAND, LANES), 1) >= HEAD_DIM

    for ci, (_, d) in enumerate(DILATIONS):
        nblk = tt // (BAND * d)

        def body(it, carry, d=d, ci=ci, nblk=nblk):
            rows, krows, firsts = [], [], []
            for uu in range(unroll):
                idx = it * unroll + uu
                r = idx // nblk
                b = idx % nblk
                start = b * (BAND * d) + r
                if d == 1:
                    rows.append(pl.ds(start, BAND))
                    krows.append(pl.ds(tt + start - BAND, 2 * BAND))
                else:
                    rows.append(pl.ds(start, BAND, stride=d))
                    krows.append(pl.ds(tt + start - BAND * d, 2 * BAND, stride=d))
                firsts.append(jnp.logical_and(j == 0, b == 0))
            ss = []
            for rw, kr in zip(rows, krows):
                qb = q_ref[0, rw, :] * (HEAD_DIM ** -0.5)
                qs = jnp.where(own, jnp.concatenate([qb, qb], axis=0), 0.0).astype(BF16)
                ss.append(_dot_nt(qs, kf[kr, :].astype(BF16)))
            pcat, stats = [], []
            for s, first in zip(ss, firsts):
                s = jnp.where(band & (c >= jnp.where(first, BAND, 0)), s, NEG)
                ps, st = [], []
                for hh in range(2):
                    sh = s[hh * BAND:(hh + 1) * BAND]
                    m = jnp.max(sh, axis=-1, keepdims=True)
                    p = jnp.exp(sh - m)
                    l = jnp.sum(p, axis=-1, keepdims=True)
                    ps.append(p.astype(BF16))
                    st.append((l, m + jnp.log(l)))
                pcat.append(jnp.concatenate(ps, axis=0))
                stats.append(st)
            pvs = [_dot(pc, vf[kr, :].astype(BF16)) for pc, kr in zip(pcat, krows)]
            for rw, pv, st in zip(rows, pvs, stats):
                oc[ci, rw, :] = jnp.where(out_h1, pv[BAND:] / st[1][0], pv[:BAND] / st[0][0])
                lc[ci, rw, :] = jnp.where(out_h1, st[1][1], st[0][1])
            return carry

        lax.fori_loop(0, tt // BAND // unroll, body, 0)

    mrows = 256
    for ch in range(tt // mrows):
        rs = slice(ch * mrows, (ch + 1) * mrows)
        ls = [lc[ci, rs, :] for ci in range(len(DILATIONS))]
        mx = functools.reduce(jnp.maximum, ls)
        es = [jnp.exp(l - mx) for l in ls]
        o_ref[0, rs, :] = sum(e * oc[ci, rs, :] for ci, e in enumerate(es)) / sum(es)

    kf[0:tt, :] = kf[tt:, :]
    vf[0:tt, :] = vf[tt:, :]


def _band(q, k, v, *, tt, unroll):
    n, t, hw = q.shape
    assert all(w // d == BAND and tt % w == 0 for w, d in DILATIONS) and t % tt == 0
    blk = pl.BlockSpec((1, tt, LANES), lambda i, c, j: (i, j, c))
    nd = len(DILATIONS)
    return pl.pallas_call(
        functools.partial(_band_kernel, tt=tt, unroll=unroll),
        grid=(n, hw // LANES, t // tt),
        in_specs=[blk, blk, blk],
        out_specs=blk,
        out_shape=jax.ShapeDtypeStruct((n, t, hw), F32),
        scratch_shapes=[pltpu.VMEM((2 * tt, LANES), F32), pltpu.VMEM((2 * tt, LANES), F32),
                        pltpu.VMEM((nd, tt, LANES), F32), pltpu.VMEM((nd, tt, LANES), F32)],
        compiler_params=_cparams(("arbitrary", "arbitrary", "arbitrary")),
        name="band",
    )(q, k, v)


def _cache_attn_kernel(q_ref, kn_ref, vn_ref, k1, v1, k2, v2, k3, v3, o_ref, *, gs, hw):
    nh = hw // HEAD_DIM
    hmask = (_lane_head(lax.broadcasted_iota(jnp.int32, (nh, hw), 1))
             == lax.broadcasted_iota(jnp.int32, (nh, hw), 0))
    nd = len(DILATIONS)
    for i in range(gs):
        r = slice(i, i + 1)
        qt = jnp.where(hmask, q_ref[r, :] * (HEAD_DIM ** -0.5), 0.0)
        qtb = qt.astype(BF16)
        kn = kn_ref[r, :]
        vn = vn_ref[r, :]
        s0 = jnp.sum(qt * kn, axis=-1, keepdims=True)
        ss = [_dot_nt(qtb, kr[i].astype(BF16)) for kr in (k1, k2, k3)]
        m = s0
        for s in ss:
            m = jnp.maximum(m, jnp.max(s, axis=-1, keepdims=True))
        p0 = jnp.exp(s0 - m)
        den = nd * p0
        num = nd * p0 * vn
        for s, vr in zip(ss, (v1, v2, v3)):
            p = jnp.exp(s - m)
            den = den + jnp.sum(p, axis=-1, keepdims=True)
            num = num + _dot(p.astype(BF16), vr[i].astype(BF16))
        o_ref[r, :] = jnp.sum(jnp.where(hmask, num / den, 0.0), axis=0, keepdims=True)


def _cache_attn(q, kn, vn, ck, cv, *, gs):
    m, hw = q.shape
    n_past = ck.shape[1]
    row = pl.BlockSpec((gs, hw), lambda i: (i, 0))
    args, specs = [], []
    for window, dil in DILATIONS:
        assert window // dil == BAND and n_past % dil == 0 and n_past >= window
        ln = n_past // dil
        spec = pl.BlockSpec((gs, BAND, hw), lambda i, _b=ln // BAND - 1: (i, _b, 0))
        for cache in (ck, cv):
            args.append(cache.reshape(m, ln, dil * hw))
            specs.append(spec)
    return pl.pallas_call(
        functools.partial(_cache_attn_kernel, gs=gs, hw=hw),
        grid=(m // gs,),
        in_specs=[row] * 3 + specs,
        out_specs=row,
        out_shape=jax.ShapeDtypeStruct((m, hw), F32),
        compiler_params=_cparams(("arbitrary",)),
        name="cache_attn",
    )(q, kn, vn, *args)


def _post_kernel(x_ref, og_ref, z_ref, ob_ref, ga_ref, gb_ref, bones_ref, wo_ref, gf_ref, wrh_ref, wrl_ref,
                 br_ref, h_ref, route_ref, cnt_ref, cnt, *, tm):
    i = pl.program_id(0)

    @pl.when(i == 0)
    def _():
        cnt[...] = jnp.zeros_like(cnt)

    bones = bones_ref[...]
    oa = _head_rms(og_ref[...], bones, ga_ref[...]) * _silu(z_ref[...])
    ob = _head_rms(ob_ref[...], bones, gb_ref[...])
    mix = jnp.concatenate([oa, ob], axis=-1).astype(BF16)
    h = x_ref[...] + _dot(mix, wo_ref[...])
    h_ref[...] = h

    mrow = _rms(h, gf_ref[...])
    mh = mrow.astype(BF16)
    ml = (mrow - mh.astype(F32)).astype(BF16)
    logit = _dot(mh, wrh_ref[...]) + _dot(mh, wrl_ref[...]) + _dot(ml, wrh_ref[...]) + br_ref[...]
    lane = lax.broadcasted_iota(jnp.int32, logit.shape, 1).astype(F32)
    gl = jnp.where(lane < N_GROUPS, logit, NEG)
    gmax = jnp.max(gl, axis=-1, keepdims=True)
    grp = jnp.min(jnp.where(gl == gmax, lane, 1e9), axis=-1, keepdims=True)
    pg = 1.0 / jnp.sum(jnp.exp(gl - gmax), axis=-1, keepdims=True)
    lo = N_GROUPS + grp * EXPERTS_PER_GROUP
    el = jnp.where((lane >= lo) & (lane < lo + EXPERTS_PER_GROUP), logit, NEG)
    v1 = jnp.max(el, axis=-1, keepdims=True)
    i1 = jnp.min(jnp.where(el == v1, lane, 1e9), axis=-1, keepdims=True)
    el2 = jnp.where(lane == i1, NEG, el)
    v2 = jnp.max(el2, axis=-1, keepdims=True)
    i2 = jnp.min(jnp.where(el2 == v2, lane, 1e9), axis=-1, keepdims=True)
    e = jnp.exp(v2 - v1)
    g1 = pg / (1.0 + e)
    g2 = pg * e / (1.0 + e)
    e1 = i1 - N_GROUPS
    e2 = i2 - N_GROUPS

    oh1 = lane == e1
    oh2 = lane == e2
    onehot = jnp.where(oh1 | oh2, 1.0, 0.0)
    ri = lax.broadcasted_iota(jnp.int32, (tm, tm), 0)
    ci = lax.broadcasted_iota(jnp.int32, (tm, tm), 1)
    tri = jnp.where(ci < ri, 1.0, 0.0).astype(BF16)
    before = _dot(tri, onehot.astype(BF16)) + cnt[...]
    r1 = jnp.sum(jnp.where(oh1, before, 0.0), axis=-1, keepdims=True)
    r2 = jnp.sum(jnp.where(oh2, before, 0.0), axis=-1, keepdims=True)
    cnt[...] = cnt[...] + jnp.sum(onehot, axis=0, keepdims=True)
    cnt_ref[...] = cnt[...]
    route = jnp.zeros_like(logit)
    for j, val in enumerate((e1, e2, r1, r2, g1, g2)):
        route = jnp.where(lane == j, val, route)
    route_ref[...] = route


def _post(x2, og, z, ob, ga, gb, bones, wo, gf, wrh, wrl, br, *, tm):
    m, d = x2.shape
    hw = og.shape[1]
    row = lambda w: pl.BlockSpec((tm, w), lambda i: (i, 0))
    const = lambda a: pl.BlockSpec(a.shape, lambda i: (0,) * a.ndim)
    consts = [ga, gb, bones, wo, gf, wrh, wrl, br]
    return pl.pallas_call(
        functools.partial(_post_kernel, tm=tm),
        grid=(m // tm,),
        in_specs=[row(d), row(hw), row(hw), row(hw)] + [const(a) for a in consts],
        out_specs=[row(d), row(LANES), pl.BlockSpec((1, LANES), lambda i: (0, 0))],
        out_shape=[jax.ShapeDtypeStruct((m, d), F32), jax.ShapeDtypeStruct((m, LANES), F32),
                   jax.ShapeDtypeStruct((1, LANES), F32)],
        scratch_shapes=[pltpu.VMEM((1, LANES), F32)],
        compiler_params=_cparams(("arbitrary",)),
        name=f"post_{m}",
    )(x2, og, z, ob, *consts)


def _dispatch_kernel(h_ref, gf_ref, dest_ref, xin_ref, xbuf_ref, mrow, dsm, sem, dsem, *, tm):
    del xin_ref
    i = pl.program_id(0)
    cp = pltpu.make_async_copy(dest_ref.at[pl.ds(i, 1)], dsm, dsem)
    cp.start()
    mrow[...] = _rms(h_ref[...], gf_ref[...])
    cp.wait()

    def body(r, carry):
        for kk in range(2):
            d = dsm[0, 2 * r + kk]
            pltpu.make_async_copy(mrow.at[pl.ds(r, 1)], xbuf_ref.at[pl.ds(d, 1)], sem).start()
        return carry

    lax.fori_loop(0, tm, body, 0, unroll=8)
    for _ in range(2):
        pltpu.make_async_copy(mrow, xbuf_ref.at[pl.ds(0, tm)], sem).wait()


def _dispatch(h, gf, dest, n_slots, *, tm):
    m, d = h.shape
    xzero = jnp.zeros((n_slots, d), F32)
    return pl.pallas_call(
        functools.partial(_dispatch_kernel, tm=tm),
        grid=(m // tm,),
        in_specs=[pl.BlockSpec((tm, d), lambda i: (i, 0)), pl.BlockSpec((1, d), lambda i: (0, 0)),
                  pl.BlockSpec(memory_space=pl.ANY), pl.BlockSpec(memory_space=pl.ANY)],
        out_specs=pl.BlockSpec(memory_space=pl.ANY),
        out_shape=jax.ShapeDtypeStruct((n_slots, d), F32),
        scratch_shapes=[pltpu.VMEM((tm, d), F32), pltpu.SMEM((1, 2 * tm), jnp.int32),
                        pltpu.SemaphoreType.DMA, pltpu.SemaphoreType.DMA],
        input_output_aliases={3: 0},
        compiler_params=_cparams(("arbitrary",), row_dma=True),
        name=f"dispatch_{m}",
    )(h, gf, dest, xzero)


def _expert_kernel(be_ref, nu_ref, x_ref, wg_ref, wu_ref, wd_ref, y_ref):
    b = pl.program_id(0)

    @pl.when(b < nu_ref[0])
    def _():
        x = x_ref[...].astype(BF16)
        hid = _silu(_dot(x, wg_ref[0])) * _dot(x, wu_ref[0])
        y_ref[...] = _dot(hid.astype(BF16), wd_ref[0])

    @pl.when(b >= nu_ref[0])
    def _():
        y_ref[...] = jnp.zeros_like(y_ref)


def _experts(xbuf, blk_expert, n_used, wg, wu, wd, *, bm):
    n_slots, d = xbuf.shape
    de = wg.shape[2]
    grid_spec = pltpu.PrefetchScalarGridSpec(
        num_scalar_prefetch=2,
        grid=(n_slots // bm,),
        in_specs=[pl.BlockSpec((bm, d), lambda b, be, nu: (b, 0)),
                  pl.BlockSpec((1, d, de), lambda b, be, nu: (be[b], 0, 0)),
                  pl.BlockSpec((1, d, de), lambda b, be, nu: (be[b], 0, 0)),
                  pl.BlockSpec((1, de, d), lambda b, be, nu: (be[b], 0, 0))],
        out_specs=pl.BlockSpec((bm, d), lambda b, be, nu: (b, 0)),
    )
    return pl.pallas_call(
        _expert_kernel,
        grid_spec=grid_spec,
        out_shape=jax.ShapeDtypeStruct((n_slots, d), F32),
        compiler_params=_cparams(("arbitrary",)),
        name=f"experts_{n_slots}",
    )(blk_expert, n_used, xbuf, wg, wu, wd)


def _combine_kernel(h_ref, route_ref, p_ref, wpg_ref, wpp_ref, gfin_ref, dest_ref, ybuf_ref, y_ref,
                    y1, y2, dsm, sem, dsem, *, tm):
    i = pl.program_id(0)
    cp = pltpu.make_async_copy(dest_ref.at[pl.ds(i, 1)], dsm, dsem)
    cp.start()
    cp.wait()

    def body(r, carry):
        for kk, dst in enumerate((y1, y2)):
            d = dsm[0, 2 * r + kk]
            pltpu.make_async_copy(ybuf_ref.at[pl.ds(d, 1)], dst.at[pl.ds(r, 1)], sem).start()
        return carry

    lax.fori_loop(0, tm, body, 0, unroll=8)
    for dst in (y1, y2):
        pltpu.make_async_copy(ybuf_ref.at[pl.ds(0, tm)], dst, sem).wait()

    route = route_ref[...]
    g1 = route[:, 4:5]
    g2 = route[:, 5:6]
    h = h_ref[...] + (g1 * y1[...] + g2 * y2[...])
    gate = _sigmoid(_dot(h.astype(BF16), wpg_ref[...]))
    out = h + gate * _dot(p_ref[...].astype(BF16), wpp_ref[...])
    y_ref[...] = _rms(out, gfin_ref[...])


def _combine(h, route, p, wpg, wpp, gfin, dest, ybuf, *, tm):
    m, d = h.shape
    row = lambda w: pl.BlockSpec((tm, w), lambda i: (i, 0))
    const = lambda a: pl.BlockSpec(a.shape, lambda i: (0,) * a.ndim)
    return pl.pallas_call(
        functools.partial(_combine_kernel, tm=tm),
        grid=(m // tm,),
        in_specs=[row(d), row(LANES), row(p.shape[1]), const(wpg), const(wpp), const(gfin),
                  pl.BlockSpec(memory_space=pl.ANY), pl.BlockSpec(memory_space=pl.ANY)],
        out_specs=row(d),
        out_shape=jax.ShapeDtypeStruct((m, d), F32),
        scratch_shapes=[pltpu.VMEM((tm, d), F32), pltpu.VMEM((tm, d), F32), pltpu.SMEM((1, 2 * tm), jnp.int32),
                        pltpu.SemaphoreType.DMA, pltpu.SemaphoreType.DMA],
        compiler_params=_cparams(("arbitrary",), row_dma=True),
        name=f"combine_{m}",
    )(h, route, p, wpg, wpp, gfin, dest, ybuf)


def _tail(x2, og, z, ob, p2, wts, *, tm, bm):
    m, d = x2.shape
    h, route, counts = _post(x2, og, z, ob, wts["ga"], wts["gb"], wts["bones"], wts["wo"], wts["gf"],
                             wts["wrh"], wts["wrl"], wts["br"], tm=tm)
    eid = route[:, 0:2].astype(jnp.int32)
    rank = route[:, 2:4].astype(jnp.int32)
    sizes = counts[0, :N_EXPERTS].astype(jnp.int32)
    padded = (sizes + bm - 1) // bm * bm
    pend = jnp.cumsum(padded)
    pstart = pend - padded
    dest = (pstart[eid] + rank).reshape(m // tm, 2 * tm)
    n_blk = (2 * m) // bm + N_EXPERTS
    blk_start = jnp.arange(n_blk, dtype=jnp.int32) * bm
    blk_expert = jnp.minimum(jnp.sum(pend[None, :] <= blk_start[:, None], axis=1), N_EXPERTS - 1).astype(jnp.int32)
    n_used = (pend[-1:] // bm).astype(jnp.int32)
    xbuf = _dispatch(h, wts["gf"], dest, n_blk * bm, tm=tm)
    ybuf = _experts(xbuf, blk_expert, n_used, wts["wg"], wts["wu"], wts["wd"], bm=bm)
    return _combine(h, route, p2, wts["wpg"], wts["wpp"], wts["gfin"], dest, ybuf, tm=tm)


def _rope_tables(pos):
    half = ROT_DIM // 2
    inv = ROPE_THETA ** (-jnp.arange(half, dtype=F32) * (2.0 / ROT_DIM))
    ang = pos.astype(F32)[:, None] * inv[None, :]
    cos, sin = jnp.cos(ang), jnp.sin(ang)
    j = jnp.arange(LANES) % HEAD_DIM
    first, second = j < half, (j >= half) & (j < ROT_DIM)
    jj = jnp.where(second, j - half, jnp.where(first, j, 0))
    c = jnp.where((first | second)[None, :], cos[:, jj], 1.0)
    s1 = jnp.where(first[None, :], -sin[:, jj], 0.0)
    s2 = jnp.where(second[None, :], sin[:, jj], 0.0)
    return c, s1, s2


def kernel(x_prompt, x_sample, cache_win_k, cache_win_v, state_conv, state_delta, p_prompt, p_sample, g_attn_norm, w_in, w_conv, a_log, dt_bias, g_a_out, g_b_out, w_out, g_ffn_norm, w_router_group, b_router_group, w_router_expert, b_router_expert, w_exp_gate, w_exp_up, w_exp_down, w_ple_gate, w_ple_proj, g_final):
    n, t, d = x_prompt.shape
    ns = x_sample.shape[0]
    assert w_in.shape[0] == 1 and x_sample.shape[1] == 1
    ha = a_log.shape[1]
    aw = ha * HEAD_DIM
    off_a = 4 * aw
    off_win = off_a + 2 * ha
    n_past = cache_win_k.shape[2]
    keep = min(DILATIONS[-1][0], t)
    hi = lambda a: a.astype(F32)

    w = w_in[0]
    w_ab = jnp.pad(w[:, off_a:off_win], ((0, 0), (0, LANES - 2 * ha)))
    w_cat = jnp.concatenate([w[:, :off_a], w[:, off_win:], w_ab], axis=1).astype(BF16)
    pad_l = lambda v: jnp.pad(hi(v), (0, LANES - v.shape[0]))[None, :]
    alog = pad_l(a_log[0])
    dtb = pad_l(dt_bias[0])
    lane_head = jnp.arange(2 * aw) // HEAD_DIM
    esel = (jnp.arange(LANES)[:, None] == lane_head[None, :]).astype(BF16)
    hd = jnp.arange(aw) // HEAD_DIM
    bones = (hd[:, None] == hd[None, :]).astype(BF16)
    wr = jnp.concatenate([hi(w_router_group[0]), hi(w_router_expert[0]).reshape(d, N_EXPERTS)], axis=1)
    wr = jnp.pad(wr, ((0, 0), (0, LANES - wr.shape[1])))
    wrh = wr.astype(BF16)
    wrl = (wr - wrh.astype(F32)).astype(BF16)
    br = jnp.concatenate([hi(b_router_group[0]), hi(b_router_expert[0]).reshape(N_EXPERTS)])
    wts = dict(
        ga=jnp.tile(hi(g_a_out[0]), ha)[None, :], gb=jnp.tile(hi(g_b_out[0]), ha)[None, :], bones=bones,
        wo=w_out[0].astype(BF16), gf=hi(g_ffn_norm[0])[None, :], wrh=wrh, wrl=wrl, br=pad_l(br),
        wg=w_exp_gate[0].astype(BF16), wu=w_exp_up[0].astype(BF16), wd=w_exp_down[0].astype(BF16),
        wpg=w_ple_gate[0].astype(BF16), wpp=w_ple_proj[0].astype(BF16), gfin=hi(g_final)[None, :])
    g_attn = hi(g_attn_norm[0])[None, :]
    wconv = hi(w_conv[0])

    tm_p = min(256, t)
    cos, s1, s2 = _rope_tables(jnp.arange(t, dtype=jnp.int32))
    qa, ka, va, z, la, bt, qb, kb, vb, conv_p = _proj(
        x_prompt, g_attn, w_cat, wconv, alog, dtb, esel, cos, s1, s2, bones, None, tm=tm_p, sample=False)
    og, s_fin = _gdn_prompt(qa, ka, va, la, bt, tt=min(512, t), unroll=4)
    ob = _band(qb, kb, vb, tt=DILATIONS[-1][0], unroll=4)
    flat = lambda a: a.reshape(n * t, a.shape[-1])
    y_prompt = _tail(flat(x_prompt), flat(og), flat(z), flat(ob), flat(p_prompt[0]), wts,
                     tm=tm_p, bm=256).reshape(n, t, d)

    cos, s1, s2 = _rope_tables(jnp.full((1,), PAST_LEN, jnp.int32))
    st = [state_conv[0][None, :, j, :] for j in range(state_conv.shape[2])]
    xs3 = x_sample.reshape(1, ns, d)
    qa_s, ka_s, va_s, z_s, la_s, bt_s, qb_s, kb_s, vb_s, ua_s = [
        a[0] for a in _proj(xs3, g_attn, w_cat, wconv, alog, dtb, esel, cos, s1, s2, bones, st,
                            tm=ns, sample=True)]
    s_packed = state_delta[0].transpose(0, 2, 1, 3).reshape(ns, HEAD_DIM, aw)
    og_s, s_new = _gdn_sample(qa_s, ka_s, va_s, la_s, bt_s, s_packed, bones, gs=8)
    ob_s = _cache_attn(qb_s, kb_s, vb_s, cache_win_k[0].reshape(ns, n_past, aw),
                       cache_win_v[0].reshape(ns, n_past, aw), gs=8)
    y_sample = _tail(x_sample.reshape(ns, d), og_s, z_s, ob_s, p_sample[0].reshape(ns, -1), wts,
                     tm=ns, bm=128).reshape(ns, 1, d)

    unpack = lambda s: s.reshape(-1, HEAD_DIM, ha, HEAD_DIM).transpose(0, 2, 1, 3)[None]
    heads = lambda a: a.reshape(a.shape[0], -1, ha, HEAD_DIM)
    return (y_prompt, y_sample,
            heads(kb[:, t - keep:])[None], heads(vb[:, t - keep:])[None],
            conv_p[:, 8 - state_conv.shape[2]:][None], unpack(s_fin),
            heads(kb_s[:, None])[None], heads(vb_s[:, None])[None],
            jnp.concatenate([state_conv[0][:, 1:], ua_s[:, None]], axis=1)[None], unpack(s_new))
```

```python
import functools
import math

import jax
import jax.numpy as jnp
from jax import lax
from jax.experimental import pallas as pl
from jax.experimental.pallas import tpu as pltpu

F32 = jnp.float32
BF16 = jnp.bfloat16

HEAD_DIM = 64
GDN_CHUNK = 64
ROT_DIM = HEAD_DIM // 4
ROPE_THETA = 500000.0
PAST_LEN = 8192
DILATIONS = ((128, 1), (512, 4), (2048, 16))
BAND = 128
N_GROUPS = 4
EXPERTS_PER_GROUP = 8
N_EXPERTS = N_GROUPS * EXPERTS_PER_GROUP
NORM_EPS = 1e-6
NEG = -1e30
LANES = 128
VMEM_LIMIT = 56 * 1024 * 1024


def _cparams(sem, row_dma=False):
    return pltpu.CompilerParams(dimension_semantics=sem, vmem_limit_bytes=VMEM_LIMIT,
                                disable_bounds_checks=row_dma)


def _dot(a, b):
    return jnp.dot(a, b, preferred_element_type=F32)


def _dot_nt(a, b):
    return lax.dot_general(a, b, (((1,), (1,)), ((), ())), preferred_element_type=F32)


def _dot_tn(a, b):
    return lax.dot_general(a, b, (((0,), (0,)), ((), ())), preferred_element_type=F32)


def _split3(x):
    hi = x.astype(BF16)
    r = x - hi.astype(F32)
    mid = r.astype(BF16)
    lo = (r - mid.astype(F32)).astype(BF16)
    return hi, mid, lo


def _dot_f32_lhs(x, w_bf16):
    hi, mid, lo = _split3(x)
    return _dot(hi, w_bf16) + _dot(mid, w_bf16) + _dot(lo, w_bf16)


def _dot_f32_rhs(w_bf16, x):
    hi, mid, lo = _split3(x)
    return _dot(w_bf16, hi) + _dot(w_bf16, mid) + _dot(w_bf16, lo)


HEAD_SHIFT = HEAD_DIM.bit_length() - 1


def _lane_head(i):
    return lax.shift_right_logical(i, HEAD_SHIFT)


def _lane_in_head(i):
    return lax.bitwise_and(i, HEAD_DIM - 1)


def _sigmoid(x):
    return 1.0 / (1.0 + jnp.exp(-x))


def _silu(x):
    return x * _sigmoid(x)


def _softplus(x):
    return jnp.maximum(x, 0.0) + jnp.log1p(jnp.exp(-jnp.abs(x)))


def _rms(x, g):
    return x * lax.rsqrt(jnp.mean(x * x, axis=-1, keepdims=True) + NORM_EPS) * g


def _head_rms(x, bones, g, precise=False):
    ss = _dot_f32_lhs(x * x, bones) if precise else _dot((x * x).astype(BF16), bones)
    return x * lax.rsqrt(ss * (1.0 / HEAD_DIM) + NORM_EPS) * g


def _proj_kernel(*refs, tm, aw, sample):
    if sample:
        (x_ref, g_ref, w_ref, wc_ref, alog_ref, dtb_ref, esel_ref, cos_ref, s1_ref, s2_ref, bones_ref,
         wl_ref, st0_ref, st1_ref, st2_ref,
         qa_ref, ka_ref, va_ref, z_ref, la_ref, bt_ref, qb_ref, kb_ref, vb_ref, conv_ref) = refs
        buf = None
    else:
        (x_ref, g_ref, w_ref, wc_ref, alog_ref, dtb_ref, esel_ref, cos_ref, s1_ref, s2_ref, bones_ref,
         qa_ref, ka_ref, va_ref, z_ref, la_ref, bt_ref, qb_ref, kb_ref, vb_ref, conv_ref, buf) = refs
        t = pl.program_id(1)

        @pl.when(t == 0)
        def _():
            buf[0:8, :] = jnp.zeros((8, 3 * aw), F32)

    xf = _rms(x_ref[0], g_ref[...])
    xn = xf.astype(BF16)
    bones = bones_ref[...]
    if sample:
        xlo = (xf - xn.astype(F32)).astype(BF16)
        proj = lambda cs: _dot(xn, w_ref[:, cs]) + _dot(xlo, w_ref[:, cs]) + _dot(xn, wl_ref[:, cs])
        sumsq = lambda y: _dot_f32_lhs(y * y, bones)
    else:
        proj = lambda cs: _dot(xn, w_ref[:, cs])
        sumsq = lambda y: _dot((y * y).astype(BF16), bones)

    outs_a = (qa_ref, ka_ref, va_ref)
    for c in range(3):
        cs = slice(c * aw, (c + 1) * aw)
        u = proj(cs)
        if sample:
            conv_ref[0, :, cs] = u
            y = (wc_ref[3:4, cs] * u + wc_ref[2:3, cs] * st2_ref[0, :, cs]
                 + wc_ref[1:2, cs] * st1_ref[0, :, cs] + wc_ref[0:1, cs] * st0_ref[0, :, cs])
        else:
            buf[8:8 + tm, cs] = u
            y = (wc_ref[3:4, cs] * u + wc_ref[2:3, cs] * buf[7:7 + tm, cs]
                 + wc_ref[1:2, cs] * buf[6:6 + tm, cs] + wc_ref[0:1, cs] * buf[5:5 + tm, cs])
        y = _silu(y)
        if c < 2:
            y = y * lax.rsqrt(sumsq(y) + NORM_EPS)
        outs_a[c][0] = y
    if not sample:
        tail = buf[tm:tm + 8, :]
        conv_ref[0] = tail
        buf[0:8, :] = tail

    z_ref[0] = proj(slice(3 * aw, 4 * aw))

    ab = proj(slice(7 * aw, 7 * aw + LANES))
    lane = lax.broadcasted_iota(jnp.int32, ab.shape, 1)
    log_a = -jnp.exp(alog_ref[...]) * _softplus(ab + dtb_ref[...])
    comb = jnp.where(lane < aw // HEAD_DIM, log_a, _sigmoid(ab))
    ex = _dot_f32_lhs(comb, esel_ref[...])
    la_ref[0] = ex[:, :aw]
    bt_ref[0] = ex[:, aw:]

    cosv, s1v, s2v = cos_ref[...], s1_ref[...], s2_ref[...]
    for c, oref in ((0, qb_ref), (1, kb_ref)):
        for gq in range(aw // LANES):
            cs = slice(4 * aw + c * aw + gq * LANES, 4 * aw + c * aw + (gq + 1) * LANES)
            u = proj(cs)
            r = (u * cosv + pltpu.roll(u, LANES - ROT_DIM // 2, 1) * s1v
                 + pltpu.roll(u, ROT_DIM // 2, 1) * s2v)
            oref[0, :, gq * LANES:(gq + 1) * LANES] = r
    vb_ref[0] = proj(slice(6 * aw, 7 * aw))


def _proj(x3, g, w_cat, wconv, alog, dtb, esel, cos, s1, s2, bones, states, *, tm, sample):
    n, t, d = x3.shape
    aw = bones.shape[0]
    nt = t // tm
    const = lambda shape: pl.BlockSpec(shape, lambda i, j: (0,) * len(shape))
    row = lambda width: pl.BlockSpec((1, tm, width), lambda i, j: (i, j, 0))
    tab = (pl.BlockSpec((1, LANES), lambda i, j: (0, 0)) if sample
           else pl.BlockSpec((tm, LANES), lambda i, j: (j, 0)))
    in_specs = [row(d), const((1, d)), const(w_cat.shape), const(wconv.shape), const((1, LANES)),
                const((1, LANES)), const(esel.shape), tab, tab, tab, const(bones.shape)]
    args = [x3, g, w_cat, wconv, alog, dtb, esel, cos, s1, s2, bones]
    if sample:
        w_lo, states = states[0], states[1:]
        in_specs += [const(w_lo.shape)] + [row(3 * aw)] * 3
        args += [w_lo] + list(states)
        conv_shape = jax.ShapeDtypeStruct((n, t, 3 * aw), F32)
        conv_spec = row(3 * aw)
        scratch = []
    else:
        conv_shape = jax.ShapeDtypeStruct((n, 8, 3 * aw), F32)
        conv_spec = pl.BlockSpec((1, 8, 3 * aw), lambda i, j: (i, 0, 0))
        scratch = [pltpu.VMEM((tm + 8, 3 * aw), F32)]
    o = jax.ShapeDtypeStruct((n, t, aw), F32)
    return pl.pallas_call(
        functools.partial(_proj_kernel, tm=tm, aw=aw, sample=sample),
        grid=(n, nt),
        in_specs=in_specs,
        out_specs=[row(aw)] * 9 + [conv_spec],
        out_shape=[o] * 9 + [conv_shape],
        scratch_shapes=scratch,
        compiler_params=_cparams(("arbitrary", "arbitrary")),
        name="proj_sample" if sample else "proj_prompt",
    )(*args)


def _rowstack(x, hms):
    return jnp.concatenate([jnp.where(hm, x, 0.0) for hm in hms], axis=0)


def _gdn_kernel(q_ref, k_ref, v_ref, la_ref, bt_ref, ltri_ref, ones_ref, o_ref, s_out_ref,
                S, KW, NN, QP, OU, AL, *, tt, hw, unroll):
    C = GDN_CHUNK
    qw = 4 * HEAD_DIM
    nq = hw // qw
    t = pl.program_id(1)

    @pl.when(t == 0)
    def _():
        S[...] = jnp.zeros_like(S)

    rowi = lax.broadcasted_iota(jnp.int32, (C, hw), 0)
    colj = _lane_in_head(lax.broadcasted_iota(jnp.int32, (C, hw), 1))
    m_incl = colj <= rowi
    m_strict = colj < rowi
    eye = (colj == rowi).astype(F32)
    lane_q = _lane_head(lax.broadcasted_iota(jnp.int32, (1, qw), 1))
    hms = [lane_q == h for h in range(4)]
    ltri = ltri_ref[...]
    ones = ones_ref[...]

    def pm(l, r):
        return _dot(l.astype(BF16), _rowstack(r, hms).astype(BF16))

    def diag_blocks(full):
        return sum(jnp.where(hms[h], full[h * HEAD_DIM:(h + 1) * HEAD_DIM, :], 0.0) for h in range(4))

    sls = [slice(qi * qw, (qi + 1) * qw) for qi in range(nq)]

    def intra(it, carry):
        cs = [it * unroll + uu for uu in range(unroll)]
        rws = [pl.ds(pl.multiple_of(c * C, C), C) for c in cs]
        qs = [q_ref[0, r, :] * (HEAD_DIM ** -0.5) for r in rws]
        ks = [k_ref[0, r, :] for r in rws]
        bs = [bt_ref[0, r, :] for r in rws]
        gs = [_dot_f32_rhs(ltri, la_ref[0, r, :]) for r in rws]
        grs = [_dot_f32_rhs(ones, g * eye) for g in gs]
        decs = [jnp.exp(jnp.where(m_incl, g - gr, NEG)) for g, gr in zip(gs, grs)]
        egs = [jnp.exp(g) for g in gs]
        kbs = [k * b for k, b in zip(ks, bs)]
        for c, g in zip(cs, gs):
            AL[pl.ds(pl.multiple_of(c * 8, 8), 8), :] = jnp.broadcast_to(jnp.exp(g[C - 1:C, :]), (8, hw))
        ch = [(ui, sl) for ui in range(unroll) for sl in sls]
        kst = [_rowstack(ks[ui][:, sl], hms).astype(BF16) for ui, sl in ch]
        kk = [_dot_nt(kbs[ui][:, sl].astype(BF16), kst[i]) for i, (ui, sl) in enumerate(ch)]
        x = [-jnp.where(m_strict[:, sl], kk[i] * decs[ui][:, sl], 0.0) for i, (ui, sl) in enumerate(ch)]
        p = [eye[:, sl] + x[i] for i, (ui, sl) in enumerate(ch)]
        for _ in range(int(math.log2(C)) - 1):
            x = [pm(xi, xi) for xi in x]
            px = [pm(pi, xi) for pi, xi in zip(p, x)]
            p = [pi + pxi for pi, pxi in zip(p, px)]
        u = [pm(p[i], (v_ref[0, rws[ui], sl] * bs[ui][:, sl])) for i, (ui, sl) in enumerate(ch)]
        w = [pm(p[i], kbs[ui][:, sl] * egs[ui][:, sl]) for i, (ui, sl) in enumerate(ch)]
        qk = [jnp.where(m_incl[:, sl], _dot_nt(qs[ui][:, sl].astype(BF16), kst[i]) * decs[ui][:, sl], 0.0)
              .astype(BF16) for i, (ui, sl) in enumerate(ch)]
        kuw = []
        for i, (ui, sl) in enumerate(ch):
            g = gs[ui][:, sl]
            kd = (ks[ui][:, sl] * jnp.exp(g[C - 1:C, :] - g)).astype(BF16)
            uw = jnp.concatenate([u[i], w[i]], axis=1).astype(BF16)
            kuw.append(_dot_tn(kd, uw))
        ou = [_dot(qk[i], _rowstack(u[i], hms).astype(BF16)) for i in range(len(ch))]
        qw_ = [_dot(qk[i], _rowstack(w[i], hms).astype(BF16)) for i in range(len(ch))]
        for i, (ui, sl) in enumerate(ch):
            krows = pl.ds(pl.multiple_of(cs[ui] * HEAD_DIM, HEAD_DIM), HEAD_DIM)
            NN[krows, sl] = diag_blocks(kuw[i][:, :qw])
            KW[krows, sl] = diag_blocks(kuw[i][:, qw:])
            OU[rws[ui], sl] = ou[i]
            QP[rws[ui], sl] = qs[ui][:, sl] * egs[ui][:, sl] - qw_[i]
        return carry

    lax.fori_loop(0, tt // C // unroll, intra, 0)

    def inter(c, carry):
        rows = pl.ds(pl.multiple_of(c * C, C), C)
        krows = pl.ds(pl.multiple_of(c * HEAD_DIM, HEAD_DIM), HEAD_DIM)
        al = AL[pl.ds(pl.multiple_of(c * 8, 8), 1), :]
        ss = [S[:, sl] for sl in sls]
        lhs = [jnp.concatenate([KW[krows, sl], QP[rows, sl]], axis=0).astype(BF16) for sl in sls]
        rs = [_dot(l, _rowstack(s, hms).astype(BF16)) for l, s in zip(lhs, ss)]
        for sl, s, r in zip(sls, ss, rs):
            S[:, sl] = s * al[:, sl] - r[:HEAD_DIM] + NN[krows, sl]
            o_ref[0, rows, sl] = r[HEAD_DIM:] + OU[rows, sl]
        return carry

    lax.fori_loop(0, tt // C, inter, 0)
    s_out_ref[0] = S[...]


def _gdn_prompt(q, k, v, la, bt, *, tt, unroll):
    n, t, hw = q.shape
    C = GDN_CHUNK
    ltri = jnp.tril(jnp.ones((C, C), F32)).astype(BF16)
    ones = jnp.ones((C, C), BF16)
    row = pl.BlockSpec((1, tt, hw), lambda i, j: (i, j, 0))
    cst = pl.BlockSpec((C, C), lambda i, j: (0, 0))
    nch = tt // C
    return pl.pallas_call(
        functools.partial(_gdn_kernel, tt=tt, hw=hw, unroll=unroll),
        grid=(n, t // tt),
        in_specs=[row] * 5 + [cst, cst],
        out_specs=[row, pl.BlockSpec((1, HEAD_DIM, hw), lambda i, j: (i, 0, 0))],
        out_shape=[jax.ShapeDtypeStruct((n, t, hw), F32), jax.ShapeDtypeStruct((n, HEAD_DIM, hw), F32)],
        scratch_shapes=[pltpu.VMEM((HEAD_DIM, hw), F32), pltpu.VMEM((nch * HEAD_DIM, hw), F32),
                        pltpu.VMEM((nch * HEAD_DIM, hw), F32), pltpu.VMEM((tt, hw), F32),
                        pltpu.VMEM((tt, hw), F32), pltpu.VMEM((nch * 8, hw), F32)],
        compiler_params=_cparams(("arbitrary", "arbitrary")),
        name="gdn_prompt",
    )(q, k, v, la, bt, ltri, ones)


def _gdn_step_kernel(q_ref, k_ref, v_ref, la_ref, bt_ref, s_ref, bones_ref, o_ref, s_out_ref, *, gs, hw):
    rowi = lax.broadcasted_iota(jnp.int32, (HEAD_DIM, hw), 0)
    colj = _lane_in_head(lax.broadcasted_iota(jnp.int32, (HEAD_DIM, hw), 1))
    eye = (colj == rowi).astype(F32)
    bones = bones_ref[...]
    for i in range(gs):
        r = slice(i, i + 1)
        q = q_ref[r, :] * (HEAD_DIM ** -0.5)
        k = k_ref[r, :]
        v = v_ref[r, :]
        eg = jnp.exp(la_ref[r, :])
        b = bt_ref[r, :]
        s = s_ref[i]
        kbc = _dot_f32_lhs(eye * k, bones)
        qbc = _dot_f32_lhs(eye * q, bones)
        ks = jnp.sum(kbc * s, axis=0, keepdims=True)
        qs = jnp.sum(qbc * s, axis=0, keepdims=True)
        qk = jnp.sum(kbc * qbc, axis=0, keepdims=True)
        vn = b * v - b * eg * ks
        o_ref[r, :] = eg * qs + qk * vn
        s_out_ref[i] = s * eg + kbc * vn


def _gdn_sample(q, k, v, la, bt, s_packed, bones, *, gs):
    m, hw = q.shape
    row = pl.BlockSpec((gs, hw), lambda i: (i, 0))
    st = pl.BlockSpec((gs, HEAD_DIM, hw), lambda i: (i, 0, 0))
    return pl.pallas_call(
        functools.partial(_gdn_step_kernel, gs=gs, hw=hw),
        grid=(m // gs,),
        in_specs=[row] * 5 + [st, pl.BlockSpec(bones.shape, lambda i: (0, 0))],
        out_specs=[row, st],
        out_shape=[jax.ShapeDtypeStruct((m, hw), F32), jax.ShapeDtypeStruct(s_packed.shape, F32)],
        compiler_params=_cparams(("arbitrary",)),
        name="gdn_sample",
    )(q, k, v, la, bt, s_packed, bones)


def _band_kernel(q_ref, k_ref, v_ref, o_ref, kf, vf, oc, lc, *, tt, unroll):
    j = pl.program_id(2)

    @pl.when(j == 0)
    def _():
        kf[0:tt, :] = jnp.zeros((tt, LANES), F32)
        vf[0:tt, :] = jnp.zeros((tt, LANES), F32)

    kf[tt:, :] = k_ref[0]
    vf[tt:, :] = v_ref[0]
    a = lax.bitwise_and(lax.broadcasted_iota(jnp.int32, (2 * BAND, 2 * BAND), 0), BAND - 1)
    c = lax.broadcasted_iota(jnp.int32, (2 * BAND, 2 * BAND), 1)
    band = (c >= a) & (c <= a + BAND)
    row_h1 = lax.broadcasted_iota(jnp.int32, (2 * BAND, LANES), 0) >= BAND
    lane_h1 = lax.broadcasted_iota(jnp.int32, (2 * BAND, LANES), 1) >= HEAD_DIM
    own = row_h1 == lane_h1
    out_h1 = lax.broadcasted_iota(jnp.int32, (BAND, LANES), 1) >= HEAD_DIM

    for ci, (_, d) in enumerate(DILATIONS):
        nblk = tt // (BAND * d)

        def body(it, carry, d=d, ci=ci, nblk=nblk):
            rows, krows, firsts = [], [], []
            for uu in range(unroll):
                idx = it * unroll + uu
                r = idx // nblk
                b = idx % nblk
                start = b * (BAND * d) + r
                if d == 1:
                    rows.append(pl.ds(start, BAND))
                    krows.append(pl.ds(tt + start - BAND, 2 * BAND))
                else:
                    rows.append(pl.ds(start, BAND, stride=d))
                    krows.append(pl.ds(tt + start - BAND * d, 2 * BAND, stride=d))
                firsts.append(jnp.logical_and(j == 0, b == 0))
            ss = []
            for rw, kr in zip(rows, krows):
                qb = q_ref[0, rw, :] * (HEAD_DIM ** -0.5)
                qs = jnp.where(own, jnp.concatenate([qb, qb], axis=0), 0.0).astype(BF16)
                ss.append(_dot_nt(qs, kf[kr, :].astype(BF16)))
            pcat, stats = [], []
            for s, first in zip(ss, firsts):
                s = jnp.where(band & (c >= jnp.where(first, BAND, 0)), s, NEG)
                ps, st = [], []
                for hh in range(2):
                    sh = s[hh * BAND:(hh + 1) * BAND]
                    m = jnp.max(sh, axis=-1, keepdims=True)
                    p = jnp.exp(sh - m)
                    l = jnp.sum(p, axis=-1, keepdims=True)
                    ps.append(p.astype(BF16))
                    st.append((l, m + jnp.log(l)))
                pcat.append(jnp.concatenate(ps, axis=0))
                stats.append(st)
            pvs = [_dot(pc, vf[kr, :].astype(BF16)) for pc, kr in zip(pcat, krows)]
            for rw, pv, st in zip(rows, pvs, stats):
                oc[ci, rw, :] = jnp.where(out_h1, pv[BAND:] / st[1][0], pv[:BAND] / st[0][0])
                lc[ci, rw, :] = jnp.where(out_h1, st[1][1], st[0][1])
            return carry

        lax.fori_loop(0, tt // BAND // unroll, body, 0)

    mrows = 256
    for ch in range(tt // mrows):
        rs = slice(ch * mrows, (ch + 1) * mrows)
        ls = [lc[ci, rs, :] for ci in range(len(DILATIONS))]
        mx = functools.reduce(jnp.maximum, ls)
        es = [jnp.exp(l - mx) for l in ls]
        o_ref[0, rs, :] = sum(e * oc[ci, rs, :] for ci, e in enumerate(es)) / sum(es)

    kf[0:tt, :] = kf[tt:, :]
    vf[0:tt, :] = vf[tt:, :]


def _band(q, k, v, *, tt, unroll):
    n, t, hw = q.shape
    assert all(w // d == BAND and tt % w == 0 for w, d in DILATIONS) and t % tt == 0
    blk = pl.BlockSpec((1, tt, LANES), lambda i, c, j: (i, j, c))
    nd = len(DILATIONS)
    return pl.pallas_call(
        functools.partial(_band_kernel, tt=tt, unroll=unroll),
        grid=(n, hw // LANES, t // tt),
        in_specs=[blk, blk, blk],
        out_specs=blk,
        out_shape=jax.ShapeDtypeStruct((n, t, hw), F32),
        scratch_shapes=[pltpu.VMEM((2 * tt, LANES), F32), pltpu.VMEM((2 * tt, LANES), F32),
                        pltpu.VMEM((nd, tt, LANES), F32), pltpu.VMEM((nd, tt, LANES), F32)],
        compiler_params=_cparams(("arbitrary", "arbitrary", "arbitrary")),
        name="band",
    )(q, k, v)


def _cache_attn_kernel(q_ref, kn_ref, vn_ref, k1, v1, k2, v2, k3, v3, o_ref, *, gs):
    nd = len(DILATIONS)
    for g in range(gs):
        q = q_ref[g] * (HEAD_DIM ** -0.5)
        vn = vn_ref[g]
        s0 = jnp.sum(q * kn_ref[g], axis=-1, keepdims=True)
        ss = [jnp.sum(kr[g, :, 0] * q[None], axis=-1, keepdims=True) for kr in (k1, k2, k3)]
        m = s0
        for s in ss:
            m = jnp.maximum(m, jnp.max(s, axis=0))
        p0 = jnp.exp(s0 - m)
        den = nd * p0
        num = nd * p0 * vn
        for s, vr in zip(ss, (v1, v2, v3)):
            p = jnp.exp(s - m[None])
            den = den + jnp.sum(p, axis=0)
            num = num + jnp.sum(p * vr[g, :, 0], axis=0)
        o_ref[g] = num / den


def _cache_attn(q, kn, vn, ck, cv, *, gs):
    m, nh, hd = q.shape
    n_past = ck.shape[1]
    row = pl.BlockSpec((gs, nh, hd), lambda i: (i, 0, 0))
    args, specs = [], []
    for window, dil in DILATIONS:
        assert window // dil == BAND and n_past % dil == 0 and n_past >= window
        ln = n_past // dil
        spec = pl.BlockSpec((gs, BAND, 1, nh, hd), lambda i, _b=ln // BAND - 1: (i, _b, 0, 0, 0))
        for cache in (ck, cv):
            args.append(cache.reshape(m, ln, dil, nh, hd))
            specs.append(spec)
    return pl.pallas_call(
        functools.partial(_cache_attn_kernel, gs=gs),
        grid=(m // gs,),
        in_specs=[row] * 3 + specs,
        out_specs=row,
        out_shape=jax.ShapeDtypeStruct((m, nh, hd), F32),
        compiler_params=_cparams(("arbitrary",)),
        name="cache_attn",
    )(q, kn, vn, *args)


def _post_kernel(*refs, tm, precise):
    if precise:
        (x_ref, og_ref, z_ref, ob_ref, ga_ref, gb_ref, bones_ref, wo_ref, gf_ref, wrh_ref, wrl_ref, br_ref,
         wol_ref, h_ref, route_ref, cnt_ref, cnt) = refs
    else:
        (x_ref, og_ref, z_ref, ob_ref, ga_ref, gb_ref, bones_ref, wo_ref, gf_ref, wrh_ref, wrl_ref, br_ref,
         h_ref, route_ref, cnt_ref, cnt) = refs
    i = pl.program_id(0)

    @pl.when(i == 0)
    def _():
        cnt[...] = jnp.zeros_like(cnt)

    bones = bones_ref[...]
    oa = _head_rms(og_ref[...], bones, ga_ref[...], precise) * _silu(z_ref[...])
    ob = _head_rms(ob_ref[...], bones, gb_ref[...], precise)
    mixf = jnp.concatenate([oa, ob], axis=-1)
    mix = mixf.astype(BF16)
    proj = _dot(mix, wo_ref[...])
    if precise:
        proj = proj + _dot((mixf - mix.astype(F32)).astype(BF16), wo_ref[...]) + _dot(mix, wol_ref[...])
    h = x_ref[...] + proj
    h_ref[...] = h

    mrow = _rms(h, gf_ref[...])
    mh = mrow.astype(BF16)
    ml = (mrow - mh.astype(F32)).astype(BF16)
    logit = _dot(mh, wrh_ref[...]) + _dot(mh, wrl_ref[...]) + _dot(ml, wrh_ref[...]) + br_ref[...]
    lane = lax.broadcasted_iota(jnp.int32, logit.shape, 1).astype(F32)
    gl = jnp.where(lane < N_GROUPS, logit, NEG)
    gmax = jnp.max(gl, axis=-1, keepdims=True)
    grp = jnp.min(jnp.where(gl == gmax, lane, 1e9), axis=-1, keepdims=True)
    pg = 1.0 / jnp.sum(jnp.exp(gl - gmax), axis=-1, keepdims=True)
    lo = N_GROUPS + grp * EXPERTS_PER_GROUP
    el = jnp.where((lane >= lo) & (lane < lo + EXPERTS_PER_GROUP), logit, NEG)
    v1 = jnp.max(el, axis=-1, keepdims=True)
    i1 = jnp.min(jnp.where(el == v1, lane, 1e9), axis=-1, keepdims=True)
    el2 = jnp.where(lane == i1, NEG, el)
    v2 = jnp.max(el2, axis=-1, keepdims=True)
    i2 = jnp.min(jnp.where(el2 == v2, lane, 1e9), axis=-1, keepdims=True)
    e = jnp.exp(v2 - v1)
    g1 = pg / (1.0 + e)
    g2 = pg * e / (1.0 + e)
    e1 = i1 - N_GROUPS
    e2 = i2 - N_GROUPS

    oh1 = lane == e1
    oh2 = lane == e2
    onehot = jnp.where(oh1 | oh2, 1.0, 0.0)
    ri = lax.broadcasted_iota(jnp.int32, (tm, tm), 0)
    ci = lax.broadcasted_iota(jnp.int32, (tm, tm), 1)
    tri = jnp.where(ci < ri, 1.0, 0.0).astype(BF16)
    before = _dot(tri, onehot.astype(BF16)) + cnt[...]
    r1 = jnp.sum(jnp.where(oh1, before, 0.0), axis=-1, keepdims=True)
    r2 = jnp.sum(jnp.where(oh2, before, 0.0), axis=-1, keepdims=True)
    cnt[...] = cnt[...] + jnp.sum(onehot, axis=0, keepdims=True)
    cnt_ref[...] = cnt[...]
    route = jnp.zeros_like(logit)
    for j, val in enumerate((e1, e2, r1, r2, g1, g2)):
        route = jnp.where(lane == j, val, route)
    route_ref[...] = route


def _post(x2, og, z, ob, ga, gb, bones, wo, gf, wrh, wrl, br, wo_lo, *, tm):
    m, d = x2.shape
    hw = og.shape[1]
    row = lambda w: pl.BlockSpec((tm, w), lambda i: (i, 0))
    const = lambda a: pl.BlockSpec(a.shape, lambda i: (0,) * a.ndim)
    consts = [ga, gb, bones, wo, gf, wrh, wrl, br] + ([] if wo_lo is None else [wo_lo])
    return pl.pallas_call(
        functools.partial(_post_kernel, tm=tm, precise=wo_lo is not None),
        grid=(m // tm,),
        in_specs=[row(d), row(hw), row(hw), row(hw)] + [const(a) for a in consts],
        out_specs=[row(d), row(LANES), pl.BlockSpec((1, LANES), lambda i: (0, 0))],
        out_shape=[jax.ShapeDtypeStruct((m, d), F32), jax.ShapeDtypeStruct((m, LANES), F32),
                   jax.ShapeDtypeStruct((1, LANES), F32)],
        scratch_shapes=[pltpu.VMEM((1, LANES), F32)],
        compiler_params=_cparams(("arbitrary",)),
        name=f"post_{m}",
    )(x2, og, z, ob, *consts)


def _dispatch_kernel(h_ref, gf_ref, dest_ref, xin_ref, xbuf_ref, mrow, dsm, sem, dsem, *, tm):
    del xin_ref
    i = pl.program_id(0)
    cp = pltpu.make_async_copy(dest_ref.at[pl.ds(i, 1)], dsm, dsem)
    cp.start()
    mrow[...] = _rms(h_ref[...], gf_ref[...])
    cp.wait()

    def body(r, carry):
        for kk in range(2):
            d = dsm[0, 2 * r + kk]
            pltpu.make_async_copy(mrow.at[pl.ds(r, 1)], xbuf_ref.at[pl.ds(d, 1)], sem).start()
        return carry

    lax.fori_loop(0, tm, body, 0, unroll=8)
    for _ in range(2):
        pltpu.make_async_copy(mrow, xbuf_ref.at[pl.ds(0, tm)], sem).wait()


def _dispatch(h, gf, dest, n_slots, *, tm):
    m, d = h.shape
    xzero = jnp.zeros((n_slots, d), F32)
    return pl.pallas_call(
        functools.partial(_dispatch_kernel, tm=tm),
        grid=(m // tm,),
        in_specs=[pl.BlockSpec((tm, d), lambda i: (i, 0)), pl.BlockSpec((1, d), lambda i: (0, 0)),
                  pl.BlockSpec(memory_space=pl.ANY), pl.BlockSpec(memory_space=pl.ANY)],
        out_specs=pl.BlockSpec(memory_space=pl.ANY),
        out_shape=jax.ShapeDtypeStruct((n_slots, d), F32),
        scratch_shapes=[pltpu.VMEM((tm, d), F32), pltpu.SMEM((1, 2 * tm), jnp.int32),
                        pltpu.SemaphoreType.DMA, pltpu.SemaphoreType.DMA],
        input_output_aliases={3: 0},
        compiler_params=_cparams(("arbitrary",), row_dma=True),
        name=f"dispatch_{m}",
    )(h, gf, dest, xzero)


def _expert_kernel(be_ref, nu_ref, x_ref, wg_ref, wu_ref, wd_ref, y_ref):
    b = pl.program_id(0)

    @pl.when(b < nu_ref[0])
    def _():
        x = x_ref[...].astype(BF16)
        hid = _silu(_dot(x, wg_ref[0])) * _dot(x, wu_ref[0])
        y_ref[...] = _dot(hid.astype(BF16), wd_ref[0])

    @pl.when(b >= nu_ref[0])
    def _():
        y_ref[...] = jnp.zeros_like(y_ref)


def _experts(xbuf, blk_expert, n_used, wg, wu, wd, *, bm):
    n_slots, d = xbuf.shape
    de = wg.shape[2]
    grid_spec = pltpu.PrefetchScalarGridSpec(
        num_scalar_prefetch=2,
        grid=(n_slots // bm,),
        in_specs=[pl.BlockSpec((bm, d), lambda b, be, nu: (b, 0)),
                  pl.BlockSpec((1, d, de), lambda b, be, nu: (be[b], 0, 0)),
                  pl.BlockSpec((1, d, de), lambda b, be, nu: (be[b], 0, 0)),
                  pl.BlockSpec((1, de, d), lambda b, be, nu: (be[b], 0, 0))],
        out_specs=pl.BlockSpec((bm, d), lambda b, be, nu: (b, 0)),
    )
    return pl.pallas_call(
        _expert_kernel,
        grid_spec=grid_spec,
        out_shape=jax.ShapeDtypeStruct((n_slots, d), F32),
        compiler_params=_cparams(("arbitrary",)),
        name=f"experts_{n_slots}",
    )(blk_expert, n_used, xbuf, wg, wu, wd)


def _combine_kernel(h_ref, route_ref, p_ref, wpg_ref, wpp_ref, gfin_ref, dest_ref, ybuf_ref, y_ref,
                    y1, y2, dsm, sem, dsem, *, tm):
    i = pl.program_id(0)
    cp = pltpu.make_async_copy(dest_ref.at[pl.ds(i, 1)], dsm, dsem)
    cp.start()
    cp.wait()

    def body(r, carry):
        for kk, dst in enumerate((y1, y2)):
            d = dsm[0, 2 * r + kk]
            pltpu.make_async_copy(ybuf_ref.at[pl.ds(d, 1)], dst.at[pl.ds(r, 1)], sem).start()
        return carry

    lax.fori_loop(0, tm, body, 0, unroll=8)
    for dst in (y1, y2):
        pltpu.make_async_copy(ybuf_ref.at[pl.ds(0, tm)], dst, sem).wait()

    route = route_ref[...]
    g1 = route[:, 4:5]
    g2 = route[:, 5:6]
    h = h_ref[...] + (g1 * y1[...] + g2 * y2[...])
    gate = _sigmoid(_dot(h.astype(BF16), wpg_ref[...]))
    out = h + gate * _dot(p_ref[...].astype(BF16), wpp_ref[...])
    y_ref[...] = _rms(out, gfin_ref[...])


def _combine(h, route, p, wpg, wpp, gfin, dest, ybuf, *, tm):
    m, d = h.shape
    row = lambda w: pl.BlockSpec((tm, w), lambda i: (i, 0))
    const = lambda a: pl.BlockSpec(a.shape, lambda i: (0,) * a.ndim)
    return pl.pallas_call(
        functools.partial(_combine_kernel, tm=tm),
        grid=(m // tm,),
        in_specs=[row(d), row(LANES), row(p.shape[1]), const(wpg), const(wpp), const(gfin),
                  pl.BlockSpec(memory_space=pl.ANY), pl.BlockSpec(memory_space=pl.ANY)],
        out_specs=row(d),
        out_shape=jax.ShapeDtypeStruct((m, d), F32),
        scratch_shapes=[pltpu.VMEM((tm, d), F32), pltpu.VMEM((tm, d), F32), pltpu.SMEM((1, 2 * tm), jnp.int32),
                        pltpu.SemaphoreType.DMA, pltpu.SemaphoreType.DMA],
        compiler_params=_cparams(("arbitrary",), row_dma=True),
        name=f"combine_{m}",
    )(h, route, p, wpg, wpp, gfin, dest, ybuf)


def _tail(x2, og, z, ob, p2, wts, *, tm, bm, precise):
    m, d = x2.shape
    h, route, counts = _post(x2, og, z, ob, wts["ga"], wts["gb"], wts["bones"], wts["wo"], wts["gf"],
                             wts["wrh"], wts["wrl"], wts["br"], wts["wo_lo"] if precise else None, tm=tm)
    eid = route[:, 0:2].astype(jnp.int32)
    rank = route[:, 2:4].astype(jnp.int32)
    sizes = counts[0, :N_EXPERTS].astype(jnp.int32)
    padded = (sizes + bm - 1) // bm * bm
    pend = jnp.cumsum(padded)
    pstart = pend - padded
    dest = (pstart[eid] + rank).reshape(m // tm, 2 * tm)
    n_blk = (2 * m) // bm + N_EXPERTS
    blk_start = jnp.arange(n_blk, dtype=jnp.int32) * bm
    blk_expert = jnp.minimum(jnp.sum(pend[None, :] <= blk_start[:, None], axis=1), N_EXPERTS - 1).astype(jnp.int32)
    n_used = (pend[-1:] // bm).astype(jnp.int32)
    xbuf = _dispatch(h, wts["gf"], dest, n_blk * bm, tm=tm)
    ybuf = _experts(xbuf, blk_expert, n_used, wts["wg"], wts["wu"], wts["wd"], bm=bm)
    return _combine(h, route, p2, wts["wpg"], wts["wpp"], wts["gfin"], dest, ybuf, tm=tm)


def _rope_tables(pos):
    half = ROT_DIM // 2
    inv = ROPE_THETA ** (-jnp.arange(half, dtype=F32) * (2.0 / ROT_DIM))
    ang = pos.astype(F32)[:, None] * inv[None, :]
    cos, sin = jnp.cos(ang), jnp.sin(ang)
    j = jnp.arange(LANES) % HEAD_DIM
    first, second = j < half, (j >= half) & (j < ROT_DIM)
    jj = jnp.where(second, j - half, jnp.where(first, j, 0))
    c = jnp.where((first | second)[None, :], cos[:, jj], 1.0)
    s1 = jnp.where(first[None, :], -sin[:, jj], 0.0)
    s2 = jnp.where(second[None, :], sin[:, jj], 0.0)
    return c, s1, s2


def kernel(x_prompt, x_sample, cache_win_k, cache_win_v, state_conv, state_delta, p_prompt, p_sample, g_attn_norm, w_in, w_conv, a_log, dt_bias, g_a_out, g_b_out, w_out, g_ffn_norm, w_router_group, b_router_group, w_router_expert, b_router_expert, w_exp_gate, w_exp_up, w_exp_down, w_ple_gate, w_ple_proj, g_final):
    n, t, d = x_prompt.shape
    ns = x_sample.shape[0]
    assert w_in.shape[0] == 1 and x_sample.shape[1] == 1
    ha = a_log.shape[1]
    aw = ha * HEAD_DIM
    off_a = 4 * aw
    off_win = off_a + 2 * ha
    n_past = cache_win_k.shape[2]
    keep = min(DILATIONS[-1][0], t)
    hi = lambda a: a.astype(F32)

    w = w_in[0]
    w_ab = jnp.pad(w[:, off_a:off_win], ((0, 0), (0, LANES - 2 * ha)))
    w_cat32 = jnp.concatenate([w[:, :off_a], w[:, off_win:], w_ab], axis=1)
    w_cat = w_cat32.astype(BF16)
    w_cat_lo = (hi(w_cat32) - w_cat.astype(F32)).astype(BF16)
    pad_l = lambda v: jnp.pad(hi(v), (0, LANES - v.shape[0]))[None, :]
    alog = pad_l(a_log[0])
    dtb = pad_l(dt_bias[0])
    lane_head = jnp.arange(2 * aw) // HEAD_DIM
    esel = (jnp.arange(LANES)[:, None] == lane_head[None, :]).astype(BF16)
    hd = jnp.arange(aw) // HEAD_DIM
    bones = (hd[:, None] == hd[None, :]).astype(BF16)
    wr = jnp.concatenate([hi(w_router_group[0]), hi(w_router_expert[0]).reshape(d, N_EXPERTS)], axis=1)
    wr = jnp.pad(wr, ((0, 0), (0, LANES - wr.shape[1])))
    wrh = wr.astype(BF16)
    wrl = (wr - wrh.astype(F32)).astype(BF16)
    br = jnp.concatenate([hi(b_router_group[0]), hi(b_router_expert[0]).reshape(N_EXPERTS)])
    wts = dict(
        ga=jnp.tile(hi(g_a_out[0]), ha)[None, :], gb=jnp.tile(hi(g_b_out[0]), ha)[None, :], bones=bones,
        wo=w_out[0].astype(BF16), wo_lo=(hi(w_out[0]) - w_out[0].astype(BF16).astype(F32)).astype(BF16),
        gf=hi(g_ffn_norm[0])[None, :], wrh=wrh, wrl=wrl, br=pad_l(br),
        wg=w_exp_gate[0].astype(BF16), wu=w_exp_up[0].astype(BF16), wd=w_exp_down[0].astype(BF16),
        wpg=w_ple_gate[0].astype(BF16), wpp=w_ple_proj[0].astype(BF16), gfin=hi(g_final)[None, :])
    g_attn = hi(g_attn_norm[0])[None, :]
    wconv = hi(w_conv[0])

    tm_p = min(256, t)
    cos, s1, s2 = _rope_tables(jnp.arange(t, dtype=jnp.int32))
    qa, ka, va, z, la, bt, qb, kb, vb, conv_p = _proj(
        x_prompt, g_attn, w_cat, wconv, alog, dtb, esel, cos, s1, s2, bones, None, tm=tm_p, sample=False)
    og, s_fin = _gdn_prompt(qa, ka, va, la, bt, tt=min(512, t), unroll=4)
    ob = _band(qb, kb, vb, tt=DILATIONS[-1][0], unroll=4)
    flat = lambda a: a.reshape(n * t, a.shape[-1])
    y_prompt = _tail(flat(x_prompt), flat(og), flat(z), flat(ob), flat(p_prompt[0]), wts,
                     tm=tm_p, bm=256, precise=False).reshape(n, t, d)

    cos, s1, s2 = _rope_tables(jnp.full((1,), PAST_LEN, jnp.int32))
    st = [state_conv[0][None, :, j, :] for j in range(state_conv.shape[2])]
    xs3 = x_sample.reshape(1, ns, d)
    qa_s, ka_s, va_s, z_s, la_s, bt_s, qb_s, kb_s, vb_s, ua_s = [
        a[0] for a in _proj(xs3, g_attn, w_cat, wconv, alog, dtb, esel, cos, s1, s2, bones, [w_cat_lo] + st,
                            tm=ns, sample=True)]
    s_packed = state_delta[0].transpose(0, 2, 1, 3).reshape(ns, HEAD_DIM, aw)
    og_s, s_new = _gdn_sample(qa_s, ka_s, va_s, la_s, bt_s, s_packed, bones, gs=8)
    per_head = lambda a: a.reshape(ns, ha, HEAD_DIM)
    ob_s = _cache_attn(per_head(qb_s), per_head(kb_s), per_head(vb_s), cache_win_k[0], cache_win_v[0],
                       gs=4).reshape(ns, aw)
    y_sample = _tail(x_sample.reshape(ns, d), og_s, z_s, ob_s, p_sample[0].reshape(ns, -1), wts,
                     tm=ns, bm=128, precise=True).reshape(ns, 1, d)

    unpack = lambda s: s.reshape(-1, HEAD_DIM, ha, HEAD_DIM).transpose(0, 2, 1, 3)[None]
    heads = lambda a: a.reshape(a.shape[0], -1, ha, HEAD_DIM)
    return (y_prompt, y_sample,
            heads(kb[:, t - keep:])[None], heads(vb[:, t - keep:])[None],
            conv_p[:, 8 - state_conv.shape[2]:][None], unpack(s_fin),
            heads(kb_s[:, None])[None], heads(vb_s[:, None])[None],
            jnp.concatenate([state_conv[0][:, 1:], ua_s[:, None]], axis=1)[None], unpack(s_new))
```

```python
import functools
import math

import jax
import jax.numpy as jnp
from jax import lax
from jax.experimental import pallas as pl
from jax.experimental.pallas import tpu as pltpu

F32 = jnp.float32
BF16 = jnp.bfloat16

HEAD_DIM = 64
GDN_CHUNK = 64
ROT_DIM = HEAD_DIM // 4
ROPE_THETA = 500000.0
PAST_LEN = 8192
DILATIONS = ((128, 1), (512, 4), (2048, 16))
BAND = 128
N_GROUPS = 4
EXPERTS_PER_GROUP = 8
N_EXPERTS = N_GROUPS * EXPERTS_PER_GROUP
NORM_EPS = 1e-6
NEG = -1e30
LANES = 128
VMEM_LIMIT = 56 * 1024 * 1024


def _cparams(sem, row_dma=False):
    return pltpu.CompilerParams(dimension_semantics=sem, vmem_limit_bytes=VMEM_LIMIT,
                                disable_bounds_checks=row_dma)


def _dot(a, b):
    return jnp.dot(a, b, preferred_element_type=F32)


def _dot_nt(a, b):
    return lax.dot_general(a, b, (((1,), (1,)), ((), ())), preferred_element_type=F32)


def _dot_tn(a, b):
    return lax.dot_general(a, b, (((0,), (0,)), ((), ())), preferred_element_type=F32)


def _split3(x):
    hi = x.astype(BF16)
    r = x - hi.astype(F32)
    mid = r.astype(BF16)
    lo = (r - mid.astype(F32)).astype(BF16)
    return hi, mid, lo


def _dot_f32_lhs(x, w_bf16):
    hi, mid, lo = _split3(x)
    return _dot(hi, w_bf16) + _dot(mid, w_bf16) + _dot(lo, w_bf16)


def _dot_f32_rhs(w_bf16, x):
    hi, mid, lo = _split3(x)
    return _dot(w_bf16, hi) + _dot(w_bf16, mid) + _dot(w_bf16, lo)


HEAD_SHIFT = HEAD_DIM.bit_length() - 1


def _lane_head(i):
    return lax.shift_right_logical(i, HEAD_SHIFT)


def _lane_in_head(i):
    return lax.bitwise_and(i, HEAD_DIM - 1)


def _sigmoid(x):
    return 1.0 / (1.0 + jnp.exp(-x))


def _silu(x):
    return x * _sigmoid(x)


def _softplus(x):
    return jnp.maximum(x, 0.0) + jnp.log1p(jnp.exp(-jnp.abs(x)))


def _rms(x, g):
    return x * lax.rsqrt(jnp.mean(x * x, axis=-1, keepdims=True) + NORM_EPS) * g


def _head_rms(x, bones, g, precise=False):
    ss = _dot_f32_lhs(x * x, bones) if precise else _dot((x * x).astype(BF16), bones)
    return x * lax.rsqrt(ss * (1.0 / HEAD_DIM) + NORM_EPS) * g


def _proj_kernel(*refs, tm, aw, sample):
    if sample:
        (x_ref, g_ref, w_ref, wc_ref, alog_ref, dtb_ref, esel_ref, cos_ref, s1_ref, s2_ref, bones_ref,
         wl_ref, st0_ref, st1_ref, st2_ref,
         qa_ref, ka_ref, va_ref, z_ref, la_ref, bt_ref, qb_ref, kb_ref, vb_ref, conv_ref) = refs
        buf = None
    else:
        (x_ref, g_ref, w_ref, wc_ref, alog_ref, dtb_ref, esel_ref, cos_ref, s1_ref, s2_ref, bones_ref,
         qa_ref, ka_ref, va_ref, z_ref, la_ref, bt_ref, qb_ref, kb_ref, vb_ref, conv_ref, buf) = refs
        t = pl.program_id(1)

        @pl.when(t == 0)
        def _():
            buf[0:8, :] = jnp.zeros((8, 3 * aw), F32)

    xf = _rms(x_ref[0], g_ref[...])
    xn = xf.astype(BF16)
    bones = bones_ref[...]
    if sample:
        xlo = (xf - xn.astype(F32)).astype(BF16)
        proj = lambda cs: _dot(xn, w_ref[:, cs]) + _dot(xlo, w_ref[:, cs]) + _dot(xn, wl_ref[:, cs])
        sumsq = lambda y: _dot_f32_lhs(y * y, bones)
    else:
        proj = lambda cs: _dot(xn, w_ref[:, cs])
        sumsq = lambda y: _dot((y * y).astype(BF16), bones)

    outs_a = (qa_ref, ka_ref, va_ref)
    for c in range(3):
        cs = slice(c * aw, (c + 1) * aw)
        u = proj(cs)
        if sample:
            conv_ref[0, :, cs] = u
            y = (wc_ref[3:4, cs] * u + wc_ref[2:3, cs] * st2_ref[0, :, cs]
                 + wc_ref[1:2, cs] * st1_ref[0, :, cs] + wc_ref[0:1, cs] * st0_ref[0, :, cs])
        else:
            buf[8:8 + tm, cs] = u
            y = (wc_ref[3:4, cs] * u + wc_ref[2:3, cs] * buf[7:7 + tm, cs]
                 + wc_ref[1:2, cs] * buf[6:6 + tm, cs] + wc_ref[0:1, cs] * buf[5:5 + tm, cs])
        y = _silu(y)
        if c < 2:
            y = y * lax.rsqrt(sumsq(y) + NORM_EPS)
        outs_a[c][0] = y
    if not sample:
        tail = buf[tm:tm + 8, :]
        conv_ref[0] = tail
        buf[0:8, :] = tail

    z_ref[0] = proj(slice(3 * aw, 4 * aw))

    ab = proj(slice(7 * aw, 7 * aw + LANES))
    lane = lax.broadcasted_iota(jnp.int32, ab.shape, 1)
    log_a = -jnp.exp(alog_ref[...]) * _softplus(ab + dtb_ref[...])
    comb = jnp.where(lane < aw // HEAD_DIM, log_a, _sigmoid(ab))
    ex = _dot_f32_lhs(comb, esel_ref[...])
    la_ref[0] = ex[:, :aw]
    bt_ref[0] = ex[:, aw:]

    cosv, s1v, s2v = cos_ref[...], s1_ref[...], s2_ref[...]
    for c, oref in ((0, qb_ref), (1, kb_ref)):
        for gq in range(aw // LANES):
            cs = slice(4 * aw + c * aw + gq * LANES, 4 * aw + c * aw + (gq + 1) * LANES)
            u = proj(cs)
            r = (u * cosv + pltpu.roll(u, LANES - ROT_DIM // 2, 1) * s1v
                 + pltpu.roll(u, ROT_DIM // 2, 1) * s2v)
            oref[0, :, gq * LANES:(gq + 1) * LANES] = r
    vb_ref[0] = proj(slice(6 * aw, 7 * aw))


def _proj(x3, g, w_cat, wconv, alog, dtb, esel, cos, s1, s2, bones, states, *, tm, sample):
    n, t, d = x3.shape
    aw = bones.shape[0]
    nt = t // tm
    const = lambda shape: pl.BlockSpec(shape, lambda i, j: (0,) * len(shape))
    row = lambda width: pl.BlockSpec((1, tm, width), lambda i, j: (i, j, 0))
    tab = (pl.BlockSpec((1, LANES), lambda i, j: (0, 0)) if sample
           else pl.BlockSpec((tm, LANES), lambda i, j: (j, 0)))
    in_specs = [row(d), const((1, d)), const(w_cat.shape), const(wconv.shape), const((1, LANES)),
                const((1, LANES)), const(esel.shape), tab, tab, tab, const(bones.shape)]
    args = [x3, g, w_cat, wconv, alog, dtb, esel, cos, s1, s2, bones]
    if sample:
        w_lo, states = states[0], states[1:]
        in_specs += [const(w_lo.shape)] + [row(3 * aw)] * 3
        args += [w_lo] + list(states)
        conv_shape = jax.ShapeDtypeStruct((n, t, 3 * aw), F32)
        conv_spec = row(3 * aw)
        scratch = []
    else:
        conv_shape = jax.ShapeDtypeStruct((n, 8, 3 * aw), F32)
        conv_spec = pl.BlockSpec((1, 8, 3 * aw), lambda i, j: (i, 0, 0))
        scratch = [pltpu.VMEM((tm + 8, 3 * aw), F32)]
    o = jax.ShapeDtypeStruct((n, t, aw), F32)
    return pl.pallas_call(
        functools.partial(_proj_kernel, tm=tm, aw=aw, sample=sample),
        grid=(n, nt),
        in_specs=in_specs,
        out_specs=[row(aw)] * 9 + [conv_spec],
        out_shape=[o] * 9 + [conv_shape],
        scratch_shapes=scratch,
        compiler_params=_cparams(("arbitrary", "arbitrary")),
        name="proj_sample" if sample else "proj_prompt",
    )(*args)


def _rowstack(x, hms):
    return jnp.concatenate([jnp.where(hm, x, 0.0) for hm in hms], axis=0)


def _gdn_kernel(q_ref, k_ref, v_ref, la_ref, bt_ref, ltri_ref, ones_ref, o_ref, s_out_ref,
                S, KW, NN, QP, OU, AL, *, tt, hw, unroll):
    C = GDN_CHUNK
    qw = 4 * HEAD_DIM
    nq = hw // qw
    t = pl.program_id(1)

    @pl.when(t == 0)
    def _():
        S[...] = jnp.zeros_like(S)

    rowi = lax.broadcasted_iota(jnp.int32, (C, hw), 0)
    colj = _lane_in_head(lax.broadcasted_iota(jnp.int32, (C, hw), 1))
    m_incl = colj <= rowi
    m_strict = colj < rowi
    eye = (colj == rowi).astype(F32)
    lane_q = _lane_head(lax.broadcasted_iota(jnp.int32, (1, qw), 1))
    hms = [lane_q == h for h in range(4)]
    ltri = ltri_ref[...]
    ones = ones_ref[...]

    def pm(l, r):
        return _dot(l.astype(BF16), _rowstack(r, hms).astype(BF16))

    def diag_blocks(full):
        return sum(jnp.where(hms[h], full[h * HEAD_DIM:(h + 1) * HEAD_DIM, :], 0.0) for h in range(4))

    sls = [slice(qi * qw, (qi + 1) * qw) for qi in range(nq)]

    def intra(it, carry):
        cs = [it * unroll + uu for uu in range(unroll)]
        rws = [pl.ds(pl.multiple_of(c * C, C), C) for c in cs]
        qs = [q_ref[0, r, :] * (HEAD_DIM ** -0.5) for r in rws]
        ks = [k_ref[0, r, :] for r in rws]
        bs = [bt_ref[0, r, :] for r in rws]
        gs = [_dot_f32_rhs(ltri, la_ref[0, r, :]) for r in rws]
        grs = [_dot_f32_rhs(ones, g * eye) for g in gs]
        decs = [jnp.exp(jnp.where(m_incl, g - gr, NEG)) for g, gr in zip(gs, grs)]
        egs = [jnp.exp(g) for g in gs]
        kbs = [k * b for k, b in zip(ks, bs)]
        for c, g in zip(cs, gs):
            AL[pl.ds(pl.multiple_of(c * 8, 8), 8), :] = jnp.broadcast_to(jnp.exp(g[C - 1:C, :]), (8, hw))
        ch = [(ui, sl) for ui in range(unroll) for sl in sls]
        kst = [_rowstack(ks[ui][:, sl], hms).astype(BF16) for ui, sl in ch]
        kk = [_dot_nt(kbs[ui][:, sl].astype(BF16), kst[i]) for i, (ui, sl) in enumerate(ch)]
        x = [-jnp.where(m_strict[:, sl], kk[i] * decs[ui][:, sl], 0.0) for i, (ui, sl) in enumerate(ch)]
        p = [eye[:, sl] + x[i] for i, (ui, sl) in enumerate(ch)]
        for _ in range(int(math.log2(C)) - 1):
            x = [pm(xi, xi) for xi in x]
            px = [pm(pi, xi) for pi, xi in zip(p, x)]
            p = [pi + pxi for pi, pxi in zip(p, px)]
        u = [pm(p[i], (v_ref[0, rws[ui], sl] * bs[ui][:, sl])) for i, (ui, sl) in enumerate(ch)]
        w = [pm(p[i], kbs[ui][:, sl] * egs[ui][:, sl]) for i, (ui, sl) in enumerate(ch)]
        qk = [jnp.where(m_incl[:, sl], _dot_nt(qs[ui][:, sl].astype(BF16), kst[i]) * decs[ui][:, sl], 0.0)
              .astype(BF16) for i, (ui, sl) in enumerate(ch)]
        kuw = []
        for i, (ui, sl) in enumerate(ch):
            g = gs[ui][:, sl]
            kd = (ks[ui][:, sl] * jnp.exp(g[C - 1:C, :] - g)).astype(BF16)
            uw = jnp.concatenate([u[i], w[i]], axis=1).astype(BF16)
            kuw.append(_dot_tn(kd, uw))
        ou = [_dot(qk[i], _rowstack(u[i], hms).astype(BF16)) for i in range(len(ch))]
        qw_ = [_dot(qk[i], _rowstack(w[i], hms).astype(BF16)) for i in range(len(ch))]
        for i, (ui, sl) in enumerate(ch):
            krows = pl.ds(pl.multiple_of(cs[ui] * HEAD_DIM, HEAD_DIM), HEAD_DIM)
            NN[krows, sl] = diag_blocks(kuw[i][:, :qw])
            KW[krows, sl] = diag_blocks(kuw[i][:, qw:])
            OU[rws[ui], sl] = ou[i]
            QP[rws[ui], sl] = qs[ui][:, sl] * egs[ui][:, sl] - qw_[i]
        return carry

    lax.fori_loop(0, tt // C // unroll, intra, 0)

    def inter(c, carry):
        rows = pl.ds(pl.multiple_of(c * C, C), C)
        krows = pl.ds(pl.multiple_of(c * HEAD_DIM, HEAD_DIM), HEAD_DIM)
        al = AL[pl.ds(pl.multiple_of(c * 8, 8), 1), :]
        ss = [S[:, sl] for sl in sls]
        lhs = [jnp.concatenate([KW[krows, sl], QP[rows, sl]], axis=0).astype(BF16) for sl in sls]
        rs = [_dot(l, _rowstack(s, hms).astype(BF16)) for l, s in zip(lhs, ss)]
        for sl, s, r in zip(sls, ss, rs):
            S[:, sl] = s * al[:, sl] - r[:HEAD_DIM] + NN[krows, sl]
            o_ref[0, rows, sl] = r[HEAD_DIM:] + OU[rows, sl]
        return carry

    lax.fori_loop(0, tt // C, inter, 0)
    s_out_ref[0] = S[...]


def _gdn_prompt(q, k, v, la, bt, *, tt, unroll):
    n, t, hw = q.shape
    C = GDN_CHUNK
    ltri = jnp.tril(jnp.ones((C, C), F32)).astype(BF16)
    ones = jnp.ones((C, C), BF16)
    row = pl.BlockSpec((1, tt, hw), lambda i, j: (i, j, 0))
    cst = pl.BlockSpec((C, C), lambda i, j: (0, 0))
    nch = tt // C
    return pl.pallas_call(
        functools.partial(_gdn_kernel, tt=tt, hw=hw, unroll=unroll),
        grid=(n, t // tt),
        in_specs=[row] * 5 + [cst, cst],
        out_specs=[row, pl.BlockSpec((1, HEAD_DIM, hw), lambda i, j: (i, 0, 0))],
        out_shape=[jax.ShapeDtypeStruct((n, t, hw), F32), jax.ShapeDtypeStruct((n, HEAD_DIM, hw), F32)],
        scratch_shapes=[pltpu.VMEM((HEAD_DIM, hw), F32), pltpu.VMEM((nch * HEAD_DIM, hw), F32),
                        pltpu.VMEM((nch * HEAD_DIM, hw), F32), pltpu.VMEM((tt, hw), F32),
                        pltpu.VMEM((tt, hw), F32), pltpu.VMEM((nch * 8, hw), F32)],
        compiler_params=_cparams(("arbitrary", "arbitrary")),
        name="gdn_prompt",
    )(q, k, v, la, bt, ltri, ones)


def _gdn_step_kernel(q_ref, k_ref, v_ref, la_ref, bt_ref, s_ref, bones_ref, o_ref, s_out_ref, *, gs, hw):
    rowi = lax.broadcasted_iota(jnp.int32, (HEAD_DIM, hw), 0)
    colj = _lane_in_head(lax.broadcasted_iota(jnp.int32, (HEAD_DIM, hw), 1))
    eye = (colj == rowi).astype(F32)
    bones = bones_ref[...]
    for i in range(gs):
        r = slice(i, i + 1)
        q = q_ref[r, :] * (HEAD_DIM ** -0.5)
        k = k_ref[r, :]
        v = v_ref[r, :]
        eg = jnp.exp(la_ref[r, :])
        b = bt_ref[r, :]
        s = s_ref[i]
        kbc = _dot_f32_lhs(eye * k, bones)
        qbc = _dot_f32_lhs(eye * q, bones)
        ks = jnp.sum(kbc * s, axis=0, keepdims=True)
        qs = jnp.sum(qbc * s, axis=0, keepdims=True)
        qk = jnp.sum(kbc * qbc, axis=0, keepdims=True)
        vn = b * v - b * eg * ks
        o_ref[r, :] = eg * qs + qk * vn
        s_out_ref[i] = s * eg + kbc * vn


def _gdn_sample(q, k, v, la, bt, s_packed, bones, *, gs):
    m, hw = q.shape
    row = pl.BlockSpec((gs, hw), lambda i: (i, 0))
    st = pl.BlockSpec((gs, HEAD_DIM, hw), lambda i: (i, 0, 0))
    return pl.pallas_call(
        functools.partial(_gdn_step_kernel, gs=gs, hw=hw),
        grid=(m // gs,),
        in_specs=[row] * 5 + [st, pl.BlockSpec(bones.shape, lambda i: (0, 0))],
        out_specs=[row, st],
        out_shape=[jax.ShapeDtypeStruct((m, hw), F32), jax.ShapeDtypeStruct(s_packed.shape, F32)],
        compiler_params=_cparams(("arbitrary",)),
        name="gdn_sample",
    )(q, k, v, la, bt, s_packed, bones)


def _band_kernel(q_ref, k_ref, v_ref, o_ref, kf, vf, oc, lc, *, tt, unroll):
    j = pl.program_id(2)

    @pl.when(j == 0)
    def _():
        kf[0:tt, :] = jnp.zeros((tt, LANES), F32)
        vf[0:tt, :] = jnp.zeros((tt, LANES), F32)

    kf[tt:, :] = k_ref[0]
    vf[tt:, :] = v_ref[0]
    a = lax.bitwise_and(lax.broadcasted_iota(jnp.int32, (2 * BAND, 2 * BAND), 0), BAND - 1)
    c = lax.broadcasted_iota(jnp.int32, (2 * BAND, 2 * BAND), 1)
    band = (c >= a) & (c <= a + BAND)
    row_h1 = lax.broadcasted_iota(jnp.int32, (2 * BAND, LANES), 0) >= BAND
    lane_h1 = lax.broadcasted_iota(jnp.int32, (2 * BAND, LANES), 1) >= HEAD_DIM
    own = row_h1 == lane_h1
    out_h1 = lax.broadcasted_iota(jnp.int32, (BAND, LANES), 1) >= HEAD_DIM

    for ci, (_, d) in enumerate(DILATIONS):
        nblk = tt // (BAND * d)

        def body(it, carry, d=d, ci=ci, nblk=nblk):
            rows, krows, firsts = [], [], []
            for uu in range(unroll):
                idx = it * unroll + uu
                r = idx // nblk
                b = idx % nblk
                start = b * (BAND * d) + r
                if d == 1:
                    rows.append(pl.ds(start, BAND))
                    krows.append(pl.ds(tt + start - BAND, 2 * BAND))
                else:
                    rows.append(pl.ds(start, BAND, stride=d))
                    krows.append(pl.ds(tt + start - BAND * d, 2 * BAND, stride=d))
                firsts.append(jnp.logical_and(j == 0, b == 0))
            ss = []
            for rw, kr in zip(rows, krows):
                qb = q_ref[0, rw, :] * (HEAD_DIM ** -0.5)
                qs = jnp.where(own, jnp.concatenate([qb, qb], axis=0), 0.0).astype(BF16)
                ss.append(_dot_nt(qs, kf[kr, :].astype(BF16)))
            pcat, stats = [], []
            for s, first in zip(ss, firsts):
                s = jnp.where(band & (c >= jnp.where(first, BAND, 0)), s, NEG)
                ps, st = [], []
                for hh in range(2):
                    sh = s[hh * BAND:(hh + 1) * BAND]
                    m = jnp.max(sh, axis=-1, keepdims=True)
                    p = jnp.exp(sh - m)
                    l = jnp.sum(p, axis=-1, keepdims=True)
                    ps.append(p.astype(BF16))
                    st.append((l, m + jnp.log(l)))
                pcat.append(jnp.concatenate(ps, axis=0))
                stats.append(st)
            pvs = [_dot(pc, vf[kr, :].astype(BF16)) for pc, kr in zip(pcat, krows)]
            for rw, pv, st in zip(rows, pvs, stats):
                oc[ci, rw, :] = jnp.where(out_h1, pv[BAND:] / st[1][0], pv[:BAND] / st[0][0])
                lc[ci, rw, :] = jnp.where(out_h1, st[1][1], st[0][1])
            return carry

        lax.fori_loop(0, tt // BAND // unroll, body, 0)

    mrows = 256
    for ch in range(tt // mrows):
        rs = slice(ch * mrows, (ch + 1) * mrows)
        ls = [lc[ci, rs, :] for ci in range(len(DILATIONS))]
        mx = functools.reduce(jnp.maximum, ls)
        es = [jnp.exp(l - mx) for l in ls]
        o_ref[0, rs, :] = sum(e * oc[ci, rs, :] for ci, e in enumerate(es)) / sum(es)

    kf[0:tt, :] = kf[tt:, :]
    vf[0:tt, :] = vf[tt:, :]


def _band(q, k, v, *, tt, unroll):
    n, t, hw = q.shape
    assert all(w // d == BAND and tt % w == 0 for w, d in DILATIONS) and t % tt == 0
    blk = pl.BlockSpec((1, tt, LANES), lambda i, c, j: (i, j, c))
    nd = len(DILATIONS)
    return pl.pallas_call(
        functools.partial(_band_kernel, tt=tt, unroll=unroll),
        grid=(n, hw // LANES, t // tt),
        in_specs=[blk, blk, blk],
        out_specs=blk,
        out_shape=jax.ShapeDtypeStruct((n, t, hw), F32),
        scratch_shapes=[pltpu.VMEM((2 * tt, LANES), F32), pltpu.VMEM((2 * tt, LANES), F32),
                        pltpu.VMEM((nd, tt, LANES), F32), pltpu.VMEM((nd, tt, LANES), F32)],
        compiler_params=_cparams(("arbitrary", "arbitrary", "arbitrary")),
        name="band",
    )(q, k, v)


def _cache_attn_kernel(q_ref, kn_ref, vn_ref, k_ref, v_ref, o_ref, *, nh, n_past):
    pos = lax.broadcasted_iota(jnp.int32, (1, n_past), 1)
    cnt = jnp.zeros((1, n_past), F32)
    for window, dil in DILATIONS:
        hit = (pos >= n_past - window) & (lax.bitwise_and(pos, dil - 1) == 0)
        cnt = cnt + jnp.where(hit, 1.0, 0.0)
    live = cnt > 0.0
    nd = float(len(DILATIONS))
    lane_h = lax.broadcasted_iota(jnp.int32, (HEAD_DIM, nh), 1)
    out = jnp.zeros((HEAD_DIM, nh), F32)
    for h in range(nh):
        q = q_ref[0, :, h:h + 1] * (HEAD_DIM ** -0.5)
        s = jnp.where(live, jnp.sum(k_ref[0, h] * q, axis=0, keepdims=True), NEG)
        s0 = jnp.sum(q * kn_ref[0, :, h:h + 1], axis=0, keepdims=True)
        m = jnp.maximum(jnp.max(s, axis=-1, keepdims=True), s0)
        p = cnt * jnp.exp(s - m)
        p0 = nd * jnp.exp(s0 - m)
        den = p0 + jnp.sum(p, axis=-1, keepdims=True)
        num = p0 * vn_ref[0, :, h:h + 1] + jnp.sum(v_ref[0, h] * p, axis=-1, keepdims=True)
        out = jnp.where(lane_h == h, num / den, out)
    o_ref[0] = out


def _cache_attn(q_t, kn_t, vn_t, ck_t, cv_t):
    m, hd, nh = q_t.shape
    n_past = ck_t.shape[3]
    assert all(w // d == BAND and n_past >= w and d & (d - 1) == 0 and n_past % d == 0 for w, d in DILATIONS)
    col = pl.BlockSpec((1, hd, nh), lambda i: (i, 0, 0))
    cache = pl.BlockSpec((1, nh, hd, n_past), lambda i: (i, 0, 0, 0))
    return pl.pallas_call(
        functools.partial(_cache_attn_kernel, nh=nh, n_past=n_past),
        grid=(m,),
        in_specs=[col, col, col, cache, cache],
        out_specs=col,
        out_shape=jax.ShapeDtypeStruct((m, hd, nh), F32),
        compiler_params=_cparams(("arbitrary",)),
        name="cache_attn",
    )(q_t, kn_t, vn_t, ck_t, cv_t)


def _post_kernel(*refs, tm, precise):
    if precise:
        (x_ref, og_ref, z_ref, ob_ref, ga_ref, gb_ref, bones_ref, wo_ref, gf_ref, wrh_ref, wrl_ref, br_ref,
         wol_ref, h_ref, route_ref, cnt_ref, cnt) = refs
    else:
        (x_ref, og_ref, z_ref, ob_ref, ga_ref, gb_ref, bones_ref, wo_ref, gf_ref, wrh_ref, wrl_ref, br_ref,
         h_ref, route_ref, cnt_ref, cnt) = refs
    i = pl.program_id(0)

    @pl.when(i == 0)
    def _():
        cnt[...] = jnp.zeros_like(cnt)

    bones = bones_ref[...]
    oa = _head_rms(og_ref[...], bones, ga_ref[...], precise) * _silu(z_ref[...])
    ob = _head_rms(ob_ref[...], bones, gb_ref[...], precise)
    mixf = jnp.concatenate([oa, ob], axis=-1)
    mix = mixf.astype(BF16)
    proj = _dot(mix, wo_ref[...])
    if precise:
        proj = proj + _dot((mixf - mix.astype(F32)).astype(BF16), wo_ref[...]) + _dot(mix, wol_ref[...])
    h = x_ref[...] + proj
    h_ref[...] = h

    mrow = _rms(h, gf_ref[...])
    mh = mrow.astype(BF16)
    ml = (mrow - mh.astype(F32)).astype(BF16)
    logit = _dot(mh, wrh_ref[...]) + _dot(mh, wrl_ref[...]) + _dot(ml, wrh_ref[...]) + br_ref[...]
    lane = lax.broadcasted_iota(jnp.int32, logit.shape, 1).astype(F32)
    gl = jnp.where(lane < N_GROUPS, logit, NEG)
    gmax = jnp.max(gl, axis=-1, keepdims=True)
    grp = jnp.min(jnp.where(gl == gmax, lane, 1e9), axis=-1, keepdims=True)
    pg = 1.0 / jnp.sum(jnp.exp(gl - gmax), axis=-1, keepdims=True)
    lo = N_GROUPS + grp * EXPERTS_PER_GROUP
    el = jnp.where((lane >= lo) & (lane < lo + EXPERTS_PER_GROUP), logit, NEG)
    v1 = jnp.max(el, axis=-1, keepdims=True)
    i1 = jnp.min(jnp.where(el == v1, lane, 1e9), axis=-1, keepdims=True)
    el2 = jnp.where(lane == i1, NEG, el)
    v2 = jnp.max(el2, axis=-1, keepdims=True)
    i2 = jnp.min(jnp.where(el2 == v2, lane, 1e9), axis=-1, keepdims=True)
    e = jnp.exp(v2 - v1)
    g1 = pg / (1.0 + e)
    g2 = pg * e / (1.0 + e)
    e1 = i1 - N_GROUPS
    e2 = i2 - N_GROUPS

    oh1 = lane == e1
    oh2 = lane == e2
    onehot = jnp.where(oh1 | oh2, 1.0, 0.0)
    ri = lax.broadcasted_iota(jnp.int32, (tm, tm), 0)
    ci = lax.broadcasted_iota(jnp.int32, (tm, tm), 1)
    tri = jnp.where(ci < ri, 1.0, 0.0).astype(BF16)
    before = _dot(tri, onehot.astype(BF16)) + cnt[...]
    r1 = jnp.sum(jnp.where(oh1, before, 0.0), axis=-1, keepdims=True)
    r2 = jnp.sum(jnp.where(oh2, before, 0.0), axis=-1, keepdims=True)
    cnt[...] = cnt[...] + jnp.sum(onehot, axis=0, keepdims=True)
    cnt_ref[...] = cnt[...]
    route = jnp.zeros_like(logit)
    for j, val in enumerate((e1, e2, r1, r2, g1, g2)):
        route = jnp.where(lane == j, val, route)
    route_ref[...] = route


def _post(x2, og, z, ob, ga, gb, bones, wo, gf, wrh, wrl, br, wo_lo, *, tm):
    m, d = x2.shape
    hw = og.shape[1]
    row = lambda w: pl.BlockSpec((tm, w), lambda i: (i, 0))
    const = lambda a: pl.BlockSpec(a.shape, lambda i: (0,) * a.ndim)
    consts = [ga, gb, bones, wo, gf, wrh, wrl, br] + ([] if wo_lo is None else [wo_lo])
    return pl.pallas_call(
        functools.partial(_post_kernel, tm=tm, precise=wo_lo is not None),
        grid=(m // tm,),
        in_specs=[row(d), row(hw), row(hw), row(hw)] + [const(a) for a in consts],
        out_specs=[row(d), row(LANES), pl.BlockSpec((1, LANES), lambda i: (0, 0))],
        out_shape=[jax.ShapeDtypeStruct((m, d), F32), jax.ShapeDtypeStruct((m, LANES), F32),
                   jax.ShapeDtypeStruct((1, LANES), F32)],
        scratch_shapes=[pltpu.VMEM((1, LANES), F32)],
        compiler_params=_cparams(("arbitrary",)),
        name=f"post_{m}",
    )(x2, og, z, ob, *consts)


def _dispatch_kernel(h_ref, gf_ref, dest_ref, xin_ref, xbuf_ref, mrow, dsm, sem, dsem, *, tm):
    del xin_ref
    i = pl.program_id(0)
    cp = pltpu.make_async_copy(dest_ref.at[pl.ds(i, 1)], dsm, dsem)
    cp.start()
    mrow[...] = _rms(h_ref[...], gf_ref[...])
    cp.wait()

    def body(r, carry):
        for kk in range(2):
            d = dsm[0, 2 * r + kk]
            pltpu.make_async_copy(mrow.at[pl.ds(r, 1)], xbuf_ref.at[pl.ds(d, 1)], sem).start()
        return carry

    lax.fori_loop(0, tm, body, 0, unroll=8)
    for _ in range(2):
        pltpu.make_async_copy(mrow, xbuf_ref.at[pl.ds(0, tm)], sem).wait()


def _dispatch(h, gf, dest, n_slots, *, tm):
    m, d = h.shape
    xzero = jnp.zeros((n_slots, d), F32)
    return pl.pallas_call(
        functools.partial(_dispatch_kernel, tm=tm),
        grid=(m // tm,),
        in_specs=[pl.BlockSpec((tm, d), lambda i: (i, 0)), pl.BlockSpec((1, d), lambda i: (0, 0)),
                  pl.BlockSpec(memory_space=pl.ANY), pl.BlockSpec(memory_space=pl.ANY)],
        out_specs=pl.BlockSpec(memory_space=pl.ANY),
        out_shape=jax.ShapeDtypeStruct((n_slots, d), F32),
        scratch_shapes=[pltpu.VMEM((tm, d), F32), pltpu.SMEM((1, 2 * tm), jnp.int32),
                        pltpu.SemaphoreType.DMA, pltpu.SemaphoreType.DMA],
        input_output_aliases={3: 0},
        compiler_params=_cparams(("arbitrary",), row_dma=True),
        name=f"dispatch_{m}",
    )(h, gf, dest, xzero)


def _expert_kernel(be_ref, nu_ref, x_ref, wg_ref, wu_ref, wd_ref, y_ref):
    b = pl.program_id(0)

    @pl.when(b < nu_ref[0])
    def _():
        x = x_ref[...].astype(BF16)
        hid = _silu(_dot(x, wg_ref[0])) * _dot(x, wu_ref[0])
        y_ref[...] = _dot(hid.astype(BF16), wd_ref[0])

    @pl.when(b >= nu_ref[0])
    def _():
        y_ref[...] = jnp.zeros_like(y_ref)


def _experts(xbuf, blk_expert, n_used, wg, wu, wd, *, bm):
    n_slots, d = xbuf.shape
    de = wg.shape[2]
    grid_spec = pltpu.PrefetchScalarGridSpec(
        num_scalar_prefetch=2,
        grid=(n_slots // bm,),
        in_specs=[pl.BlockSpec((bm, d), lambda b, be, nu: (b, 0)),
                  pl.BlockSpec((1, d, de), lambda b, be, nu: (be[b], 0, 0)),
                  pl.BlockSpec((1, d, de), lambda b, be, nu: (be[b], 0, 0)),
                  pl.BlockSpec((1, de, d), lambda b, be, nu: (be[b], 0, 0))],
        out_specs=pl.BlockSpec((bm, d), lambda b, be, nu: (b, 0)),
    )
    return pl.pallas_call(
        _expert_kernel,
        grid_spec=grid_spec,
        out_shape=jax.ShapeDtypeStruct((n_slots, d), F32),
        compiler_params=_cparams(("arbitrary",)),
        name=f"experts_{n_slots}",
    )(blk_expert, n_used, xbuf, wg, wu, wd)


def _combine_kernel(h_ref, route_ref, p_ref, wpg_ref, wpp_ref, gfin_ref, dest_ref, ybuf_ref, y_ref,
                    y1, y2, dsm, sem, dsem, *, tm):
    i = pl.program_id(0)
    cp = pltpu.make_async_copy(dest_ref.at[pl.ds(i, 1)], dsm, dsem)
    cp.start()
    cp.wait()

    def body(r, carry):
        for kk, dst in enumerate((y1, y2)):
            d = dsm[0, 2 * r + kk]
            pltpu.make_async_copy(ybuf_ref.at[pl.ds(d, 1)], dst.at[pl.ds(r, 1)], sem).start()
        return carry

    lax.fori_loop(0, tm, body, 0, unroll=8)
    for dst in (y1, y2):
        pltpu.make_async_copy(ybuf_ref.at[pl.ds(0, tm)], dst, sem).wait()

    route = route_ref[...]
    g1 = route[:, 4:5]
    g2 = route[:, 5:6]
    h = h_ref[...] + (g1 * y1[...] + g2 * y2[...])
    gate = _sigmoid(_dot(h.astype(BF16), wpg_ref[...]))
    out = h + gate * _dot(p_ref[...].astype(BF16), wpp_ref[...])
    y_ref[...] = _rms(out, gfin_ref[...])


def _combine(h, route, p, wpg, wpp, gfin, dest, ybuf, *, tm):
    m, d = h.shape
    row = lambda w: pl.BlockSpec((tm, w), lambda i: (i, 0))
    const = lambda a: pl.BlockSpec(a.shape, lambda i: (0,) * a.ndim)
    return pl.pallas_call(
        functools.partial(_combine_kernel, tm=tm),
        grid=(m // tm,),
        in_specs=[row(d), row(LANES), row(p.shape[1]), const(wpg), const(wpp), const(gfin),
                  pl.BlockSpec(memory_space=pl.ANY), pl.BlockSpec(memory_space=pl.ANY)],
        out_specs=row(d),
        out_shape=jax.ShapeDtypeStruct((m, d), F32),
        scratch_shapes=[pltpu.VMEM((tm, d), F32), pltpu.VMEM((tm, d), F32), pltpu.SMEM((1, 2 * tm), jnp.int32),
                        pltpu.SemaphoreType.DMA, pltpu.SemaphoreType.DMA],
        compiler_params=_cparams(("arbitrary",), row_dma=True),
        name=f"combine_{m}",
    )(h, route, p, wpg, wpp, gfin, dest, ybuf)


def _tail(x2, og, z, ob, p2, wts, *, tm, bm, precise):
    m, d = x2.shape
    h, route, counts = _post(x2, og, z, ob, wts["ga"], wts["gb"], wts["bones"],
                             wts["wo_hi"] if precise else wts["wo"], wts["gf"],
                             wts["wrh"], wts["wrl"], wts["br"], wts["wo_lo"] if precise else None, tm=tm)
    eid = route[:, 0:2].astype(jnp.int32)
    rank = route[:, 2:4].astype(jnp.int32)
    sizes = counts[0, :N_EXPERTS].astype(jnp.int32)
    padded = (sizes + bm - 1) // bm * bm
    pend = jnp.cumsum(padded)
    pstart = pend - padded
    dest = (pstart[eid] + rank).reshape(m // tm, 2 * tm)
    n_blk = (2 * m) // bm + N_EXPERTS
    blk_start = jnp.arange(n_blk, dtype=jnp.int32) * bm
    blk_expert = jnp.minimum(jnp.sum(pend[None, :] <= blk_start[:, None], axis=1), N_EXPERTS - 1).astype(jnp.int32)
    n_used = (pend[-1:] // bm).astype(jnp.int32)
    xbuf = _dispatch(h, wts["gf"], dest, n_blk * bm, tm=tm)
    ybuf = _experts(xbuf, blk_expert, n_used, wts["wg"], wts["wu"], wts["wd"], bm=bm)
    return _combine(h, route, p2, wts["wpg"], wts["wpp"], wts["gfin"], dest, ybuf, tm=tm)


def _split_hi_lo(w):
    bits = lax.bitcast_convert_type(w.astype(F32), jnp.uint32)
    hi32 = lax.bitcast_convert_type(bits & jnp.uint32(0xFFFF0000), F32)
    return hi32.astype(BF16), (w - hi32).astype(BF16)


def _rope_tables(pos):
    half = ROT_DIM // 2
    inv = ROPE_THETA ** (-jnp.arange(half, dtype=F32) * (2.0 / ROT_DIM))
    ang = pos.astype(F32)[:, None] * inv[None, :]
    cos, sin = jnp.cos(ang), jnp.sin(ang)
    j = jnp.arange(LANES) % HEAD_DIM
    first, second = j < half, (j >= half) & (j < ROT_DIM)
    jj = jnp.where(second, j - half, jnp.where(first, j, 0))
    c = jnp.where((first | second)[None, :], cos[:, jj], 1.0)
    s1 = jnp.where(first[None, :], -sin[:, jj], 0.0)
    s2 = jnp.where(second[None, :], sin[:, jj], 0.0)
    return c, s1, s2


def kernel(x_prompt, x_sample, cache_win_k, cache_win_v, state_conv, state_delta, p_prompt, p_sample, g_attn_norm, w_in, w_conv, a_log, dt_bias, g_a_out, g_b_out, w_out, g_ffn_norm, w_router_group, b_router_group, w_router_expert, b_router_expert, w_exp_gate, w_exp_up, w_exp_down, w_ple_gate, w_ple_proj, g_final):
    n, t, d = x_prompt.shape
    ns = x_sample.shape[0]
    assert w_in.shape[0] == 1 and x_sample.shape[1] == 1
    ha = a_log.shape[1]
    aw = ha * HEAD_DIM
    off_a = 4 * aw
    off_win = off_a + 2 * ha
    keep =min(DILATIONS[-1][0], t)
    hi = lambda a: a.astype(F32)

    w = w_in[0]
    w_ab = jnp.pad(w[:, off_a:off_win], ((0, 0), (0, LANES - 2 * ha)))
    w_cat32 = jnp.concatenate([w[:, :off_a], w[:, off_win:], w_ab], axis=1)
    w_cat = w_cat32.astype(BF16)
    w_cat_hi, w_cat_lo = _split_hi_lo(w_cat32)
    pad_l = lambda v: jnp.pad(hi(v), (0, LANES - v.shape[0]))[None, :]
    alog = pad_l(a_log[0])
    dtb = pad_l(dt_bias[0])
    lane_head = jnp.arange(2 * aw) // HEAD_DIM
    esel = (jnp.arange(LANES)[:, None] == lane_head[None, :]).astype(BF16)
    hd = jnp.arange(aw) // HEAD_DIM
    bones = (hd[:, None] == hd[None, :]).astype(BF16)
    wr = jnp.concatenate([hi(w_router_group[0]), hi(w_router_expert[0]).reshape(d, N_EXPERTS)], axis=1)
    wr = jnp.pad(wr, ((0, 0), (0, LANES - wr.shape[1])))
    wrh, wrl = _split_hi_lo(wr)
    wo_hi, wo_lo = _split_hi_lo(w_out[0])
    br = jnp.concatenate([hi(b_router_group[0]), hi(b_router_expert[0]).reshape(N_EXPERTS)])
    wts = dict(
        ga=jnp.tile(hi(g_a_out[0]), ha)[None, :], gb=jnp.tile(hi(g_b_out[0]), ha)[None, :], bones=bones,
        wo=w_out[0].astype(BF16), wo_hi=wo_hi, wo_lo=wo_lo,
        gf=hi(g_ffn_norm[0])[None, :], wrh=wrh, wrl=wrl, br=pad_l(br),
        wg=w_exp_gate[0].astype(BF16), wu=w_exp_up[0].astype(BF16), wd=w_exp_down[0].astype(BF16),
        wpg=w_ple_gate[0].astype(BF16), wpp=w_ple_proj[0].astype(BF16), gfin=hi(g_final)[None, :])
    g_attn = hi(g_attn_norm[0])[None, :]
    wconv = hi(w_conv[0])

    tm_p = min(256, t)
    cos, s1, s2 = _rope_tables(jnp.arange(t, dtype=jnp.int32))
    qa, ka, va, z, la, bt, qb, kb, vb, conv_p = _proj(
        x_prompt, g_attn, w_cat, wconv, alog, dtb, esel, cos, s1, s2, bones, None, tm=tm_p, sample=False)
    og, s_fin = _gdn_prompt(qa, ka, va, la, bt, tt=min(512, t), unroll=4)
    ob = _band(qb, kb, vb, tt=DILATIONS[-1][0], unroll=4)
    flat = lambda a: a.reshape(n * t, a.shape[-1])
    y_prompt = _tail(flat(x_prompt), flat(og), flat(z), flat(ob), flat(p_prompt[0]), wts,
                     tm=tm_p, bm=256, precise=False).reshape(n, t, d)

    cos, s1, s2 = _rope_tables(jnp.full((1,), PAST_LEN, jnp.int32))
    st = [state_conv[0][None, :, j, :] for j in range(state_conv.shape[2])]
    xs3 = x_sample.reshape(1, ns, d)
    qa_s, ka_s, va_s, z_s, la_s, bt_s, qb_s, kb_s, vb_s, ua_s = [
        a[0] for a in _proj(xs3, g_attn, w_cat_hi, wconv, alog, dtb, esel, cos, s1, s2, bones, [w_cat_lo] + st,
                            tm=ns, sample=True)]
    s_packed = state_delta[0].transpose(0, 2, 1, 3).reshape(ns, HEAD_DIM, aw)
    og_s, s_new = _gdn_sample(qa_s, ka_s, va_s, la_s, bt_s, s_packed, bones, gs=8)
    cols = lambda a: a.reshape(ns, ha, HEAD_DIM).transpose(0, 2, 1)
    pos_minor = lambda c: c[0].transpose(0, 2, 3, 1)
    ob_s = _cache_attn(cols(qb_s), cols(kb_s), cols(vb_s), pos_minor(cache_win_k), pos_minor(cache_win_v))
    ob_s = ob_s.transpose(0, 2, 1).reshape(ns, aw)
    y_sample = _tail(x_sample.reshape(ns, d), og_s, z_s, ob_s, p_sample[0].reshape(ns, -1), wts,
                     tm=ns, bm=128, precise=True).reshape(ns, 1, d)

    unpack = lambda s: s.reshape(-1, HEAD_DIM, ha, HEAD_DIM).transpose(0, 2, 1, 3)[None]
    heads = lambda a: a.reshape(a.shape[0], -1, ha, HEAD_DIM)
    return (y_prompt, y_sample,
            heads(kb[:, t - keep:])[None], heads(vb[:, t - keep:])[None],
            conv_p[:, 8 - state_conv.shape[2]:][None], unpack(s_fin),
            heads(kb_s[:, None])[None], heads(vb_s[:, None])[None],
            jnp.concatenate([state_conv[0][:, 1:], ua_s[:, None]], axis=1)[None], unpack(s_new))
```

```python
import functools
import math

import jax
import jax.numpy as jnp
from jax import lax
from jax.experimental import pallas as pl
from jax.experimental.pallas import tpu as pltpu

F32 = jnp.float32
BF16 = jnp.bfloat16

HEAD_DIM = 64
GDN_CHUNK = 64
ROT_DIM = HEAD_DIM // 4
ROPE_THETA = 500000.0
PAST_LEN = 8192
DILATIONS = ((128, 1), (512, 4), (2048, 16))
BAND = 128
N_GROUPS = 4
EXPERTS_PER_GROUP = 8
N_EXPERTS = N_GROUPS * EXPERTS_PER_GROUP
NORM_EPS = 1e-6
NEG = -1e30
LANES = 128
VMEM_LIMIT = 56 * 1024 * 1024


def _cparams(sem, row_dma=False):
    return pltpu.CompilerParams(dimension_semantics=sem, vmem_limit_bytes=VMEM_LIMIT,
                                disable_bounds_checks=row_dma)


def _dot(a, b):
    return jnp.dot(a, b, preferred_element_type=F32)


def _dot_nt(a, b):
    return lax.dot_general(a, b, (((1,), (1,)), ((), ())), preferred_element_type=F32)


def _dot_tn(a, b):
    return lax.dot_general(a, b, (((0,), (0,)), ((), ())), preferred_element_type=F32)


def _split3(x):
    hi = x.astype(BF16)
    r = x - hi.astype(F32)
    mid = r.astype(BF16)
    lo = (r - mid.astype(F32)).astype(BF16)
    return hi, mid, lo


def _dot_f32_lhs(x, w_bf16):
    hi, mid, lo = _split3(x)
    return _dot(hi, w_bf16) + _dot(mid, w_bf16) + _dot(lo, w_bf16)


def _dot_f32_rhs(w_bf16, x):
    hi, mid, lo = _split3(x)
    return _dot(w_bf16, hi) + _dot(w_bf16, mid) + _dot(w_bf16, lo)


HEAD_SHIFT = HEAD_DIM.bit_length() - 1


def _lane_head(i):
    return lax.shift_right_logical(i, HEAD_SHIFT)


def _lane_in_head(i):
    return lax.bitwise_and(i, HEAD_DIM - 1)


def _sigmoid(x):
    return 1.0 / (1.0 + jnp.exp(-x))


def _silu(x):
    return x * _sigmoid(x)


def _softplus(x):
    return jnp.maximum(x, 0.0) + jnp.log1p(jnp.exp(-jnp.abs(x)))


def _rms(x, g):
    return x * lax.rsqrt(jnp.mean(x * x, axis=-1, keepdims=True) + NORM_EPS) * g


def _head_rms(x, bones, g, precise=False):
    ss = _dot_f32_lhs(x * x, bones) if precise else _dot((x * x).astype(BF16), bones)
    return x * lax.rsqrt(ss * (1.0 / HEAD_DIM) + NORM_EPS) * g


def _proj_kernel(*refs, tm, aw, sample):
    if sample:
        (x_ref, g_ref, w_ref, wc_ref, alog_ref, dtb_ref, esel_ref, cos_ref, s1_ref, s2_ref, bones_ref,
         wl_ref, st0_ref, st1_ref, st2_ref,
         qa_ref, ka_ref, va_ref, z_ref, la_ref, bt_ref, qb_ref, kb_ref, vb_ref, conv_ref) = refs
        buf = None
    else:
        (x_ref, g_ref, w_ref, wc_ref, alog_ref, dtb_ref, esel_ref, cos_ref, s1_ref, s2_ref, bones_ref,
         qa_ref, ka_ref, va_ref, z_ref, la_ref, bt_ref, qb_ref, kb_ref, vb_ref, conv_ref, buf) = refs
        t = pl.program_id(1)

        @pl.when(t == 0)
        def _():
            buf[0:8, :] = jnp.zeros((8, 3 * aw), F32)

    xf = _rms(x_ref[0], g_ref[...])
    xn = xf.astype(BF16)
    bones = bones_ref[...]
    if sample:
        xlo = (xf - xn.astype(F32)).astype(BF16)
        proj = lambda cs: _dot(xn, w_ref[:, cs]) + _dot(xlo, w_ref[:, cs]) + _dot(xn, wl_ref[:, cs])
        sumsq = lambda y: _dot_f32_lhs(y * y, bones)
    else:
        proj = lambda cs: _dot(xn, w_ref[:, cs])
        sumsq = lambda y: _dot((y * y).astype(BF16), bones)

    outs_a = (qa_ref, ka_ref, va_ref)
    for c in range(3):
        cs = slice(c * aw, (c + 1) * aw)
        u = proj(cs)
        if sample:
            conv_ref[0, :, cs] = u
            y = (wc_ref[3:4, cs] * u + wc_ref[2:3, cs] * st2_ref[0, :, cs]
                 + wc_ref[1:2, cs] * st1_ref[0, :, cs] + wc_ref[0:1, cs] * st0_ref[0, :, cs])
        else:
            buf[8:8 + tm, cs] = u
            y = (wc_ref[3:4, cs] * u + wc_ref[2:3, cs] * buf[7:7 + tm, cs]
                 + wc_ref[1:2, cs] * buf[6:6 + tm, cs] + wc_ref[0:1, cs] * buf[5:5 + tm, cs])
        y = _silu(y)
        if c < 2:
            y = y * lax.rsqrt(sumsq(y) + NORM_EPS)
        outs_a[c][0] = y
    if not sample:
        tail = buf[tm:tm + 8, :]
        conv_ref[0] = tail
        buf[0:8, :] = tail

    z_ref[0] = proj(slice(3 * aw, 4 * aw))

    ab = proj(slice(7 * aw, 7 * aw + LANES))
    lane = lax.broadcasted_iota(jnp.int32, ab.shape, 1)
    log_a = -jnp.exp(alog_ref[...]) * _softplus(ab + dtb_ref[...])
    comb = jnp.where(lane < aw // HEAD_DIM, log_a, _sigmoid(ab))
    ex = _dot_f32_lhs(comb, esel_ref[...])
    la_ref[0] = ex[:, :aw]
    bt_ref[0] = ex[:, aw:]

    cosv, s1v, s2v = cos_ref[...], s1_ref[...], s2_ref[...]
    for c, oref in ((0, qb_ref), (1, kb_ref)):
        for gq in range(aw // LANES):
            cs = slice(4 * aw + c * aw + gq * LANES, 4 * aw + c * aw + (gq + 1) * LANES)
            u = proj(cs)
            r = (u * cosv + pltpu.roll(u, LANES - ROT_DIM // 2, 1) * s1v
                 + pltpu.roll(u, ROT_DIM // 2, 1) * s2v)
            oref[0, :, gq * LANES:(gq + 1) * LANES] = r
    vb_ref[0] = proj(slice(6 * aw, 7 * aw))


def _proj(x3, g, w_cat, wconv, alog, dtb, esel, cos, s1, s2, bones, states, *, tm, sample):
    n, t, d = x3.shape
    aw = bones.shape[0]
    nt = t // tm
    const = lambda shape: pl.BlockSpec(shape, lambda i, j: (0,) * len(shape))
    row = lambda width: pl.BlockSpec((1, tm, width), lambda i, j: (i, j, 0))
    tab = (pl.BlockSpec((1, LANES), lambda i, j: (0, 0)) if sample
           else pl.BlockSpec((tm, LANES), lambda i, j: (j, 0)))
    in_specs = [row(d), const((1, d)), const(w_cat.shape), const(wconv.shape), const((1, LANES)),
                const((1, LANES)), const(esel.shape), tab, tab, tab, const(bones.shape)]
    args = [x3, g, w_cat, wconv, alog, dtb, esel, cos, s1, s2, bones]
    if sample:
        w_lo, states = states[0], states[1:]
        in_specs += [const(w_lo.shape)] + [row(3 * aw)] * 3
        args += [w_lo] + list(states)
        conv_shape = jax.ShapeDtypeStruct((n, t, 3 * aw), F32)
        conv_spec = row(3 * aw)
        scratch = []
    else:
        conv_shape = jax.ShapeDtypeStruct((n, 8, 3 * aw), F32)
        conv_spec = pl.BlockSpec((1, 8, 3 * aw), lambda i, j: (i, 0, 0))
        scratch = [pltpu.VMEM((tm + 8, 3 * aw), F32)]
    o = jax.ShapeDtypeStruct((n, t, aw), F32)
    return pl.pallas_call(
        functools.partial(_proj_kernel, tm=tm, aw=aw, sample=sample),
        grid=(n, nt),
        in_specs=in_specs,
        out_specs=[row(aw)] * 9 + [conv_spec],
        out_shape=[o] * 9 + [conv_shape],
        scratch_shapes=scratch,
        compiler_params=_cparams(("arbitrary", "arbitrary")),
        name="proj_sample" if sample else "proj_prompt",
    )(*args)


def _rowstack(x, hms):
    return jnp.concatenate([jnp.where(hm, x, 0.0) for hm in hms], axis=0)


def _gdn_kernel(q_ref, k_ref, v_ref, la_ref, bt_ref, ltri_ref, ones_ref, o_ref, s_out_ref,
                S, KW, NN, QP, OU, AL, *, tt, hw, unroll):
    C = GDN_CHUNK
    qw = 4 * HEAD_DIM
    nq = hw // qw
    t = pl.program_id(1)

    @pl.when(t == 0)
    def _():
        S[...] = jnp.zeros_like(S)

    rowi = lax.broadcasted_iota(jnp.int32, (C, hw), 0)
    colj = _lane_in_head(lax.broadcasted_iota(jnp.int32, (C, hw), 1))
    m_incl = colj <= rowi
    m_strict = colj < rowi
    eye = (colj == rowi).astype(F32)
    lane_q = _lane_head(lax.broadcasted_iota(jnp.int32, (1, qw), 1))
    hms = [lane_q == h for h in range(4)]
    ltri = ltri_ref[...]
    ones = ones_ref[...]

    def pm(l, r):
        return _dot(l.astype(BF16), _rowstack(r, hms).astype(BF16))

    def diag_blocks(full):
        return sum(jnp.where(hms[h], full[h * HEAD_DIM:(h + 1) * HEAD_DIM, :], 0.0) for h in range(4))

    sls = [slice(qi * qw, (qi + 1) * qw) for qi in range(nq)]

    def intra(it, carry):
        cs = [it * unroll + uu for uu in range(unroll)]
        rws = [pl.ds(pl.multiple_of(c * C, C), C) for c in cs]
        qs = [q_ref[0, r, :] * (HEAD_DIM ** -0.5) for r in rws]
        ks = [k_ref[0, r, :] for r in rws]
        bs = [bt_ref[0, r, :] for r in rws]
        gs = [_dot_f32_rhs(ltri, la_ref[0, r, :]) for r in rws]
        grs = [_dot_f32_rhs(ones, g * eye) for g in gs]
        decs = [jnp.exp(jnp.where(m_incl, g - gr, NEG)) for g, gr in zip(gs, grs)]
        egs = [jnp.exp(g) for g in gs]
        kbs = [k * b for k, b in zip(ks, bs)]
        for c, g in zip(cs, gs):
            AL[pl.ds(pl.multiple_of(c * 8, 8), 8), :] = jnp.broadcast_to(jnp.exp(g[C - 1:C, :]), (8, hw))
        ch = [(ui, sl) for ui in range(unroll) for sl in sls]
        kst = [_rowstack(ks[ui][:, sl], hms).astype(BF16) for ui, sl in ch]
        kk = [_dot_nt(kbs[ui][:, sl].astype(BF16), kst[i]) for i, (ui, sl) in enumerate(ch)]
        x = [-jnp.where(m_strict[:, sl], kk[i] * decs[ui][:, sl], 0.0) for i, (ui, sl) in enumerate(ch)]
        p = [eye[:, sl] + x[i] for i, (ui, sl) in enumerate(ch)]
        for _ in range(int(math.log2(C)) - 1):
            x = [pm(xi, xi) for xi in x]
            px = [pm(pi, xi) for pi, xi in zip(p, x)]
            p = [pi + pxi for pi, pxi in zip(p, px)]
        u = [pm(p[i], (v_ref[0, rws[ui], sl] * bs[ui][:, sl])) for i, (ui, sl) in enumerate(ch)]
        w = [pm(p[i], kbs[ui][:, sl] * egs[ui][:, sl]) for i, (ui, sl) in enumerate(ch)]
        qk = [jnp.where(m_incl[:, sl], _dot_nt(qs[ui][:, sl].astype(BF16), kst[i]) * decs[ui][:, sl], 0.0)
              .astype(BF16) for i, (ui, sl) in enumerate(ch)]
        kuw = []
        for i, (ui, sl) in enumerate(ch):
            g = gs[ui][:, sl]
            kd = (ks[ui][:, sl] * jnp.exp(g[C - 1:C, :] - g)).astype(BF16)
            uw = jnp.concatenate([u[i], w[i]], axis=1).astype(BF16)
            kuw.append(_dot_tn(kd, uw))
        ou = [_dot(qk[i], _rowstack(u[i], hms).astype(BF16)) for i in range(len(ch))]
        qw_ = [_dot(qk[i], _rowstack(w[i], hms).astype(BF16)) for i in range(len(ch))]
        for i, (ui, sl) in enumerate(ch):
            krows = pl.ds(pl.multiple_of(cs[ui] * HEAD_DIM, HEAD_DIM), HEAD_DIM)
            NN[krows, sl] = diag_blocks(kuw[i][:, :qw])
            KW[krows, sl] = diag_blocks(kuw[i][:, qw:])
            OU[rws[ui], sl] = ou[i]
            QP[rws[ui], sl] = qs[ui][:, sl] * egs[ui][:, sl] - qw_[i]
        return carry

    lax.fori_loop(0, tt // C // unroll, intra, 0)

    def inter(c, carry):
        rows = pl.ds(pl.multiple_of(c * C, C), C)
        krows = pl.ds(pl.multiple_of(c * HEAD_DIM, HEAD_DIM), HEAD_DIM)
        al = AL[pl.ds(pl.multiple_of(c * 8, 8), 1), :]
        ss = [S[:, sl] for sl in sls]
        lhs = [jnp.concatenate([KW[krows, sl], QP[rows, sl]], axis=0).astype(BF16) for sl in sls]
        rs = [_dot(l, _rowstack(s, hms).astype(BF16)) for l, s in zip(lhs, ss)]
        for sl, s, r in zip(sls, ss, rs):
            S[:, sl] = s * al[:, sl] - r[:HEAD_DIM] + NN[krows, sl]
            o_ref[0, rows, sl] = r[HEAD_DIM:] + OU[rows, sl]
        return carry

    lax.fori_loop(0, tt // C, inter, 0)
    s_out_ref[0] = S[...]


def _gdn_prompt(q, k, v, la, bt, *, tt, unroll):
    n, t, hw = q.shape
    C = GDN_CHUNK
    ltri = jnp.tril(jnp.ones((C, C), F32)).astype(BF16)
    ones = jnp.ones((C, C), BF16)
    row = pl.BlockSpec((1, tt, hw), lambda i, j: (i, j, 0))
    cst = pl.BlockSpec((C, C), lambda i, j: (0, 0))
    nch = tt // C
    return pl.pallas_call(
        functools.partial(_gdn_kernel, tt=tt, hw=hw, unroll=unroll),
        grid=(n, t // tt),
        in_specs=[row] * 5 + [cst, cst],
        out_specs=[row, pl.BlockSpec((1, HEAD_DIM, hw), lambda i, j: (i, 0, 0))],
        out_shape=[jax.ShapeDtypeStruct((n, t, hw), F32), jax.ShapeDtypeStruct((n, HEAD_DIM, hw), F32)],
        scratch_shapes=[pltpu.VMEM((HEAD_DIM, hw), F32), pltpu.VMEM((nch * HEAD_DIM, hw), F32),
                        pltpu.VMEM((nch * HEAD_DIM, hw), F32), pltpu.VMEM((tt, hw), F32),
                        pltpu.VMEM((tt, hw), F32), pltpu.VMEM((nch * 8, hw), F32)],
        compiler_params=_cparams(("arbitrary", "arbitrary")),
        name="gdn_prompt",
    )(q, k, v, la, bt, ltri, ones)


def _gdn_step_kernel(q_ref, k_ref, v_ref, la_ref, bt_ref, s_ref, bones_ref, o_ref, s_out_ref, *, gs, hw):
    rowi = lax.broadcasted_iota(jnp.int32, (HEAD_DIM, hw), 0)
    colj = _lane_in_head(lax.broadcasted_iota(jnp.int32, (HEAD_DIM, hw), 1))
    eye = (colj == rowi).astype(F32)
    bones = bones_ref[...]
    for i in range(gs):
        r = slice(i, i + 1)
        q = q_ref[r, :] * (HEAD_DIM ** -0.5)
        k = k_ref[r, :]
        v = v_ref[r, :]
        eg = jnp.exp(la_ref[r, :])
        b = bt_ref[r, :]
        s = s_ref[i]
        kbc = _dot_f32_lhs(eye * k, bones)
        qbc = _dot_f32_lhs(eye * q, bones)
        ks = jnp.sum(kbc * s, axis=0, keepdims=True)
        qs = jnp.sum(qbc * s, axis=0, keepdims=True)
        qk = jnp.sum(kbc * qbc, axis=0, keepdims=True)
        vn = b * v - b * eg * ks
        o_ref[r, :] = eg * qs + qk * vn
        s_out_ref[i] = s * eg + kbc * vn


def _gdn_sample(q, k, v, la, bt, s_packed, bones, *, gs):
    m, hw = q.shape
    row = pl.BlockSpec((gs, hw), lambda i: (i, 0))
    st = pl.BlockSpec((gs, HEAD_DIM, hw), lambda i: (i, 0, 0))
    return pl.pallas_call(
        functools.partial(_gdn_step_kernel, gs=gs, hw=hw),
        grid=(m // gs,),
        in_specs=[row] * 5 + [st, pl.BlockSpec(bones.shape, lambda i: (0, 0))],
        out_specs=[row, st],
        out_shape=[jax.ShapeDtypeStruct((m, hw), F32), jax.ShapeDtypeStruct(s_packed.shape, F32)],
        compiler_params=_cparams(("arbitrary",)),
        name="gdn_sample",
    )(q, k, v, la, bt, s_packed, bones)


def _band_kernel(q_ref, k_ref, v_ref, o_ref, kf, vf, oc, lc, *, tt, unroll):
    j = pl.program_id(2)

    @pl.when(j == 0)
    def _():
        kf[0:tt, :] = jnp.zeros((tt, LANES), F32)
        vf[0:tt, :] = jnp.zeros((tt, LANES), F32)

    kf[tt:, :] = k_ref[0]
    vf[tt:, :] = v_ref[0]
    a = lax.bitwise_and(lax.broadcasted_iota(jnp.int32, (2 * BAND, 2 * BAND), 0), BAND - 1)
    c = lax.broadcasted_iota(jnp.int32, (2 * BAND, 2 * BAND), 1)
    band = (c >= a) & (c <= a + BAND)
    row_h1 = lax.broadcasted_iota(jnp.int32, (2 * BAND, LANES), 0) >= BAND
    lane_h1 = lax.broadcasted_iota(jnp.int32, (2 * BAND, LANES), 1) >= HEAD_DIM
    own = row_h1 == lane_h1
    out_h1 = lax.broadcasted_iota(jnp.int32, (BAND, LANES), 1) >= HEAD_DIM

    for ci, (_, d) in enumerate(DILATIONS):
        nblk = tt // (BAND * d)

        def body(it, carry, d=d, ci=ci, nblk=nblk):
            rows, krows, firsts = [], [], []
            for uu in range(unroll):
                idx = it * unroll + uu
                r = idx // nblk
                b = idx % nblk
                start = b * (BAND * d) + r
                if d == 1:
                    rows.append(pl.ds(start, BAND))
                    krows.append(pl.ds(tt + start - BAND, 2 * BAND))
                else:
                    rows.append(pl.ds(start, BAND, stride=d))
                    krows.append(pl.ds(tt + start - BAND * d, 2 * BAND, stride=d))
                firsts.append(jnp.logical_and(j == 0, b == 0))
            ss = []
            for rw, kr in zip(rows, krows):
                qb = q_ref[0, rw, :] * (HEAD_DIM ** -0.5)
                qs = jnp.where(own, jnp.concatenate([qb, qb], axis=0), 0.0).astype(BF16)
                ss.append(_dot_nt(qs, kf[kr, :].astype(BF16)))
            pcat, stats = [], []
            for s, first in zip(ss, firsts):
                s = jnp.where(band & (c >= jnp.where(first, BAND, 0)), s, NEG)
                ps, st = [], []
                for hh in range(2):
                    sh = s[hh * BAND:(hh + 1) * BAND]
                    m = jnp.max(sh, axis=-1, keepdims=True)
                    p = jnp.exp(sh - m)
                    l = jnp.sum(p, axis=-1, keepdims=True)
                    ps.append(p.astype(BF16))
                    st.append((l, m + jnp.log(l)))
                pcat.append(jnp.concatenate(ps, axis=0))
                stats.append(st)
            pvs = [_dot(pc, vf[kr, :].astype(BF16)) for pc, kr in zip(pcat, krows)]
            for rw, pv, st in zip(rows, pvs, stats):
                oc[ci, rw, :] = jnp.where(out_h1, pv[BAND:] / st[1][0], pv[:BAND] / st[0][0])
                lc[ci, rw, :] = jnp.where(out_h1, st[1][1], st[0][1])
            return carry

        lax.fori_loop(0, tt // BAND // unroll, body, 0)

    mrows = 256
    for ch in range(tt // mrows):
        rs = slice(ch * mrows, (ch + 1) * mrows)
        ls = [lc[ci, rs, :] for ci in range(len(DILATIONS))]
        mx = functools.reduce(jnp.maximum, ls)
        es = [jnp.exp(l - mx) for l in ls]
        o_ref[0, rs, :] = sum(e * oc[ci, rs, :] for ci, e in enumerate(es)) / sum(es)

    kf[0:tt, :] = kf[tt:, :]
    vf[0:tt, :] = vf[tt:, :]


def _band(q, k, v, *, tt, unroll):
    n, t, hw = q.shape
    assert all(w // d == BAND and tt % w == 0 for w, d in DILATIONS) and t % tt == 0
    blk = pl.BlockSpec((1, tt, LANES), lambda i, c, j: (i, j, c))
    nd = len(DILATIONS)
    return pl.pallas_call(
        functools.partial(_band_kernel, tt=tt, unroll=unroll),
        grid=(n, hw // LANES, t // tt),
        in_specs=[blk, blk, blk],
        out_specs=blk,
        out_shape=jax.ShapeDtypeStruct((n, t, hw), F32),
        scratch_shapes=[pltpu.VMEM((2 * tt, LANES), F32), pltpu.VMEM((2 * tt, LANES), F32),
                        pltpu.VMEM((nd, tt, LANES), F32), pltpu.VMEM((nd, tt, LANES), F32)],
        compiler_params=_cparams(("arbitrary", "arbitrary", "arbitrary")),
        name="band",
    )(q, k, v)


def _cache_attn_kernel(q_ref, kn_ref, vn_ref, k_ref, v_ref, o_ref, *, nh, n_past):
    pos = lax.broadcasted_iota(jnp.int32, (1, n_past), 1)
    cnt = jnp.zeros((1, n_past), F32)
    for window, dil in DILATIONS:
        hit = (pos >= n_past - window) & (lax.bitwise_and(pos, dil - 1) == 0)
        cnt = cnt + jnp.where(hit, 1.0, 0.0)
    live = cnt > 0.0
    nd = float(len(DILATIONS))
    lane_h = lax.broadcasted_iota(jnp.int32, (HEAD_DIM, nh), 1)
    out = jnp.zeros((HEAD_DIM, nh), F32)
    for h in range(nh):
        q = q_ref[0, :, h:h + 1] * (HEAD_DIM ** -0.5)
        s = jnp.where(live, jnp.sum(k_ref[0, h] * q, axis=0, keepdims=True), NEG)
        s0 = jnp.sum(q * kn_ref[0, :, h:h + 1], axis=0, keepdims=True)
        m = jnp.maximum(jnp.max(s, axis=-1, keepdims=True), s0)
        p = cnt * jnp.exp(s - m)
        p0 = nd * jnp.exp(s0 - m)
        den = p0 + jnp.sum(p, axis=-1, keepdims=True)
        num = p0 * vn_ref[0, :, h:h + 1] + jnp.sum(v_ref[0, h] * p, axis=-1, keepdims=True)
        out = jnp.where(lane_h == h, num / den, out)
    o_ref[0] = out


def _cache_attn(q_t, kn_t, vn_t, ck_t, cv_t):
    m, hd, nh = q_t.shape
    n_past = ck_t.shape[3]
    assert all(w // d == BAND and n_past >= w and d & (d - 1) == 0 and n_past % d == 0 for w, d in DILATIONS)
    col = pl.BlockSpec((1, hd, nh), lambda i: (i, 0, 0))
    cache = pl.BlockSpec((1, nh, hd, n_past), lambda i: (i, 0, 0, 0))
    return pl.pallas_call(
        functools.partial(_cache_attn_kernel, nh=nh, n_past=n_past),
        grid=(m,),
        in_specs=[col, col, col, cache, cache],
        out_specs=col,
        out_shape=jax.ShapeDtypeStruct((m, hd, nh), F32),
        compiler_params=_cparams(("arbitrary",)),
        name="cache_attn",
    )(q_t, kn_t, vn_t, ck_t, cv_t)


def _post_kernel(*refs, tm, precise):
    if precise:
        (x_ref, og_ref, z_ref, ob_ref, ga_ref, gb_ref, bones_ref, wo_ref, gf_ref, wrh_ref, wrl_ref, br_ref,
         wol_ref, h_ref, route_ref, cnt_ref, cnt) = refs
    else:
        (x_ref, og_ref, z_ref, ob_ref, ga_ref, gb_ref, bones_ref, wo_ref, gf_ref, wrh_ref, wrl_ref, br_ref,
         h_ref, route_ref, cnt_ref, cnt) = refs
    i = pl.program_id(0)

    @pl.when(i == 0)
    def _():
        cnt[...] = jnp.zeros_like(cnt)

    bones = bones_ref[...]
    oa = _head_rms(og_ref[...], bones, ga_ref[...], precise) * _silu(z_ref[...])
    ob = _head_rms(ob_ref[...], bones, gb_ref[...], precise)
    mixf = jnp.concatenate([oa, ob], axis=-1)
    mix = mixf.astype(BF16)
    proj = _dot(mix, wo_ref[...])
    if precise:
        proj = proj + _dot((mixf - mix.astype(F32)).astype(BF16), wo_ref[...]) + _dot(mix, wol_ref[...])
    h = x_ref[...] + proj
    h_ref[...] = h

    mrow = _rms(h, gf_ref[...])
    mh = mrow.astype(BF16)
    ml = (mrow - mh.astype(F32)).astype(BF16)
    logit = _dot(mh, wrh_ref[...]) + _dot(mh, wrl_ref[...]) + _dot(ml, wrh_ref[...]) + br_ref[...]
    lane = lax.broadcasted_iota(jnp.int32, logit.shape, 1).astype(F32)
    gl = jnp.where(lane < N_GROUPS, logit, NEG)
    gmax = jnp.max(gl, axis=-1, keepdims=True)
    grp = jnp.min(jnp.where(gl == gmax, lane, 1e9), axis=-1, keepdims=True)
    pg = 1.0 / jnp.sum(jnp.exp(gl - gmax), axis=-1, keepdims=True)
    lo = N_GROUPS + grp * EXPERTS_PER_GROUP
    el = jnp.where((lane >= lo) & (lane < lo + EXPERTS_PER_GROUP), logit, NEG)
    v1 = jnp.max(el, axis=-1, keepdims=True)
    i1 = jnp.min(jnp.where(el == v1, lane, 1e9), axis=-1, keepdims=True)
    el2 = jnp.where(lane == i1, NEG, el)
    v2 = jnp.max(el2, axis=-1, keepdims=True)
    i2 = jnp.min(jnp.where(el2 == v2, lane, 1e9), axis=-1, keepdims=True)
    e = jnp.exp(v2 - v1)
    g1 = pg / (1.0 + e)
    g2 = pg * e / (1.0 + e)
    e1 = i1 - N_GROUPS
    e2 = i2 - N_GROUPS

    oh1 = lane == e1
    oh2 = lane == e2
    onehot = jnp.where(oh1 | oh2, 1.0, 0.0)
    ri = lax.broadcasted_iota(jnp.int32, (tm, tm), 0)
    ci = lax.broadcasted_iota(jnp.int32, (tm, tm), 1)
    tri = jnp.where(ci < ri, 1.0, 0.0).astype(BF16)
    before = _dot(tri, onehot.astype(BF16)) + cnt[...]
    r1 = jnp.sum(jnp.where(oh1, before, 0.0), axis=-1, keepdims=True)
    r2 = jnp.sum(jnp.where(oh2, before, 0.0), axis=-1, keepdims=True)
    cnt[...] = cnt[...] + jnp.sum(onehot, axis=0, keepdims=True)
    cnt_ref[...] = cnt[...]
    route = jnp.zeros_like(logit)
    for j, val in enumerate((e1, e2, r1, r2, g1, g2)):
        route = jnp.where(lane == j, val, route)
    route_ref[...] = route


def _post(x2, og, z, ob, ga, gb, bones, wo, gf, wrh, wrl, br, wo_lo, *, tm):
    m, d = x2.shape
    hw = og.shape[1]
    row = lambda w: pl.BlockSpec((tm, w), lambda i: (i, 0))
    const = lambda a: pl.BlockSpec(a.shape, lambda i: (0,) * a.ndim)
    consts = [ga, gb, bones, wo, gf, wrh, wrl, br] + ([] if wo_lo is None else [wo_lo])
    return pl.pallas_call(
        functools.partial(_post_kernel, tm=tm, precise=wo_lo is not None),
        grid=(m // tm,),
        in_specs=[row(d), row(hw), row(hw), row(hw)] + [const(a) for a in consts],
        out_specs=[row(d), row(LANES), pl.BlockSpec((1, LANES), lambda i: (0, 0))],
        out_shape=[jax.ShapeDtypeStruct((m, d), F32), jax.ShapeDtypeStruct((m, LANES), F32),
                   jax.ShapeDtypeStruct((1, LANES), F32)],
        scratch_shapes=[pltpu.VMEM((1, LANES), F32)],
        compiler_params=_cparams(("arbitrary",)),
        name=f"post_{m}",
    )(x2, og, z, ob, *consts)


ROW_TILE = 8


def _rows_from_tiles(ref, n):
    return jnp.concatenate([ref[pl.ds(c, n, stride=ROW_TILE), :] for c in range(ROW_TILE)], axis=1)


def _rows_to_tiles(ref, rows):
    n = rows.shape[0]
    for c in range(ROW_TILE):
        ref[pl.ds(c, n, stride=ROW_TILE), :] = rows[:, c * LANES:(c + 1) * LANES]


def _dispatch_kernel(pstart_ref, pend_ref, h_ref, gf_ref, dest_ref, xbuf_ref, mrow, zbuf, dsm, sem, dsem, zsem,
                     *, tm, bm, nsteps, n_blk):
    i = pl.program_id(0)
    slot = i % 2
    dcp = pltpu.make_async_copy(dest_ref.at[i], dsm, dsem)
    dcp.start()

    def zero_copy(e):
        start = pl.multiple_of((pend_ref[e] - bm) * ROW_TILE, bm * ROW_TILE)
        return pltpu.make_async_copy(zbuf, xbuf_ref.at[pl.ds(start, bm * ROW_TILE)], zsem)

    def zero_block(b):
        start = pl.multiple_of(b * (bm * ROW_TILE), bm * ROW_TILE)
        return pltpu.make_async_copy(zbuf, xbuf_ref.at[pl.ds(start, bm * ROW_TILE)], zsem)

    @pl.when(i == 0)
    def _():
        zbuf[...] = jnp.zeros_like(zbuf)
        n_used = pend_ref[N_EXPERTS - 1] // bm
        for start_or_wait in (True, False):
            for e in range(N_EXPERTS):
                @pl.when(pend_ref[e] > pstart_ref[e])
                def _(e=e, start_or_wait=start_or_wait):
                    zero_copy(e).start() if start_or_wait else zero_copy(e).wait()

            def tail(b, carry, start_or_wait=start_or_wait):
                zero_block(b).start() if start_or_wait else zero_block(b).wait()
                return carry

            lax.fori_loop(n_used, n_blk, tail, 0)

    def wait_rows(s):
        for _ in range(2):
            pltpu.make_async_copy(mrow.at[s], xbuf_ref.at[pl.ds(0, tm * ROW_TILE)], sem.at[s]).wait()

    @pl.when(i >= 2)
    def _():
        wait_rows(slot)

    m = _rms(h_ref[...], gf_ref[...])
    dcp.wait()

    for s in range(2):
        @pl.when(slot == s)
        def _(s=s):
            _rows_to_tiles(mrow.at[s], m)

            def body(r, carry):
                src = mrow.at[s, pl.ds(pl.multiple_of(r * ROW_TILE, ROW_TILE), ROW_TILE)]
                for kk in range(2):
                    d = pl.multiple_of(dsm[2 * r + kk] * ROW_TILE, ROW_TILE)
                    pltpu.make_async_copy(src, xbuf_ref.at[pl.ds(d, ROW_TILE)], sem.at[s]).start()
                return carry

            lax.fori_loop(0, tm, body, 0, unroll=8)

    @pl.when(i == nsteps - 1)
    def _():
        wait_rows(slot)
        if nsteps >= 2:
            wait_rows(1 - slot)


def _dispatch(h, gf, dest, pstart, pend, n_slots, *, tm, bm):
    m, d = h.shape
    assert d == ROW_TILE * LANES
    nsteps = m // tm
    grid_spec = pltpu.PrefetchScalarGridSpec(
        num_scalar_prefetch=2,
        grid=(nsteps,),
        in_specs=[pl.BlockSpec((tm, d), lambda i, ps, pe: (i, 0)), pl.BlockSpec((1, d), lambda i, ps, pe: (0, 0)),
                  pl.BlockSpec(memory_space=pl.ANY)],
        out_specs=pl.BlockSpec(memory_space=pl.ANY),
        scratch_shapes=[pltpu.VMEM((2, tm * ROW_TILE, LANES), F32), pltpu.VMEM((bm * ROW_TILE, LANES), F32),
                        pltpu.SMEM((2 * tm,), jnp.int32),
                        pltpu.SemaphoreType.DMA((2,)), pltpu.SemaphoreType.DMA, pltpu.SemaphoreType.DMA],
    )
    return pl.pallas_call(
        functools.partial(_dispatch_kernel, tm=tm, bm=bm, nsteps=nsteps, n_blk=n_slots // bm),
        grid_spec=grid_spec,
        out_shape=jax.ShapeDtypeStruct((n_slots * ROW_TILE, LANES), F32),
        compiler_params=_cparams(("arbitrary",), row_dma=True),
        name=f"dispatch_{m}",
    )(pstart, pend, h, gf, dest)


def _expert_kernel(be_ref, nu_ref, x_ref, wg_ref, wu_ref, wd_ref, y_ref, *, bm):
    b = pl.program_id(0)

    @pl.when(b < nu_ref[0])
    def _():
        x = _rows_from_tiles(x_ref, bm).astype(BF16)
        hid = _silu(_dot(x, wg_ref[0])) * _dot(x, wu_ref[0])
        _rows_to_tiles(y_ref, _dot(hid.astype(BF16), wd_ref[0]))

    @pl.when(b >= nu_ref[0])
    def _():
        y_ref[...] = jnp.zeros_like(y_ref)


def _experts(xbuf, blk_expert, n_used, wg, wu, wd, *, bm):
    d, de = wg.shape[1], wg.shape[2]
    n_slots = xbuf.shape[0] // ROW_TILE
    blk = (bm * ROW_TILE, LANES)
    grid_spec = pltpu.PrefetchScalarGridSpec(
        num_scalar_prefetch=2,
        grid=(n_slots // bm,),
        in_specs=[pl.BlockSpec(blk, lambda b, be, nu: (jnp.minimum(b, nu[0] - 1), 0)),
                  pl.BlockSpec((1, d, de), lambda b, be, nu: (be[b], 0, 0)),
                  pl.BlockSpec((1, d, de), lambda b, be, nu: (be[b], 0, 0)),
                  pl.BlockSpec((1, de, d), lambda b, be, nu: (be[b], 0, 0))],
        out_specs=pl.BlockSpec(blk, lambda b, be, nu: (b, 0)),
    )
    return pl.pallas_call(
        functools.partial(_expert_kernel, bm=bm),
        grid_spec=grid_spec,
        out_shape=jax.ShapeDtypeStruct(xbuf.shape, F32),
        compiler_params=_cparams(("arbitrary",)),
        name=f"experts_{n_slots}",
    )(blk_expert, n_used, xbuf, wg, wu, wd)


def _combine_kernel(h_ref, route_ref, p_ref, wpg_ref, wpp_ref, gfin_ref, dest_ref, ybuf_ref, y_ref,
                    ys, dsm0, dsm1, sem, dsem, *, tm, nsteps):
    i = pl.program_id(0)
    slot = i % 2

    def issue(j, s):
        dsm = (dsm0, dsm1)[s]
        cp = pltpu.make_async_copy(dest_ref.at[j], dsm, dsem)
        cp.start()
        cp.wait()

        def body(r, carry):
            for kk in range(2):
                d = pl.multiple_of(dsm[2 * r + kk] * ROW_TILE, ROW_TILE)
                dst = ys.at[s, kk, pl.ds(pl.multiple_of(r * ROW_TILE, ROW_TILE), ROW_TILE)]
                pltpu.make_async_copy(ybuf_ref.at[pl.ds(d, ROW_TILE)], dst, sem.at[s]).start()
            return carry

        lax.fori_loop(0, tm, body, 0, unroll=8)

    @pl.when(i == 0)
    def _():
        issue(0, 0)

    for s in range(2):
        @pl.when(jnp.logical_and(i + 1 < nsteps, slot == 1 - s))
        def _(s=s):
            issue(i + 1, s)

    for kk in range(2):
        pltpu.make_async_copy(ybuf_ref.at[pl.ds(0, tm * ROW_TILE)], ys.at[slot, kk], sem.at[slot]).wait()

    route = route_ref[...]
    g1 = route[:, 4:5]
    g2 = route[:, 5:6]
    h = h_ref[...] + (g1 * _rows_from_tiles(ys.at[slot, 0], tm) + g2 * _rows_from_tiles(ys.at[slot, 1], tm))
    gate = _sigmoid(_dot(h.astype(BF16), wpg_ref[...]))
    out = h + gate * _dot(p_ref[...].astype(BF16), wpp_ref[...])
    y_ref[...] = _rms(out, gfin_ref[...])


def _combine(h, route, p, wpg, wpp, gfin, dest, ybuf, *, tm):
    m, d = h.shape
    row = lambda w: pl.BlockSpec((tm, w), lambda i: (i, 0))
    const = lambda a: pl.BlockSpec(a.shape, lambda i: (0,) * a.ndim)
    return pl.pallas_call(
        functools.partial(_combine_kernel, tm=tm, nsteps=m // tm),
        grid=(m // tm,),
        in_specs=[row(d), row(LANES), row(p.shape[1]), const(wpg), const(wpp), const(gfin),
                  pl.BlockSpec(memory_space=pl.ANY), pl.BlockSpec(memory_space=pl.ANY)],
        out_specs=row(d),
        out_shape=jax.ShapeDtypeStruct((m, d), F32),
        scratch_shapes=[pltpu.VMEM((2, 2, tm * ROW_TILE, LANES), F32), pltpu.SMEM((2 * tm,), jnp.int32),
                        pltpu.SMEM((2 * tm,), jnp.int32), pltpu.SemaphoreType.DMA((2,)), pltpu.SemaphoreType.DMA],
        compiler_params=_cparams(("arbitrary",), row_dma=True),
        name=f"combine_{m}",
    )(h, route, p, wpg, wpp, gfin, dest, ybuf)


def _tail(x2, og, z, ob, p2, wts, *, tm, bm, precise):
    m, d = x2.shape
    h, route, counts = _post(x2, og, z, ob, wts["ga"], wts["gb"], wts["bones"],
                             wts["wo_hi"] if precise else wts["wo"], wts["gf"],
                             wts["wrh"], wts["wrl"], wts["br"], wts["wo_lo"] if precise else None, tm=tm)
    eid = route[:, 0:2].astype(jnp.int32)
    rank = route[:, 2:4].astype(jnp.int32)
    sizes = counts[0, :N_EXPERTS].astype(jnp.int32)
    padded = (sizes + bm - 1) // bm * bm
    pend = jnp.cumsum(padded)
    pstart = pend - padded
    dest = (pstart[eid] + rank).reshape(m // tm, 2 * tm)
    n_blk = (2 * m) // bm + N_EXPERTS
    blk_start = jnp.arange(n_blk, dtype=jnp.int32) * bm
    blk_expert = jnp.minimum(jnp.sum(pend[None, :] <= blk_start[:, None], axis=1), N_EXPERTS - 1).astype(jnp.int32)
    n_used = (pend[-1:] // bm).astype(jnp.int32)
    xbuf = _dispatch(h, wts["gf"], dest, pstart.astype(jnp.int32), pend.astype(jnp.int32), n_blk * bm,
                     tm=tm, bm=bm)
    ybuf = _experts(xbuf, blk_expert, n_used, wts["wg"], wts["wu"], wts["wd"], bm=bm)
    return _combine(h, route, p2, wts["wpg"], wts["wpp"], wts["gfin"], dest, ybuf, tm=tm)


def _split_hi_lo(w):
    bits = lax.bitcast_convert_type(w.astype(F32), jnp.uint32)
    hi32 = lax.bitcast_convert_type(bits & jnp.uint32(0xFFFF0000), F32)
    return hi32.astype(BF16), (w - hi32).astype(BF16)


def _rope_tables(pos):
    half = ROT_DIM // 2
    inv = ROPE_THETA ** (-jnp.arange(half, dtype=F32) * (2.0 / ROT_DIM))
    ang = pos.astype(F32)[:, None] * inv[None, :]
    cos, sin = jnp.cos(ang), jnp.sin(ang)
    j = jnp.arange(LANES) % HEAD_DIM
    first, second = j < half, (j >= half) & (j < ROT_DIM)
    jj = jnp.where(second, j - half, jnp.where(first, j, 0))
    c = jnp.where((first | second)[None, :], cos[:, jj], 1.0)
    s1 = jnp.where(first[None, :], -sin[:, jj], 0.0)
    s2 = jnp.where(second[None, :], sin[:, jj], 0.0)
    return c, s1, s2


def kernel(x_prompt, x_sample, cache_win_k, cache_win_v, state_conv, state_delta, p_prompt, p_sample, g_attn_norm, w_in, w_conv, a_log, dt_bias, g_a_out, g_b_out, w_out, g_ffn_norm, w_router_group, b_router_group, w_router_expert, b_router_expert, w_exp_gate, w_exp_up, w_exp_down, w_ple_gate, w_ple_proj, g_final):
    n, t, d = x_prompt.shape
    ns = x_sample.shape[0]
    assert w_in.shape[0] == 1 and x_sample.shape[1] == 1
    ha = a_log.shape[1]
    aw = ha * HEAD_DIM
    off_a = 4 * aw
    off_win = off_a + 2 * ha
    keep =min(DILATIONS[-1][0], t)
    hi = lambda a: a.astype(F32)

    w = w_in[0]
    w_ab = jnp.pad(w[:, off_a:off_win], ((0, 0), (0, LANES - 2 * ha)))
    w_cat32 = jnp.concatenate([w[:, :off_a], w[:, off_win:], w_ab], axis=1)
    w_cat = w_cat32.astype(BF16)
    w_cat_hi, w_cat_lo = _split_hi_lo(w_cat32)
    pad_l = lambda v: jnp.pad(hi(v), (0, LANES - v.shape[0]))[None, :]
    alog = pad_l(a_log[0])
    dtb = pad_l(dt_bias[0])
    lane_head = jnp.arange(2 * aw) // HEAD_DIM
    esel = (jnp.arange(LANES)[:, None] == lane_head[None, :]).astype(BF16)
    hd = jnp.arange(aw) // HEAD_DIM
    bones = (hd[:, None] == hd[None, :]).astype(BF16)
    wr = jnp.concatenate([hi(w_router_group[0]), hi(w_router_expert[0]).reshape(d, N_EXPERTS)], axis=1)
    wr = jnp.pad(wr, ((0, 0), (0, LANES - wr.shape[1])))
    wrh, wrl = _split_hi_lo(wr)
    wo_hi, wo_lo = _split_hi_lo(w_out[0])
    br = jnp.concatenate([hi(b_router_group[0]), hi(b_router_expert[0]).reshape(N_EXPERTS)])
    wts = dict(
        ga=jnp.tile(hi(g_a_out[0]), ha)[None, :], gb=jnp.tile(hi(g_b_out[0]), ha)[None, :], bones=bones,
        wo=w_out[0].astype(BF16), wo_hi=wo_hi, wo_lo=wo_lo,
        gf=hi(g_ffn_norm[0])[None, :], wrh=wrh, wrl=wrl, br=pad_l(br),
        wg=w_exp_gate[0].astype(BF16), wu=w_exp_up[0].astype(BF16), wd=w_exp_down[0].astype(BF16),
        wpg=w_ple_gate[0].astype(BF16), wpp=w_ple_proj[0].astype(BF16), gfin=hi(g_final)[None, :])
    g_attn = hi(g_attn_norm[0])[None, :]
    wconv = hi(w_conv[0])

    tm_p = min(256, t)
    cos, s1, s2 = _rope_tables(jnp.arange(t, dtype=jnp.int32))
    qa, ka, va, z, la, bt, qb, kb, vb, conv_p = _proj(
        x_prompt, g_attn, w_cat, wconv, alog, dtb, esel, cos, s1, s2, bones, None, tm=tm_p, sample=False)
    og, s_fin = _gdn_prompt(qa, ka, va, la, bt, tt=min(512, t), unroll=4)
    ob = _band(qb, kb, vb, tt=DILATIONS[-1][0], unroll=4)
    flat = lambda a: a.reshape(n * t, a.shape[-1])
    y_prompt = _tail(flat(x_prompt), flat(og), flat(z), flat(ob), flat(p_prompt[0]), wts,
                     tm=min(512, t), bm=256, precise=False).reshape(n, t, d)

    cos, s1, s2 = _rope_tables(jnp.full((1,), PAST_LEN, jnp.int32))
    st = [state_conv[0][None, :, j, :] for j in range(state_conv.shape[2])]
    xs3 = x_sample.reshape(1, ns, d)
    qa_s, ka_s, va_s, z_s, la_s, bt_s, qb_s, kb_s, vb_s, ua_s = [
        a[0] for a in _proj(xs3, g_attn, w_cat_hi, wconv, alog, dtb, esel, cos, s1, s2, bones, [w_cat_lo] + st,
                            tm=ns, sample=True)]
    s_packed = state_delta[0].transpose(0, 2, 1, 3).reshape(ns, HEAD_DIM, aw)
    og_s, s_new = _gdn_sample(qa_s, ka_s, va_s, la_s, bt_s, s_packed, bones, gs=8)
    cols = lambda a: a.reshape(ns, ha, HEAD_DIM).transpose(0, 2, 1)
    pos_minor = lambda c: c[0].transpose(0, 2, 3, 1)
    ob_s = _cache_attn(cols(qb_s), cols(kb_s), cols(vb_s), pos_minor(cache_win_k), pos_minor(cache_win_v))
    ob_s = ob_s.transpose(0, 2, 1).reshape(ns, aw)
    y_sample = _tail(x_sample.reshape(ns, d), og_s, z_s, ob_s, p_sample[0].reshape(ns, -1), wts,
                     tm=ns, bm=128, precise=True).reshape(ns, 1, d)

    unpack = lambda s: s.reshape(-1, HEAD_DIM, ha, HEAD_DIM).transpose(0, 2, 1, 3)[None]
    heads = lambda a: a.reshape(a.shape[0], -1, ha, HEAD_DIM)
    return (y_prompt, y_sample,
            heads(kb[:, t - keep:])[None], heads(vb[:, t - keep:])[None],
            conv_p[:, 8 - state_conv.shape[2]:][None], unpack(s_fin),
            heads(kb_s[:, None])[None], heads(vb_s[:, None])[None],
            jnp.concatenate([state_conv[0][:, 1:], ua_s[:, None]], axis=1)[None], unpack(s_new))
```

```python
import functools
import math

import jax
import jax.numpy as jnp
from jax import lax
from jax.experimental import pallas as pl
from jax.experimental.pallas import tpu as pltpu

F32 = jnp.float32
BF16 = jnp.bfloat16

HEAD_DIM = 64
GDN_CHUNK = 64
ROT_DIM = HEAD_DIM // 4
ROPE_THETA = 500000.0
PAST_LEN = 8192
DILATIONS = ((128, 1), (512, 4), (2048, 16))
BAND = 128
N_GROUPS = 4
EXPERTS_PER_GROUP = 8
N_EXPERTS = N_GROUPS * EXPERTS_PER_GROUP
NORM_EPS = 1e-6
NEG = -1e30
LANES = 128
VMEM_LIMIT = 56 * 1024 * 1024


def _cparams(sem, row_dma=False):
    return pltpu.CompilerParams(dimension_semantics=sem, vmem_limit_bytes=VMEM_LIMIT,
                                disable_bounds_checks=row_dma)


def _dot(a, b):
    return jnp.dot(a, b, preferred_element_type=F32)


def _dot_nt(a, b):
    return lax.dot_general(a, b, (((1,), (1,)), ((), ())), preferred_element_type=F32)


def _dot_tn(a, b):
    return lax.dot_general(a, b, (((0,), (0,)), ((), ())), preferred_element_type=F32)


def _split3(x):
    hi = x.astype(BF16)
    r = x - hi.astype(F32)
    mid = r.astype(BF16)
    lo = (r - mid.astype(F32)).astype(BF16)
    return hi, mid, lo


def _dot_f32_lhs(x, w_bf16):
    hi, mid, lo = _split3(x)
    return _dot(hi, w_bf16) + _dot(mid, w_bf16) + _dot(lo, w_bf16)


def _dot_f32_rhs(w_bf16, x):
    hi, mid, lo = _split3(x)
    return _dot(w_bf16, hi) + _dot(w_bf16, mid) + _dot(w_bf16, lo)


HEAD_SHIFT = HEAD_DIM.bit_length() - 1


def _lane_head(i):
    return lax.shift_right_logical(i, HEAD_SHIFT)


def _lane_in_head(i):
    return lax.bitwise_and(i, HEAD_DIM - 1)


def _sigmoid(x):
    return 1.0 / (1.0 + jnp.exp(-x))


def _silu(x):
    return x * _sigmoid(x)


def _softplus(x):
    return jnp.maximum(x, 0.0) + jnp.log1p(jnp.exp(-jnp.abs(x)))


def _rms(x, g):
    return x * lax.rsqrt(jnp.mean(x * x, axis=-1, keepdims=True) + NORM_EPS) * g


MXU_TILE = 256


def _head_sums(x2, bones):
    tile = bones[:MXU_TILE, :MXU_TILE]
    xb = x2.astype(BF16)
    return jnp.concatenate([_dot(xb[:, c:c + MXU_TILE], tile) for c in range(0, x2.shape[1], MXU_TILE)], axis=1)


def _head_rms(x, bones, g, precise=False):
    ss = _dot_f32_lhs(x * x, bones) if precise else _head_sums(x * x, bones)
    return x * lax.rsqrt(ss * (1.0 / HEAD_DIM) + NORM_EPS) * g


def _proj_kernel(*refs, tm, aw, sample):
    if sample:
        (x_ref, g_ref, w_ref, wc_ref, alog_ref, dtb_ref, esel_ref, cos_ref, s1_ref, s2_ref, bones_ref,
         wl_ref, st0_ref, st1_ref, st2_ref,
         qa_ref, ka_ref, va_ref, z_ref, la_ref, bt_ref, qb_ref, kb_ref, vb_ref, conv_ref) = refs
        buf = None
    else:
        (x_ref, g_ref, w_ref, wc_ref, alog_ref, dtb_ref, esel_ref, cos_ref, s1_ref, s2_ref, bones_ref,
         qa_ref, ka_ref, va_ref, z_ref, la_ref, bt_ref, qb_ref, kb_ref, vb_ref, conv_ref, buf) = refs
        t = pl.program_id(1)

        @pl.when(t == 0)
        def _():
            buf[0:8, :] = jnp.zeros((8, 3 * aw), F32)

    xf = _rms(x_ref[0], g_ref[...])
    xn = xf.astype(BF16)
    bones = bones_ref[...]
    if sample:
        xlo = (xf - xn.astype(F32)).astype(BF16)
        proj = lambda cs: _dot(xn, w_ref[:, cs]) + _dot(xlo, w_ref[:, cs]) + _dot(xn, wl_ref[:, cs])
        sumsq = lambda y: _dot_f32_lhs(y * y, bones)
    else:
        proj = lambda cs: _dot(xn, w_ref[:, cs])
        sumsq = lambda y: _head_sums(y * y, bones)

    outs_a = (qa_ref, ka_ref, va_ref)
    for c in range(3):
        cs = slice(c * aw, (c + 1) * aw)
        u = proj(cs)
        if sample:
            conv_ref[0, :, cs] = u
            y = (wc_ref[3:4, cs] * u + wc_ref[2:3, cs] * st2_ref[0, :, cs]
                 + wc_ref[1:2, cs] * st1_ref[0, :, cs] + wc_ref[0:1, cs] * st0_ref[0, :, cs])
        else:
            buf[8:8 + tm, cs] = u
            y = (wc_ref[3:4, cs] * u + wc_ref[2:3, cs] * buf[7:7 + tm, cs]
                 + wc_ref[1:2, cs] * buf[6:6 + tm, cs] + wc_ref[0:1, cs] * buf[5:5 + tm, cs])
        y = _silu(y)
        if c < 2:
            y = y * lax.rsqrt(sumsq(y) + NORM_EPS)
        outs_a[c][0] = y
    if not sample:
        tail = buf[tm:tm + 8, :]
        conv_ref[0] = tail
        buf[0:8, :] = tail

    z_ref[0] = proj(slice(3 * aw, 4 * aw))

    ab = proj(slice(7 * aw, 7 * aw + LANES))
    lane = lax.broadcasted_iota(jnp.int32, ab.shape, 1)
    log_a = -jnp.exp(alog_ref[...]) * _softplus(ab + dtb_ref[...])
    comb = jnp.where(lane < aw // HEAD_DIM, log_a, _sigmoid(ab))
    if sample:
        ex = _dot_f32_lhs(comb, esel_ref[...])
    else:
        c_hi = comb.astype(BF16)
        ex = _dot(c_hi, esel_ref[...]) + _dot((comb - c_hi.astype(F32)).astype(BF16), esel_ref[...])
    la_ref[0] = ex[:, :aw]
    bt_ref[0] = ex[:, aw:]

    cosv, s1v, s2v = cos_ref[...], s1_ref[...], s2_ref[...]
    for c, oref in ((0, qb_ref), (1, kb_ref)):
        for gq in range(aw // LANES):
            cs = slice(4 * aw + c * aw + gq * LANES, 4 * aw + c * aw + (gq + 1) * LANES)
            u = proj(cs)
            r = (u * cosv + pltpu.roll(u, LANES - ROT_DIM // 2, 1) * s1v
                 + pltpu.roll(u, ROT_DIM // 2, 1) * s2v)
            oref[0, :, gq * LANES:(gq + 1) * LANES] = r
    vb_ref[0] = proj(slice(6 * aw, 7 * aw))


def _proj(x3, g, w_cat, wconv, alog, dtb, esel, cos, s1, s2, bones, states, *, tm, sample):
    n, t, d = x3.shape
    aw = bones.shape[0]
    nt = t // tm
    const = lambda shape: pl.BlockSpec(shape, lambda i, j: (0,) * len(shape))
    row = lambda width: pl.BlockSpec((1, tm, width), lambda i, j: (i, j, 0))
    tab = (pl.BlockSpec((1, LANES), lambda i, j: (0, 0)) if sample
           else pl.BlockSpec((tm, LANES), lambda i, j: (j, 0)))
    in_specs = [row(d), const((1, d)), const(w_cat.shape), const(wconv.shape), const((1, LANES)),
                const((1, LANES)), const(esel.shape), tab, tab, tab, const(bones.shape)]
    args = [x3, g, w_cat, wconv, alog, dtb, esel, cos, s1, s2, bones]
    if sample:
        w_lo, states = states[0], states[1:]
        in_specs += [const(w_lo.shape)] + [row(3 * aw)] * 3
        args += [w_lo] + list(states)
        conv_shape = jax.ShapeDtypeStruct((n, t, 3 * aw), F32)
        conv_spec = row(3 * aw)
        scratch = []
    else:
        conv_shape = jax.ShapeDtypeStruct((n, 8, 3 * aw), F32)
        conv_spec = pl.BlockSpec((1, 8, 3 * aw), lambda i, j: (i, 0, 0))
        scratch = [pltpu.VMEM((tm + 8, 3 * aw), F32)]
    o = jax.ShapeDtypeStruct((n, t, aw), F32)
    return pl.pallas_call(
        functools.partial(_proj_kernel, tm=tm, aw=aw, sample=sample),
        grid=(n, nt),
        in_specs=in_specs,
        out_specs=[row(aw)] * 9 + [conv_spec],
        out_shape=[o] * 9 + [conv_shape],
        scratch_shapes=scratch,
        compiler_params=_cparams(("arbitrary", "arbitrary")),
        name="proj_sample" if sample else "proj_prompt",
    )(*args)


def _rowstack(x, hms):
    return jnp.concatenate([jnp.where(hm, x, 0.0) for hm in hms], axis=0)


def _gdn_kernel(q_ref, k_ref, v_ref, la_ref, bt_ref, ltri_ref, ones_ref, o_ref, s_out_ref,
                S, KW, NN, QP, OU, AL, *, tt, hw, unroll):
    C = GDN_CHUNK
    qw = 4 * HEAD_DIM
    nq = hw // qw
    t = pl.program_id(1)

    @pl.when(t == 0)
    def _():
        S[...] = jnp.zeros_like(S)

    rowi = lax.broadcasted_iota(jnp.int32, (C, hw), 0)
    colj = _lane_in_head(lax.broadcasted_iota(jnp.int32, (C, hw), 1))
    m_incl = colj <= rowi
    m_strict = colj < rowi
    eye = (colj == rowi).astype(F32)
    lane_q = _lane_head(lax.broadcasted_iota(jnp.int32, (1, qw), 1))
    hms = [lane_q == h for h in range(4)]
    ltri = ltri_ref[...]
    ones = ones_ref[...]

    def pm(l, r):
        return _dot(l.astype(BF16), _rowstack(r, hms).astype(BF16))

    def diag_blocks(full):
        return sum(jnp.where(hms[h], full[h * HEAD_DIM:(h + 1) * HEAD_DIM, :], 0.0) for h in range(4))

    sls = [slice(qi * qw, (qi + 1) * qw) for qi in range(nq)]

    def intra(it, carry):
        cs = [it * unroll + uu for uu in range(unroll)]
        rws = [pl.ds(pl.multiple_of(c * C, C), C) for c in cs]
        qs = [q_ref[0, r, :] * (HEAD_DIM ** -0.5) for r in rws]
        ks = [k_ref[0, r, :] for r in rws]
        bs = [bt_ref[0, r, :] for r in rws]
        gs = [_dot_f32_rhs(ltri, la_ref[0, r, :]) for r in rws]
        grs = [_dot_f32_rhs(ones, g * eye) for g in gs]
        decs = [jnp.exp(jnp.where(m_incl, g - gr, NEG)) for g, gr in zip(gs, grs)]
        egs = [jnp.exp(g) for g in gs]
        kbs = [k * b for k, b in zip(ks, bs)]
        for c, g in zip(cs, gs):
            AL[pl.ds(pl.multiple_of(c * 8, 8), 8), :] = jnp.broadcast_to(jnp.exp(g[C - 1:C, :]), (8, hw))
        ch = [(ui, sl) for ui in range(unroll) for sl in sls]
        kst = [_rowstack(ks[ui][:, sl], hms).astype(BF16) for ui, sl in ch]
        kk = [_dot_nt(kbs[ui][:, sl].astype(BF16), kst[i]) for i, (ui, sl) in enumerate(ch)]
        x = [-jnp.where(m_strict[:, sl], kk[i] * decs[ui][:, sl], 0.0) for i, (ui, sl) in enumerate(ch)]
        p = [eye[:, sl] + x[i] for i, (ui, sl) in enumerate(ch)]
        for _ in range(int(math.log2(C)) - 1):
            x = [pm(xi, xi) for xi in x]
            px = [pm(pi, xi) for pi, xi in zip(p, x)]
            p = [pi + pxi for pi, pxi in zip(p, px)]
        u = [pm(p[i], (v_ref[0, rws[ui], sl] * bs[ui][:, sl])) for i, (ui, sl) in enumerate(ch)]
        w = [pm(p[i], kbs[ui][:, sl] * egs[ui][:, sl]) for i, (ui, sl) in enumerate(ch)]
        qk = [jnp.where(m_incl[:, sl], _dot_nt(qs[ui][:, sl].astype(BF16), kst[i]) * decs[ui][:, sl], 0.0)
              .astype(BF16) for i, (ui, sl) in enumerate(ch)]
        kuw = []
        for i, (ui, sl) in enumerate(ch):
            g = gs[ui][:, sl]
            kd = (ks[ui][:, sl] * jnp.exp(g[C - 1:C, :] - g)).astype(BF16)
            uw = jnp.concatenate([u[i], w[i]], axis=1).astype(BF16)
            kuw.append(_dot_tn(kd, uw))
        ou = [_dot(qk[i], _rowstack(u[i], hms).astype(BF16)) for i in range(len(ch))]
        qw_ = [_dot(qk[i], _rowstack(w[i], hms).astype(BF16)) for i in range(len(ch))]
        for i, (ui, sl) in enumerate(ch):
            krows = pl.ds(pl.multiple_of(cs[ui] * HEAD_DIM, HEAD_DIM), HEAD_DIM)
            NN[krows, sl] = diag_blocks(kuw[i][:, :qw])
            KW[krows, sl] = diag_blocks(kuw[i][:, qw:])
            OU[rws[ui], sl] = ou[i]
            QP[rws[ui], sl] = qs[ui][:, sl] * egs[ui][:, sl] - qw_[i]
        return carry

    lax.fori_loop(0, tt // C // unroll, intra, 0)

    def inter(c, carry):
        rows = pl.ds(pl.multiple_of(c * C, C), C)
        krows = pl.ds(pl.multiple_of(c * HEAD_DIM, HEAD_DIM), HEAD_DIM)
        al = AL[pl.ds(pl.multiple_of(c * 8, 8), 1), :]
        ss = [S[:, sl] for sl in sls]
        lhs = [jnp.concatenate([KW[krows, sl], QP[rows, sl]], axis=0).astype(BF16) for sl in sls]
        rs = [_dot(l, _rowstack(s, hms).astype(BF16)) for l, s in zip(lhs, ss)]
        for sl, s, r in zip(sls, ss, rs):
            S[:, sl] = s * al[:, sl] - r[:HEAD_DIM] + NN[krows, sl]
            o_ref[0, rows, sl] = r[HEAD_DIM:] + OU[rows, sl]
        return carry

    lax.fori_loop(0, tt // C, inter, 0)
    s_out_ref[0] = S[...]


def _gdn_prompt(q, k, v, la, bt, *, tt, unroll):
    n, t, hw = q.shape
    C = GDN_CHUNK
    ltri = jnp.tril(jnp.ones((C, C), F32)).astype(BF16)
    ones = jnp.ones((C, C), BF16)
    row = pl.BlockSpec((1, tt, hw), lambda i, j: (i, j, 0))
    cst = pl.BlockSpec((C, C), lambda i, j: (0, 0))
    nch = tt // C
    return pl.pallas_call(
        functools.partial(_gdn_kernel, tt=tt, hw=hw, unroll=unroll),
        grid=(n, t // tt),
        in_specs=[row] * 5 + [cst, cst],
        out_specs=[row, pl.BlockSpec((1, HEAD_DIM, hw), lambda i, j: (i, 0, 0))],
        out_shape=[jax.ShapeDtypeStruct((n, t, hw), F32), jax.ShapeDtypeStruct((n, HEAD_DIM, hw), F32)],
        scratch_shapes=[pltpu.VMEM((HEAD_DIM, hw), F32), pltpu.VMEM((nch * HEAD_DIM, hw), F32),
                        pltpu.VMEM((nch * HEAD_DIM, hw), F32), pltpu.VMEM((tt, hw), F32),
                        pltpu.VMEM((tt, hw), F32), pltpu.VMEM((nch * 8, hw), F32)],
        compiler_params=_cparams(("arbitrary", "arbitrary")),
        name="gdn_prompt",
    )(q, k, v, la, bt, ltri, ones)


def _gdn_step_kernel(q_ref, k_ref, v_ref, la_ref, bt_ref, s_ref, bones_ref, o_ref, s_out_ref, *, gs, hw):
    rowi = lax.broadcasted_iota(jnp.int32, (HEAD_DIM, hw), 0)
    colj = _lane_in_head(lax.broadcasted_iota(jnp.int32, (HEAD_DIM, hw), 1))
    eye = (colj == rowi).astype(F32)
    bones = bones_ref[...]
    for i in range(gs):
        r = slice(i, i + 1)
        q = q_ref[r, :] * (HEAD_DIM ** -0.5)
        k = k_ref[r, :]
        v = v_ref[r, :]
        eg = jnp.exp(la_ref[r, :])
        b = bt_ref[r, :]
        s = s_ref[i]
        kbc = _dot_f32_lhs(eye * k, bones)
        qbc = _dot_f32_lhs(eye * q, bones)
        ks = jnp.sum(kbc * s, axis=0, keepdims=True)
        qs = jnp.sum(qbc * s, axis=0, keepdims=True)
        qk = jnp.sum(kbc * qbc, axis=0, keepdims=True)
        vn = b * v - b * eg * ks
        o_ref[r, :] = eg * qs + qk * vn
        s_out_ref[i] = s * eg + kbc * vn


def _gdn_sample(q, k, v, la, bt, s_packed, bones, *, gs):
    m, hw = q.shape
    row = pl.BlockSpec((gs, hw), lambda i: (i, 0))
    st = pl.BlockSpec((gs, HEAD_DIM, hw), lambda i: (i, 0, 0))
    return pl.pallas_call(
        functools.partial(_gdn_step_kernel, gs=gs, hw=hw),
        grid=(m // gs,),
        in_specs=[row] * 5 + [st, pl.BlockSpec(bones.shape, lambda i: (0, 0))],
        out_specs=[row, st],
        out_shape=[jax.ShapeDtypeStruct((m, hw), F32), jax.ShapeDtypeStruct(s_packed.shape, F32)],
        compiler_params=_cparams(("arbitrary",)),
        name="gdn_sample",
    )(q, k, v, la, bt, s_packed, bones)


def _band_kernel(q_ref, k_ref, v_ref, o_ref, kf, vf, oc, lc, *, tt, unroll):
    j = pl.program_id(2)

    @pl.when(j == 0)
    def _():
        kf[0:tt, :] = jnp.zeros((tt, LANES), F32)
        vf[0:tt, :] = jnp.zeros((tt, LANES), F32)

    kf[tt:, :] = k_ref[0]
    vf[tt:, :] = v_ref[0]
    a = lax.bitwise_and(lax.broadcasted_iota(jnp.int32, (2 * BAND, 2 * BAND), 0), BAND - 1)
    c = lax.broadcasted_iota(jnp.int32, (2 * BAND, 2 * BAND), 1)
    band = (c >= a) & (c <= a + BAND)
    bias = jnp.where(band, 0.0, NEG)
    bias_first = jnp.where(band & (c >= BAND), 0.0, NEG)
    row_h1 = lax.broadcasted_iota(jnp.int32, (2 * BAND, LANES), 0) >= BAND
    lane_h1 = lax.broadcasted_iota(jnp.int32, (2 * BAND, LANES), 1) >= HEAD_DIM
    own = row_h1 == lane_h1
    out_h1 = lax.broadcasted_iota(jnp.int32, (BAND, LANES), 1) >= HEAD_DIM
    ones_kv = jnp.ones((2 * BAND, LANES), BF16)

    for ci, (_, d) in enumerate(DILATIONS):
        nblk = tt // (BAND * d)

        def body(it, carry, d=d, ci=ci, nblk=nblk):
            rows, krows, firsts = [], [], []
            for uu in range(unroll):
                idx = it * unroll + uu
                r = idx // nblk
                b = idx % nblk
                start = b * (BAND * d) + r
                if d == 1:
                    rows.append(pl.ds(start, BAND))
                    krows.append(pl.ds(tt + start - BAND, 2 * BAND))
                else:
                    rows.append(pl.ds(start, BAND, stride=d))
                    krows.append(pl.ds(tt + start - BAND * d, 2 * BAND, stride=d))
                firsts.append(jnp.logical_and(j == 0, b == 0))
            ss = []
            for rw, kr in zip(rows, krows):
                qb = q_ref[0, rw, :] * (HEAD_DIM ** -0.5)
                qs = jnp.where(own, jnp.concatenate([qb, qb], axis=0), 0.0).astype(BF16)
                ss.append(_dot_nt(qs, kf[kr, :].astype(BF16)))
            ps, ms = [], []
            for s, first in zip(ss, firsts):
                s = s + jnp.where(first, bias_first, bias)
                m = jnp.max(s, axis=-1, keepdims=True)
                ps.append(jnp.exp((s - m).astype(BF16)))
                ms.append(m)
            pvl = [_dot(p, jnp.concatenate([vf[kr, :].astype(BF16), ones_kv], axis=1)) for p, kr in zip(ps, krows)]
            for rw, r, m in zip(rows, pvl, ms):
                l = r[:, LANES:]
                on = r[:, :LANES] / l
                lse = m + jnp.log(l)
                oc[ci, rw, :] = jnp.where(out_h1, on[BAND:], on[:BAND])
                lc[ci, rw, :] = jnp.where(out_h1, lse[BAND:], lse[:BAND])
            return carry

        lax.fori_loop(0, tt // BAND // unroll, body, 0)

    mrows = 256
    for ch in range(tt // mrows):
        rs = slice(ch * mrows, (ch + 1) * mrows)
        ls = [lc[ci, rs, :] for ci in range(len(DILATIONS))]
        mx = functools.reduce(jnp.maximum, ls)
        es = [jnp.exp(l - mx) for l in ls]
        o_ref[0, rs, :] = sum(e * oc[ci, rs, :] for ci, e in enumerate(es)) / sum(es)

    kf[0:tt, :] = kf[tt:, :]
    vf[0:tt, :] = vf[tt:, :]


def _band(q, k, v, *, tt, unroll):
    n, t, hw = q.shape
    assert all(w // d == BAND and tt % w == 0 for w, d in DILATIONS) and t % tt == 0
    blk = pl.BlockSpec((1, tt, LANES), lambda i, c, j: (i, j, c))
    nd = len(DILATIONS)
    return pl.pallas_call(
        functools.partial(_band_kernel, tt=tt, unroll=unroll),
        grid=(n, hw // LANES, t // tt),
        in_specs=[blk, blk, blk],
        out_specs=blk,
        out_shape=jax.ShapeDtypeStruct((n, t, hw), F32),
        scratch_shapes=[pltpu.VMEM((2 * tt, LANES), F32), pltpu.VMEM((2 * tt, LANES), F32),
                        pltpu.VMEM((nd, tt, LANES), F32), pltpu.VMEM((nd, tt, LANES), F32)],
        compiler_params=_cparams(("arbitrary", "arbitrary", "arbitrary")),
        name="band",
    )(q, k, v)


def _cache_attn_kernel(q_ref, kn_ref, vn_ref, k_ref, v_ref, o_ref, *, nh, n_past):
    pos = lax.broadcasted_iota(jnp.int32, (1, n_past), 1)
    cnt = jnp.zeros((1, n_past), F32)
    for window, dil in DILATIONS:
        hit = (pos >= n_past - window) & (lax.bitwise_and(pos, dil - 1) == 0)
        cnt = cnt + jnp.where(hit, 1.0, 0.0)
    live = cnt > 0.0
    nd = float(len(DILATIONS))
    lane_h = lax.broadcasted_iota(jnp.int32, (HEAD_DIM, nh), 1)
    out = jnp.zeros((HEAD_DIM, nh), F32)
    for h in range(nh):
        q = q_ref[0, :, h:h + 1] * (HEAD_DIM ** -0.5)
        s = jnp.where(live, jnp.sum(k_ref[0, h] * q, axis=0, keepdims=True), NEG)
        s0 = jnp.sum(q * kn_ref[0, :, h:h + 1], axis=0, keepdims=True)
        m = jnp.maximum(jnp.max(s, axis=-1, keepdims=True), s0)
        p = cnt * jnp.exp(s - m)
        p0 = nd * jnp.exp(s0 - m)
        den = p0 + jnp.sum(p, axis=-1, keepdims=True)
        num = p0 * vn_ref[0, :, h:h + 1] + jnp.sum(v_ref[0, h] * p, axis=-1, keepdims=True)
        out = jnp.where(lane_h == h, num / den, out)
    o_ref[0] = out


def _cache_attn(q_t, kn_t, vn_t, ck_t, cv_t):
    m, hd, nh = q_t.shape
    n_past = ck_t.shape[3]
    assert all(w // d == BAND and n_past >= w and d & (d - 1) == 0 and n_past % d == 0 for w, d in DILATIONS)
    col = pl.BlockSpec((1, hd, nh), lambda i: (i, 0, 0))
    cache = pl.BlockSpec((1, nh, hd, n_past), lambda i: (i, 0, 0, 0))
    return pl.pallas_call(
        functools.partial(_cache_attn_kernel, nh=nh, n_past=n_past),
        grid=(m,),
        in_specs=[col, col, col, cache, cache],
        out_specs=col,
        out_shape=jax.ShapeDtypeStruct((m, hd, nh), F32),
        compiler_params=_cparams(("arbitrary",)),
        name="cache_attn",
    )(q_t, kn_t, vn_t, ck_t, cv_t)


def _post_kernel(*refs, tm, precise):
    if precise:
        (x_ref, og_ref, z_ref, ob_ref, ga_ref, gb_ref, bones_ref, wo_ref, gf_ref, wrh_ref, wrl_ref, br_ref,
         wol_ref, h_ref, route_ref, rt_ref, cnt_ref, cnt) = refs
    else:
        (x_ref, og_ref, z_ref, ob_ref, ga_ref, gb_ref, bones_ref, wo_ref, gf_ref, wrh_ref, wrl_ref, br_ref,
         h_ref, route_ref, rt_ref, cnt_ref, cnt) = refs
    i = pl.program_id(0)

    @pl.when(i == 0)
    def _():
        cnt[...] = jnp.zeros_like(cnt)

    bones = bones_ref[...]
    oa = _head_rms(og_ref[...], bones, ga_ref[...], precise) * _silu(z_ref[...])
    ob = _head_rms(ob_ref[...], bones, gb_ref[...], precise)
    mixf = jnp.concatenate([oa, ob], axis=-1)
    mix = mixf.astype(BF16)
    proj = _dot(mix, wo_ref[...])
    if precise:
        proj = proj + _dot((mixf - mix.astype(F32)).astype(BF16), wo_ref[...]) + _dot(mix, wol_ref[...])
    h = x_ref[...] + proj
    h_ref[...] = h

    mrow = _rms(h, gf_ref[...])
    mh = mrow.astype(BF16)
    ml = (mrow - mh.astype(F32)).astype(BF16)
    logit = _dot(mh, wrh_ref[...]) + _dot(mh, wrl_ref[...]) + _dot(ml, wrh_ref[...]) + br_ref[...]
    lane = lax.broadcasted_iota(jnp.int32, logit.shape, 1).astype(F32)
    gl = jnp.where(lane < N_GROUPS, logit, NEG)
    gmax = jnp.max(gl, axis=-1, keepdims=True)
    grp = jnp.min(jnp.where(gl == gmax, lane, 1e9), axis=-1, keepdims=True)
    pg = 1.0 / jnp.sum(jnp.exp(gl - gmax), axis=-1, keepdims=True)
    lo = N_GROUPS + grp * EXPERTS_PER_GROUP
    el = jnp.where((lane >= lo) & (lane < lo + EXPERTS_PER_GROUP), logit, NEG)
    v1 = jnp.max(el, axis=-1, keepdims=True)
    i1 = jnp.min(jnp.where(el == v1, lane, 1e9), axis=-1, keepdims=True)
    el2 = jnp.where(lane == i1, NEG, el)
    v2 = jnp.max(el2, axis=-1, keepdims=True)
    i2 = jnp.min(jnp.where(el2 == v2, lane, 1e9), axis=-1, keepdims=True)
    e = jnp.exp(v2 - v1)
    g1 = pg / (1.0 + e)
    g2 = pg * e / (1.0 + e)
    e1 = i1 - N_GROUPS
    e2 = i2 - N_GROUPS

    oh1 = lane == e1
    oh2 = lane == e2
    onehot = jnp.where(oh1 | oh2, 1.0, 0.0)
    ri = lax.broadcasted_iota(jnp.int32, (tm, tm), 0)
    ci = lax.broadcasted_iota(jnp.int32, (tm, tm), 1)
    tri = jnp.where(ci < ri, 1.0, 0.0).astype(BF16)
    before = _dot(tri, onehot.astype(BF16)) + cnt[...]
    r1 = jnp.sum(jnp.where(oh1, before, 0.0), axis=-1, keepdims=True)
    r2 = jnp.sum(jnp.where(oh2, before, 0.0), axis=-1, keepdims=True)
    cnt[...] = cnt[...] + jnp.sum(onehot, axis=0, keepdims=True)
    cnt_ref[...] = cnt[...]
    route = jnp.zeros_like(logit)
    for j, val in enumerate((e1, e2, r1, r2, g1, g2)):
        route = jnp.where(lane == j, val, route)
    route_ref[...] = route
    rt_ref[...] = jnp.transpose(route)[:ROW_TILE]


def _post(x2, og, z, ob, ga, gb, bones, wo, gf, wrh, wrl, br, wo_lo, *, tm):
    m, d = x2.shape
    hw = og.shape[1]
    row = lambda w: pl.BlockSpec((tm, w), lambda i: (i, 0))
    const = lambda a: pl.BlockSpec(a.shape, lambda i: (0,) * a.ndim)
    consts = [ga, gb, bones, wo, gf, wrh, wrl, br] + ([] if wo_lo is None else [wo_lo])
    return pl.pallas_call(
        functools.partial(_post_kernel, tm=tm, precise=wo_lo is not None),
        grid=(m // tm,),
        in_specs=[row(d), row(hw), row(hw), row(hw)] + [const(a) for a in consts],
        out_specs=[row(d), row(LANES), pl.BlockSpec((ROW_TILE, tm), lambda i: (0, i)),
                   pl.BlockSpec((1, LANES), lambda i: (0, 0))],
        out_shape=[jax.ShapeDtypeStruct((m, d), F32), jax.ShapeDtypeStruct((m, LANES), F32),
                   jax.ShapeDtypeStruct((ROW_TILE, m), F32), jax.ShapeDtypeStruct((1, LANES), F32)],
        scratch_shapes=[pltpu.VMEM((1, LANES), F32)],
        compiler_params=_cparams(("arbitrary",)),
        name=f"post_{m}",
    )(x2, og, z, ob, *consts)


ROW_TILE = 8


def _rows_from_tiles(ref, n):
    return jnp.concatenate([ref[pl.ds(c, n, stride=ROW_TILE), :] for c in range(ROW_TILE)], axis=1)


def _rows_to_tiles(ref, rows):
    n = rows.shape[0]
    for c in range(ROW_TILE):
        ref[pl.ds(c, n, stride=ROW_TILE), :] = rows[:, c * LANES:(c + 1) * LANES]


def _dispatch_kernel(pstart_ref, pend_ref, h_ref, gf_ref, dest_ref, xbuf_ref, mrow, zbuf, dsm, sem, dsem, zsem,
                     *, tm, bm, nsteps, n_blk):
    i = pl.program_id(0)
    slot = i % 2
    dcp = pltpu.make_async_copy(dest_ref.at[i], dsm, dsem)
    dcp.start()

    def zero_copy(e):
        start = pl.multiple_of((pend_ref[e] - bm) * ROW_TILE, bm * ROW_TILE)
        return pltpu.make_async_copy(zbuf, xbuf_ref.at[pl.ds(start, bm * ROW_TILE)], zsem)

    def zero_block(b):
        start = pl.multiple_of(b * (bm * ROW_TILE), bm * ROW_TILE)
        return pltpu.make_async_copy(zbuf, xbuf_ref.at[pl.ds(start, bm * ROW_TILE)], zsem)

    @pl.when(i == 0)
    def _():
        zbuf[...] = jnp.zeros_like(zbuf)
        n_used = pend_ref[N_EXPERTS - 1] // bm
        for start_or_wait in (True, False):
            for e in range(N_EXPERTS):
                @pl.when(pend_ref[e] > pstart_ref[e])
                def _(e=e, start_or_wait=start_or_wait):
                    zero_copy(e).start() if start_or_wait else zero_copy(e).wait()

            def tail(b, carry, start_or_wait=start_or_wait):
                zero_block(b).start() if start_or_wait else zero_block(b).wait()
                return carry

            lax.fori_loop(n_used, n_blk, tail, 0)

    def wait_rows(s):
        for _ in range(2):
            pltpu.make_async_copy(mrow.at[s], xbuf_ref.at[pl.ds(0, tm * ROW_TILE)], sem.at[s]).wait()

    @pl.when(i >= 2)
    def _():
        wait_rows(slot)

    m = _rms(h_ref[...], gf_ref[...])
    dcp.wait()

    for s in range(2):
        @pl.when(slot == s)
        def _(s=s):
            _rows_to_tiles(mrow.at[s], m)

            def body(r, carry):
                src = mrow.at[s, pl.ds(pl.multiple_of(r * ROW_TILE, ROW_TILE), ROW_TILE)]
                for kk in range(2):
                    d = pl.multiple_of(dsm[kk * tm + r] * ROW_TILE, ROW_TILE)
                    pltpu.make_async_copy(src, xbuf_ref.at[pl.ds(d, ROW_TILE)], sem.at[s]).start()
                return carry

            lax.fori_loop(0, tm, body, 0, unroll=8)

    @pl.when(i == nsteps - 1)
    def _():
        wait_rows(slot)
        if nsteps >= 2:
            wait_rows(1 - slot)


def _dispatch(h, gf, dest, pstart, pend, n_slots, *, tm, bm):
    m, d = h.shape
    assert d == ROW_TILE * LANES
    nsteps = m // tm
    grid_spec = pltpu.PrefetchScalarGridSpec(
        num_scalar_prefetch=2,
        grid=(nsteps,),
        in_specs=[pl.BlockSpec((tm, d), lambda i, ps, pe: (i, 0)), pl.BlockSpec((1, d), lambda i, ps, pe: (0, 0)),
                  pl.BlockSpec(memory_space=pl.ANY)],
        out_specs=pl.BlockSpec(memory_space=pl.ANY),
        scratch_shapes=[pltpu.VMEM((2, tm * ROW_TILE, LANES), F32), pltpu.VMEM((bm * ROW_TILE, LANES), F32),
                        pltpu.SMEM((2 * tm,), jnp.int32),
                        pltpu.SemaphoreType.DMA((2,)), pltpu.SemaphoreType.DMA, pltpu.SemaphoreType.DMA],
    )
    return pl.pallas_call(
        functools.partial(_dispatch_kernel, tm=tm, bm=bm, nsteps=nsteps, n_blk=n_slots // bm),
        grid_spec=grid_spec,
        out_shape=jax.ShapeDtypeStruct((n_slots * ROW_TILE, LANES), F32),
        compiler_params=_cparams(("arbitrary",), row_dma=True),
        name=f"dispatch_{m}",
    )(pstart, pend, h, gf, dest)


def _expert_kernel(be_ref, nu_ref, x_ref, wg_ref, wu_ref, wd_ref, y_ref, *, bm):
    b = pl.program_id(0)

    @pl.when(b < nu_ref[0])
    def _():
        x = _rows_from_tiles(x_ref, bm).astype(BF16)
        hid = _silu(_dot(x, wg_ref[0])) * _dot(x, wu_ref[0])
        _rows_to_tiles(y_ref, _dot(hid.astype(BF16), wd_ref[0]))

    @pl.when(b >= nu_ref[0])
    def _():
        y_ref[...] = jnp.zeros_like(y_ref)


def _experts(xbuf, blk_expert, n_used, wg, wu, wd, *, bm):
    d, de = wg.shape[1], wg.shape[2]
    n_slots = xbuf.shape[0] // ROW_TILE
    blk = (bm * ROW_TILE, LANES)
    grid_spec = pltpu.PrefetchScalarGridSpec(
        num_scalar_prefetch=2,
        grid=(n_slots // bm,),
        in_specs=[pl.BlockSpec(blk, lambda b, be, nu: (jnp.minimum(b, nu[0] - 1), 0)),
                  pl.BlockSpec((1, d, de), lambda b, be, nu: (be[b], 0, 0)),
                  pl.BlockSpec((1, d, de), lambda b, be, nu: (be[b], 0, 0)),
                  pl.BlockSpec((1, de, d), lambda b, be, nu: (be[b], 0, 0))],
        out_specs=pl.BlockSpec(blk, lambda b, be, nu: (b, 0)),
    )
    return pl.pallas_call(
        functools.partial(_expert_kernel, bm=bm),
        grid_spec=grid_spec,
        out_shape=jax.ShapeDtypeStruct(xbuf.shape, F32),
        compiler_params=_cparams(("arbitrary",)),
        name=f"experts_{n_slots}",
    )(blk_expert, n_used, xbuf, wg, wu, wd)


def _combine_kernel(h_ref, route_ref, p_ref, wpg_ref, wpp_ref, gfin_ref, dest_ref, ybuf_ref, y_ref,
                    ys, dsm0, dsm1, sem, dsem, *, tm, nsteps):
    i = pl.program_id(0)
    slot = i % 2

    def issue(j, s):
        dsm = (dsm0, dsm1)[s]
        cp = pltpu.make_async_copy(dest_ref.at[j], dsm, dsem)
        cp.start()
        cp.wait()

        def body(r, carry):
            for kk in range(2):
                d = pl.multiple_of(dsm[kk * tm + r] * ROW_TILE, ROW_TILE)
                dst = ys.at[s, kk, pl.ds(pl.multiple_of(r * ROW_TILE, ROW_TILE), ROW_TILE)]
                pltpu.make_async_copy(ybuf_ref.at[pl.ds(d, ROW_TILE)], dst, sem.at[s]).start()
            return carry

        lax.fori_loop(0, tm, body, 0, unroll=8)

    @pl.when(i == 0)
    def _():
        issue(0, 0)

    for s in range(2):
        @pl.when(jnp.logical_and(i + 1 < nsteps, slot == 1 - s))
        def _(s=s):
            issue(i + 1, s)

    for kk in range(2):
        pltpu.make_async_copy(ybuf_ref.at[pl.ds(0, tm * ROW_TILE)], ys.at[slot, kk], sem.at[slot]).wait()

    route = route_ref[...]
    g1 = route[:, 4:5]
    g2 = route[:, 5:6]
    h = h_ref[...] + (g1 * _rows_from_tiles(ys.at[slot, 0], tm) + g2 * _rows_from_tiles(ys.at[slot, 1], tm))
    gate = _sigmoid(_dot(h.astype(BF16), wpg_ref[...]))
    out = h + gate * _dot(p_ref[...].astype(BF16), wpp_ref[...])
    y_ref[...] = _rms(out, gfin_ref[...])


def _combine(h, route, p, wpg, wpp, gfin, dest, ybuf, *, tm):
    m, d = h.shape
    row = lambda w: pl.BlockSpec((tm, w), lambda i: (i, 0))
    const = lambda a: pl.BlockSpec(a.shape, lambda i: (0,) * a.ndim)
    return pl.pallas_call(
        functools.partial(_combine_kernel, tm=tm, nsteps=m // tm),
        grid=(m // tm,),
        in_specs=[row(d), row(LANES), row(p.shape[1]), const(wpg), const(wpp), const(gfin),
                  pl.BlockSpec(memory_space=pl.ANY), pl.BlockSpec(memory_space=pl.ANY)],
        out_specs=row(d),
        out_shape=jax.ShapeDtypeStruct((m, d), F32),
        scratch_shapes=[pltpu.VMEM((2, 2, tm * ROW_TILE, LANES), F32), pltpu.SMEM((2 * tm,), jnp.int32),
                        pltpu.SMEM((2 * tm,), jnp.int32), pltpu.SemaphoreType.DMA((2,)), pltpu.SemaphoreType.DMA],
        compiler_params=_cparams(("arbitrary",), row_dma=True),
        name=f"combine_{m}",
    )(h, route, p, wpg, wpp, gfin, dest, ybuf)


def _tail(x2, og, z, ob, p2, wts, *, tm, bm, precise):
    m, d = x2.shape
    h, route, route_t, counts = _post(x2, og, z, ob, wts["ga"], wts["gb"], wts["bones"],
                                      wts["wo_hi"] if precise else wts["wo"], wts["gf"], wts["wrh"], wts["wrl"],
                                      wts["br"], wts["wo_lo"] if precise else None, tm=tm)
    eid = route_t[0:2].astype(jnp.int32)
    rank = route_t[2:4].astype(jnp.int32)
    sizes = counts[0, :N_EXPERTS].astype(jnp.int32)
    padded = (sizes + bm - 1) // bm * bm
    pend = jnp.cumsum(padded)
    pstart = pend - padded
    dest = (pstart[eid] + rank).reshape(2, m // tm, tm).transpose(1, 0, 2).reshape(m // tm, 2 * tm)
    n_blk = (2 * m) // bm + N_EXPERTS
    blk_start = jnp.arange(n_blk, dtype=jnp.int32) * bm
    blk_expert = jnp.minimum(jnp.sum(pend[None, :] <= blk_start[:, None], axis=1), N_EXPERTS - 1).astype(jnp.int32)
    n_used = (pend[-1:] // bm).astype(jnp.int32)
    xbuf = _dispatch(h, wts["gf"], dest, pstart.astype(jnp.int32), pend.astype(jnp.int32), n_blk * bm,
                     tm=tm, bm=bm)
    ybuf = _experts(xbuf, blk_expert, n_used, wts["wg"], wts["wu"], wts["wd"], bm=bm)
    return _combine(h, route, p2, wts["wpg"], wts["wpp"], wts["gfin"], dest, ybuf, tm=tm)


def _split_hi_lo(w):
    bits = lax.bitcast_convert_type(w.astype(F32), jnp.uint32)
    hi32 = lax.bitcast_convert_type(bits & jnp.uint32(0xFFFF0000), F32)
    return hi32.astype(BF16), (w - hi32).astype(BF16)


def _rope_tables(pos):
    half = ROT_DIM // 2
    inv = ROPE_THETA ** (-jnp.arange(half, dtype=F32) * (2.0 / ROT_DIM))
    ang = pos.astype(F32)[:, None] * inv[None, :]
    cos, sin = jnp.cos(ang), jnp.sin(ang)
    j = jnp.arange(LANES) % HEAD_DIM
    first, second = j < half, (j >= half) & (j < ROT_DIM)
    jj = jnp.where(second, j - half, jnp.where(first, j, 0))
    c = jnp.where((first | second)[None, :], cos[:, jj], 1.0)
    s1 = jnp.where(first[None, :], -sin[:, jj], 0.0)
    s2 = jnp.where(second[None, :], sin[:, jj], 0.0)
    return c, s1, s2


def kernel(x_prompt, x_sample, cache_win_k, cache_win_v, state_conv, state_delta, p_prompt, p_sample, g_attn_norm, w_in, w_conv, a_log, dt_bias, g_a_out, g_b_out, w_out, g_ffn_norm, w_router_group, b_router_group, w_router_expert, b_router_expert, w_exp_gate, w_exp_up, w_exp_down, w_ple_gate, w_ple_proj, g_final):
    n, t, d = x_prompt.shape
    ns = x_sample.shape[0]
    assert w_in.shape[0] == 1 and x_sample.shape[1] == 1
    ha = a_log.shape[1]
    aw = ha * HEAD_DIM
    off_a = 4 * aw
    off_win = off_a + 2 * ha
    keep =min(DILATIONS[-1][0], t)
    hi = lambda a: a.astype(F32)

    w = w_in[0]
    w_ab = jnp.pad(w[:, off_a:off_win], ((0, 0), (0, LANES - 2 * ha)))
    w_cat32 = jnp.concatenate([w[:, :off_a], w[:, off_win:], w_ab], axis=1)
    w_cat = w_cat32.astype(BF16)
    w_cat_hi, w_cat_lo = _split_hi_lo(w_cat32)
    pad_l = lambda v: jnp.pad(hi(v), (0, LANES - v.shape[0]))[None, :]
    alog = pad_l(a_log[0])
    dtb = pad_l(dt_bias[0])
    lane_head = jnp.arange(2 * aw) // HEAD_DIM
    esel = (jnp.arange(LANES)[:, None] == lane_head[None, :]).astype(BF16)
    hd = jnp.arange(aw) // HEAD_DIM
    bones = (hd[:, None] == hd[None, :]).astype(BF16)
    wr = jnp.concatenate([hi(w_router_group[0]), hi(w_router_expert[0]).reshape(d, N_EXPERTS)], axis=1)
    wr = jnp.pad(wr, ((0, 0), (0, LANES - wr.shape[1])))
    wrh, wrl = _split_hi_lo(wr)
    wo_hi, wo_lo = _split_hi_lo(w_out[0])
    br = jnp.concatenate([hi(b_router_group[0]), hi(b_router_expert[0]).reshape(N_EXPERTS)])
    wts = dict(
        ga=jnp.tile(hi(g_a_out[0]), ha)[None, :], gb=jnp.tile(hi(g_b_out[0]), ha)[None, :], bones=bones,
        wo=w_out[0].astype(BF16), wo_hi=wo_hi, wo_lo=wo_lo,
        gf=hi(g_ffn_norm[0])[None, :], wrh=wrh, wrl=wrl, br=pad_l(br),
        wg=w_exp_gate[0].astype(BF16), wu=w_exp_up[0].astype(BF16), wd=w_exp_down[0].astype(BF16),
        wpg=w_ple_gate[0].astype(BF16), wpp=w_ple_proj[0].astype(BF16), gfin=hi(g_final)[None, :])
    g_attn = hi(g_attn_norm[0])[None, :]
    wconv = hi(w_conv[0])

    tm_p = min(256, t)
    cos, s1, s2 = _rope_tables(jnp.arange(t, dtype=jnp.int32))
    qa, ka, va, z, la, bt, qb, kb, vb, conv_p = _proj(
        x_prompt, g_attn, w_cat, wconv, alog, dtb, esel, cos, s1, s2, bones, None, tm=tm_p, sample=False)
    og, s_fin = _gdn_prompt(qa, ka, va, la, bt, tt=min(512, t), unroll=4)
    ob = _band(qb, kb, vb, tt=DILATIONS[-1][0], unroll=8)
    flat = lambda a: a.reshape(n * t, a.shape[-1])
    y_prompt = _tail(flat(x_prompt), flat(og), flat(z), flat(ob), flat(p_prompt[0]), wts,
                     tm=min(512, t), bm=256, precise=False).reshape(n, t, d)

    cos, s1, s2 = _rope_tables(jnp.full((1,), PAST_LEN, jnp.int32))
    st = [state_conv[0][None, :, j, :] for j in range(state_conv.shape[2])]
    xs3 = x_sample.reshape(1, ns, d)
    qa_s, ka_s, va_s, z_s, la_s, bt_s, qb_s, kb_s, vb_s, ua_s = [
        a[0] for a in _proj(xs3, g_attn, w_cat_hi, wconv, alog, dtb, esel, cos, s1, s2, bones, [w_cat_lo] + st,
                            tm=ns, sample=True)]
    s_packed = state_delta[0].transpose(0, 2, 1, 3).reshape(ns, HEAD_DIM, aw)
    og_s, s_new = _gdn_sample(qa_s, ka_s, va_s, la_s, bt_s, s_packed, bones, gs=8)
    cols = lambda a: a.reshape(ns, ha, HEAD_DIM).transpose(0, 2, 1)
    pos_minor = lambda c: c[0].transpose(0, 2, 3, 1)
    ob_s = _cache_attn(cols(qb_s), cols(kb_s), cols(vb_s), pos_minor(cache_win_k), pos_minor(cache_win_v))
    ob_s = ob_s.transpose(0, 2, 1).reshape(ns, aw)
    y_sample = _tail(x_sample.reshape(ns, d), og_s, z_s, ob_s, p_sample[0].reshape(ns, -1), wts,
                     tm=ns, bm=128, precise=True).reshape(ns, 1, d)

    unpack = lambda s: s.reshape(-1, HEAD_DIM, ha, HEAD_DIM).transpose(0, 2, 1, 3)[None]
    heads = lambda a: a.reshape(a.shape[0], -1, ha, HEAD_DIM)
    return (y_prompt, y_sample,
            heads(kb[:, t - keep:])[None], heads(vb[:, t - keep:])[None],
            conv_p[:, 8 - state_conv.shape[2]:][None], unpack(s_fin),
            heads(kb_s[:, None])[None], heads(vb_s[:, None])[None],
            jnp.concatenate([state_conv[0][:, 1:], ua_s[:, None]], axis=1)[None], unpack(s_new))
```

```python
import functools
import math

import jax
import jax.numpy as jnp
from jax import lax
from jax.experimental import pallas as pl
from jax.experimental.pallas import tpu as pltpu

F32 = jnp.float32
BF16 = jnp.bfloat16

HEAD_DIM = 64
GDN_CHUNK = 64
ROT_DIM = HEAD_DIM // 4
ROPE_THETA = 500000.0
PAST_LEN = 8192
DILATIONS = ((128, 1), (512, 4), (2048, 16))
BAND = 128
N_GROUPS = 4
EXPERTS_PER_GROUP = 8
N_EXPERTS = N_GROUPS * EXPERTS_PER_GROUP
NORM_EPS = 1e-6
NEG = -1e30
LANES = 128
VMEM_LIMIT = 56 * 1024 * 1024


def _cparams(sem, row_dma=False):
    return pltpu.CompilerParams(dimension_semantics=sem, vmem_limit_bytes=VMEM_LIMIT,
                                disable_bounds_checks=row_dma)


def _dot(a, b):
    return jnp.dot(a, b, preferred_element_type=F32)


def _dot_nt(a, b):
    return lax.dot_general(a, b, (((1,), (1,)), ((), ())), preferred_element_type=F32)


def _dot_tn(a, b):
    return lax.dot_general(a, b, (((0,), (0,)), ((), ())), preferred_element_type=F32)


def _split3(x):
    hi = x.astype(BF16)
    r = x - hi.astype(F32)
    mid = r.astype(BF16)
    lo = (r - mid.astype(F32)).astype(BF16)
    return hi, mid, lo


def _dot_f32_lhs(x, w_bf16):
    hi, mid, lo = _split3(x)
    return _dot(hi, w_bf16) + _dot(mid, w_bf16) + _dot(lo, w_bf16)


def _dot_f32_rhs(w_bf16, x):
    hi, mid, lo = _split3(x)
    return _dot(w_bf16, hi) + _dot(w_bf16, mid) + _dot(w_bf16, lo)


HEAD_SHIFT = HEAD_DIM.bit_length() - 1


def _lane_head(i):
    return lax.shift_right_logical(i, HEAD_SHIFT)


def _lane_in_head(i):
    return lax.bitwise_and(i, HEAD_DIM - 1)


def _sigmoid(x):
    return 1.0 / (1.0 + jnp.exp(-x))


def _silu(x):
    return x * _sigmoid(x)


def _softplus(x):
    return jnp.maximum(x, 0.0) + jnp.log1p(jnp.exp(-jnp.abs(x)))


def _rms(x, g):
    return x * lax.rsqrt(jnp.mean(x * x, axis=-1, keepdims=True) + NORM_EPS) * g


MXU_TILE = 256


def _head_sums(x2, bones):
    tile = bones[:MXU_TILE, :MXU_TILE]
    xb = x2.astype(BF16)
    return jnp.concatenate([_dot(xb[:, c:c + MXU_TILE], tile) for c in range(0, x2.shape[1], MXU_TILE)], axis=1)


def _head_rms(x, bones, g, precise=False):
    ss = _dot_f32_lhs(x * x, bones) if precise else _head_sums(x * x, bones)
    return x * lax.rsqrt(ss * (1.0 / HEAD_DIM) + NORM_EPS) * g


def _proj_kernel(*refs, tm, aw, sample):
    if sample:
        (x_ref, g_ref, w_ref, wc_ref, alog_ref, dtb_ref, esel_ref, cos_ref, s1_ref, s2_ref, bones_ref,
         wl_ref, st0_ref, st1_ref, st2_ref,
         qa_ref, ka_ref, va_ref, z_ref, la_ref, bt_ref, qb_ref, kb_ref, vb_ref, conv_ref) = refs
        buf = None
    else:
        (x_ref, g_ref, w_ref, wc_ref, alog_ref, dtb_ref, esel_ref, cos_ref, s1_ref, s2_ref, bones_ref,
         qa_ref, ka_ref, va_ref, z_ref, la_ref, bt_ref, qb_ref, kb_ref, vb_ref, conv_ref, buf) = refs
        t = pl.program_id(1)

        @pl.when(t == 0)
        def _():
            buf[0:8, :] = jnp.zeros((8, 3 * aw), F32)

    xf = _rms(x_ref[0], g_ref[...])
    xn = xf.astype(BF16)
    bones = bones_ref[...]
    if sample:
        xlo = (xf - xn.astype(F32)).astype(BF16)
        proj = lambda cs: _dot(xn, w_ref[:, cs]) + _dot(xlo, w_ref[:, cs]) + _dot(xn, wl_ref[:, cs])
        sumsq = lambda y: _dot_f32_lhs(y * y, bones)
    else:
        proj = lambda cs: _dot(xn, w_ref[:, cs])
        sumsq = lambda y: _head_sums(y * y, bones)

    outs_a = (qa_ref, ka_ref, va_ref)
    for c in range(3):
        cs = slice(c * aw, (c + 1) * aw)
        u = proj(cs)
        if sample:
            conv_ref[0, :, cs] = u
            y = (wc_ref[3:4, cs] * u + wc_ref[2:3, cs] * st2_ref[0, :, cs]
                 + wc_ref[1:2, cs] * st1_ref[0, :, cs] + wc_ref[0:1, cs] * st0_ref[0, :, cs])
        else:
            buf[8:8 + tm, cs] = u
            y = (wc_ref[3:4, cs] * u + wc_ref[2:3, cs] * buf[7:7 + tm, cs]
                 + wc_ref[1:2, cs] * buf[6:6 + tm, cs] + wc_ref[0:1, cs] * buf[5:5 + tm, cs])
        y = _silu(y)
        if c < 2:
            y = y * lax.rsqrt(sumsq(y) + NORM_EPS)
        outs_a[c][0] = y
    if not sample:
        tail = buf[tm:tm + 8, :]
        conv_ref[0] = tail
        buf[0:8, :] = tail

    z_ref[0] = proj(slice(3 * aw, 4 * aw))

    ab = proj(slice(7 * aw, 7 * aw + LANES))
    lane = lax.broadcasted_iota(jnp.int32, ab.shape, 1)
    log_a = -jnp.exp(alog_ref[...]) * _softplus(ab + dtb_ref[...])
    comb = jnp.where(lane < aw // HEAD_DIM, log_a, _sigmoid(ab))
    if sample:
        ex = _dot_f32_lhs(comb, esel_ref[...])
    else:
        c_hi = comb.astype(BF16)
        ex = _dot(c_hi, esel_ref[...]) + _dot((comb - c_hi.astype(F32)).astype(BF16), esel_ref[...])
    la_ref[0] = ex[:, :aw]
    bt_ref[0] = ex[:, aw:]

    cosv, s1v, s2v = cos_ref[...], s1_ref[...], s2_ref[...]
    for c, oref in ((0, qb_ref), (1, kb_ref)):
        for gq in range(aw // LANES):
            cs = slice(4 * aw + c * aw + gq * LANES, 4 * aw + c * aw + (gq + 1) * LANES)
            u = proj(cs)
            r = (u * cosv + pltpu.roll(u, LANES - ROT_DIM // 2, 1) * s1v
                 + pltpu.roll(u, ROT_DIM // 2, 1) * s2v)
            oref[0, :, gq * LANES:(gq + 1) * LANES] = r
    vb_ref[0] = proj(slice(6 * aw, 7 * aw))


def _proj(x3, g, w_cat, wconv, alog, dtb, esel, cos, s1, s2, bones, states, *, tm, sample):
    n, t, d = x3.shape
    aw = bones.shape[0]
    nt = t // tm
    const = lambda shape: pl.BlockSpec(shape, lambda i, j: (0,) * len(shape))
    row = lambda width: pl.BlockSpec((1, tm, width), lambda i, j: (i, j, 0))
    tab = (pl.BlockSpec((1, LANES), lambda i, j: (0, 0)) if sample
           else pl.BlockSpec((tm, LANES), lambda i, j: (j, 0)))
    in_specs = [row(d), const((1, d)), const(w_cat.shape), const(wconv.shape), const((1, LANES)),
                const((1, LANES)), const(esel.shape), tab, tab, tab, const(bones.shape)]
    args = [x3, g, w_cat, wconv, alog, dtb, esel, cos, s1, s2, bones]
    if sample:
        w_lo, states = states[0], states[1:]
        in_specs += [const(w_lo.shape)] + [row(3 * aw)] * 3
        args += [w_lo] + list(states)
        conv_shape = jax.ShapeDtypeStruct((n, t, 3 * aw), F32)
        conv_spec = row(3 * aw)
        scratch = []
    else:
        conv_shape = jax.ShapeDtypeStruct((n, 8, 3 * aw), F32)
        conv_spec = pl.BlockSpec((1, 8, 3 * aw), lambda i, j: (i, 0, 0))
        scratch = [pltpu.VMEM((tm + 8, 3 * aw), F32)]
    o = jax.ShapeDtypeStruct((n, t, aw), F32)
    return pl.pallas_call(
        functools.partial(_proj_kernel, tm=tm, aw=aw, sample=sample),
        grid=(n, nt),
        in_specs=in_specs,
        out_specs=[row(aw)] * 9 + [conv_spec],
        out_shape=[o] * 9 + [conv_shape],
        scratch_shapes=scratch,
        compiler_params=_cparams(("arbitrary", "arbitrary")),
        name="proj_sample" if sample else "proj_prompt",
    )(*args)


def _rowstack(x, hms):
    return jnp.concatenate([jnp.where(hm, x, 0.0) for hm in hms], axis=0)


def _gdn_kernel(q_ref, k_ref, v_ref, la_ref, bt_ref, ltri_ref, ones_ref, o_ref, s_out_ref,
                S, KW, NN, QP, OU, AL, *, tt, hw, unroll):
    C = GDN_CHUNK
    qw = 4 * HEAD_DIM
    nq = hw // qw
    t = pl.program_id(1)

    @pl.when(t == 0)
    def _():
        S[...] = jnp.zeros_like(S)

    rowi = lax.broadcasted_iota(jnp.int32, (C, hw), 0)
    colj = _lane_in_head(lax.broadcasted_iota(jnp.int32, (C, hw), 1))
    m_incl = colj <= rowi
    m_strict = colj < rowi
    eye = (colj == rowi).astype(F32)
    lane_q = _lane_head(lax.broadcasted_iota(jnp.int32, (1, qw), 1))
    hms = [lane_q == h for h in range(4)]
    ltri = ltri_ref[...]
    ones = ones_ref[...]

    def pm(l, r):
        return _dot(l.astype(BF16), _rowstack(r, hms).astype(BF16))

    def diag_blocks(full):
        return sum(jnp.where(hms[h], full[h * HEAD_DIM:(h + 1) * HEAD_DIM, :], 0.0) for h in range(4))

    sls = [slice(qi * qw, (qi + 1) * qw) for qi in range(nq)]

    def intra(it, carry):
        cs = [it * unroll + uu for uu in range(unroll)]
        rws = [pl.ds(pl.multiple_of(c * C, C), C) for c in cs]
        qs = [q_ref[0, r, :] * (HEAD_DIM ** -0.5) for r in rws]
        ks = [k_ref[0, r, :] for r in rws]
        bs = [bt_ref[0, r, :] for r in rws]
        gs = [_dot_f32_rhs(ltri, la_ref[0, r, :]) for r in rws]
        grs = [_dot_f32_rhs(ones, g * eye) for g in gs]
        decs = [jnp.exp(jnp.where(m_incl, g - gr, NEG)) for g, gr in zip(gs, grs)]
        egs = [jnp.exp(g) for g in gs]
        kbs = [k * b for k, b in zip(ks, bs)]
        for c, g in zip(cs, gs):
            AL[pl.ds(pl.multiple_of(c * 8, 8), 8), :] = jnp.broadcast_to(jnp.exp(g[C - 1:C, :]), (8, hw))
        ch = [(ui, sl) for ui in range(unroll) for sl in sls]
        kst = [_rowstack(ks[ui][:, sl], hms).astype(BF16) for ui, sl in ch]
        kk = [_dot_nt(kbs[ui][:, sl].astype(BF16), kst[i]) for i, (ui, sl) in enumerate(ch)]
        x = [-jnp.where(m_strict[:, sl], kk[i] * decs[ui][:, sl], 0.0) for i, (ui, sl) in enumerate(ch)]
        p = [eye[:, sl] + x[i] for i, (ui, sl) in enumerate(ch)]
        for _ in range(int(math.log2(C)) - 1):
            x = [pm(xi, xi) for xi in x]
            px = [pm(pi, xi) for pi, xi in zip(p, x)]
            p = [pi + pxi for pi, pxi in zip(p, px)]
        u = [pm(p[i], (v_ref[0, rws[ui], sl] * bs[ui][:, sl])) for i, (ui, sl) in enumerate(ch)]
        w = [pm(p[i], kbs[ui][:, sl] * egs[ui][:, sl]) for i, (ui, sl) in enumerate(ch)]
        qk = [jnp.where(m_incl[:, sl], _dot_nt(qs[ui][:, sl].astype(BF16), kst[i]) * decs[ui][:, sl], 0.0)
              .astype(BF16) for i, (ui, sl) in enumerate(ch)]
        kuw = []
        for i, (ui, sl) in enumerate(ch):
            g = gs[ui][:, sl]
            kd = (ks[ui][:, sl] * jnp.exp(g[C - 1:C, :] - g)).astype(BF16)
            uw = jnp.concatenate([u[i], w[i]], axis=1).astype(BF16)
            kuw.append(_dot_tn(kd, uw))
        ou = [_dot(qk[i], _rowstack(u[i], hms).astype(BF16)) for i in range(len(ch))]
        qw_ = [_dot(qk[i], _rowstack(w[i], hms).astype(BF16)) for i in range(len(ch))]
        for i, (ui, sl) in enumerate(ch):
            krows = pl.ds(pl.multiple_of(cs[ui] * HEAD_DIM, HEAD_DIM), HEAD_DIM)
            NN[krows, sl] = diag_blocks(kuw[i][:, :qw])
            KW[krows, sl] = diag_blocks(kuw[i][:, qw:])
            OU[rws[ui], sl] = ou[i]
            QP[rws[ui], sl] = qs[ui][:, sl] * egs[ui][:, sl] - qw_[i]
        return carry

    lax.fori_loop(0, tt // C // unroll, intra, 0)

    def inter(c, carry):
        rows = pl.ds(pl.multiple_of(c * C, C), C)
        krows = pl.ds(pl.multiple_of(c * HEAD_DIM, HEAD_DIM), HEAD_DIM)
        al = AL[pl.ds(pl.multiple_of(c * 8, 8), 1), :]
        ss = [S[:, sl] for sl in sls]
        lhs = [jnp.concatenate([KW[krows, sl], QP[rows, sl]], axis=0).astype(BF16) for sl in sls]
        rs = [_dot(l, _rowstack(s, hms).astype(BF16)) for l, s in zip(lhs, ss)]
        for sl, s, r in zip(sls, ss, rs):
            S[:, sl] = s * al[:, sl] - r[:HEAD_DIM] + NN[krows, sl]
            o_ref[0, rows, sl] = r[HEAD_DIM:] + OU[rows, sl]
        return carry

    lax.fori_loop(0, tt // C, inter, 0)
    s_out_ref[0] = S[...]


def _gdn_prompt(q, k, v, la, bt, *, tt, unroll):
    n, t, hw = q.shape
    C = GDN_CHUNK
    ltri = jnp.tril(jnp.ones((C, C), F32)).astype(BF16)
    ones = jnp.ones((C, C), BF16)
    row = pl.BlockSpec((1, tt, hw), lambda i, j: (i, j, 0))
    cst = pl.BlockSpec((C, C), lambda i, j: (0, 0))
    nch = tt // C
    return pl.pallas_call(
        functools.partial(_gdn_kernel, tt=tt, hw=hw, unroll=unroll),
        grid=(n, t // tt),
        in_specs=[row] * 5 + [cst, cst],
        out_specs=[row, pl.BlockSpec((1, HEAD_DIM, hw), lambda i, j: (i, 0, 0))],
        out_shape=[jax.ShapeDtypeStruct((n, t, hw), F32), jax.ShapeDtypeStruct((n, HEAD_DIM, hw), F32)],
        scratch_shapes=[pltpu.VMEM((HEAD_DIM, hw), F32), pltpu.VMEM((nch * HEAD_DIM, hw), F32),
                        pltpu.VMEM((nch * HEAD_DIM, hw), F32), pltpu.VMEM((tt, hw), F32),
                        pltpu.VMEM((tt, hw), F32), pltpu.VMEM((nch * 8, hw), F32)],
        compiler_params=_cparams(("arbitrary", "arbitrary")),
        name="gdn_prompt",
    )(q, k, v, la, bt, ltri, ones)


def _gdn_step_kernel(q_ref, k_ref, v_ref, la_ref, bt_ref, s_ref, bones_ref, o_ref, s_out_ref, *, gs, hw):
    rowi = lax.broadcasted_iota(jnp.int32, (HEAD_DIM, hw), 0)
    colj = _lane_in_head(lax.broadcasted_iota(jnp.int32, (HEAD_DIM, hw), 1))
    eye = (colj == rowi).astype(F32)
    bones = bones_ref[...]
    for i in range(gs):
        r = slice(i, i + 1)
        q = q_ref[r, :] * (HEAD_DIM ** -0.5)
        k = k_ref[r, :]
        v = v_ref[r, :]
        eg = jnp.exp(la_ref[r, :])
        b = bt_ref[r, :]
        s = s_ref[i]
        kbc = _dot_f32_lhs(eye * k, bones)
        qbc = _dot_f32_lhs(eye * q, bones)
        ks = jnp.sum(kbc * s, axis=0, keepdims=True)
        qs = jnp.sum(qbc * s, axis=0, keepdims=True)
        qk = jnp.sum(kbc * qbc, axis=0, keepdims=True)
        vn = b * v - b * eg * ks
        o_ref[r, :] = eg * qs + qk * vn
        s_out_ref[i] = s * eg + kbc * vn


def _gdn_sample(q, k, v, la, bt, s_packed, bones, *, gs):
    m, hw = q.shape
    row = pl.BlockSpec((gs, hw), lambda i: (i, 0))
    st = pl.BlockSpec((gs, HEAD_DIM, hw), lambda i: (i, 0, 0))
    return pl.pallas_call(
        functools.partial(_gdn_step_kernel, gs=gs, hw=hw),
        grid=(m // gs,),
        in_specs=[row] * 5 + [st, pl.BlockSpec(bones.shape, lambda i: (0, 0))],
        out_specs=[row, st],
        out_shape=[jax.ShapeDtypeStruct((m, hw), F32), jax.ShapeDtypeStruct(s_packed.shape, F32)],
        compiler_params=_cparams(("arbitrary",)),
        name="gdn_sample",
    )(q, k, v, la, bt, s_packed, bones)


def _band_kernel(q_ref, k_ref, v_ref, o_ref, kf, vf, oc, lc, *, tt, unroll):
    j = pl.program_id(2)

    @pl.when(j == 0)
    def _():
        kf[0:tt, :] = jnp.zeros((tt, LANES), F32)
        vf[0:tt, :] = jnp.zeros((tt, LANES), F32)

    kf[tt:, :] = k_ref[0]
    vf[tt:, :] = v_ref[0]
    a = lax.bitwise_and(lax.broadcasted_iota(jnp.int32, (2 * BAND, 2 * BAND), 0), BAND - 1)
    c = lax.broadcasted_iota(jnp.int32, (2 * BAND, 2 * BAND), 1)
    band = (c >= a) & (c <= a + BAND)
    bias = jnp.where(band, 0.0, NEG)
    bias_first = jnp.where(band & (c >= BAND), 0.0, NEG)
    row_h1 = lax.broadcasted_iota(jnp.int32, (2 * BAND, LANES), 0) >= BAND
    lane_h1 = lax.broadcasted_iota(jnp.int32, (2 * BAND, LANES), 1) >= HEAD_DIM
    own = row_h1 == lane_h1
    out_h1 = lax.broadcasted_iota(jnp.int32, (BAND, LANES), 1) >= HEAD_DIM
    ones_kv = jnp.ones((2 * BAND, LANES), BF16)

    for ci, (_, d) in enumerate(DILATIONS):
        nblk = tt // (BAND * d)

        def body(it, carry, d=d, ci=ci, nblk=nblk):
            rows, krows, firsts = [], [], []
            for uu in range(unroll):
                idx = it * unroll + uu
                r = idx // nblk
                b = idx % nblk
                start = b * (BAND * d) + r
                if d == 1:
                    rows.append(pl.ds(start, BAND))
                    krows.append(pl.ds(tt + start - BAND, 2 * BAND))
                else:
                    rows.append(pl.ds(start, BAND, stride=d))
                    krows.append(pl.ds(tt + start - BAND * d, 2 * BAND, stride=d))
                firsts.append(jnp.logical_and(j == 0, b == 0))
            ss = []
            for rw, kr in zip(rows, krows):
                qb = q_ref[0, rw, :] * (HEAD_DIM ** -0.5)
                qs = jnp.where(own, jnp.concatenate([qb, qb], axis=0), 0.0).astype(BF16)
                ss.append(_dot_nt(qs, kf[kr, :].astype(BF16)))
            ps, ms = [], []
            for s, first in zip(ss, firsts):
                s = s + jnp.where(first, bias_first, bias)
                m = jnp.max(s, axis=-1, keepdims=True)
                ps.append(jnp.exp((s - m).astype(BF16)))
                ms.append(m)
            pvl = [_dot(p, jnp.concatenate([vf[kr, :].astype(BF16), ones_kv], axis=1)) for p, kr in zip(ps, krows)]
            for rw, r, m in zip(rows, pvl, ms):
                l = r[:, LANES:]
                on = r[:, :LANES] / l
                lse = m + jnp.log(l)
                oc[ci, rw, :] = jnp.where(out_h1, on[BAND:], on[:BAND])
                lc[ci, rw, :] = jnp.where(out_h1, lse[BAND:], lse[:BAND])
            return carry

        lax.fori_loop(0, tt // BAND // unroll, body, 0)

    mrows = 256
    for ch in range(tt // mrows):
        rs = slice(ch * mrows, (ch + 1) * mrows)
        ls = [lc[ci, rs, :] for ci in range(len(DILATIONS))]
        mx = functools.reduce(jnp.maximum, ls)
        es = [jnp.exp(l - mx) for l in ls]
        o_ref[0, rs, :] = sum(e * oc[ci, rs, :] for ci, e in enumerate(es)) / sum(es)

    kf[0:tt, :] = kf[tt:, :]
    vf[0:tt, :] = vf[tt:, :]


def _band(q, k, v, *, tt, unroll):
    n, t, hw = q.shape
    assert all(w // d == BAND and tt % w == 0 for w, d in DILATIONS) and t % tt == 0
    blk = pl.BlockSpec((1, tt, LANES), lambda i, c, j: (i, j, c))
    nd = len(DILATIONS)
    return pl.pallas_call(
        functools.partial(_band_kernel, tt=tt, unroll=unroll),
        grid=(n, hw // LANES, t // tt),
        in_specs=[blk, blk, blk],
        out_specs=blk,
        out_shape=jax.ShapeDtypeStruct((n, t, hw), F32),
        scratch_shapes=[pltpu.VMEM((2 * tt, LANES), F32), pltpu.VMEM((2 * tt, LANES), F32),
                        pltpu.VMEM((nd, tt, LANES), F32), pltpu.VMEM((nd, tt, LANES), F32)],
        compiler_params=_cparams(("arbitrary", "arbitrary", "arbitrary")),
        name="band",
    )(q, k, v)


def _cache_attn_kernel(q_ref, kn_ref, vn_ref, k_ref, v_ref, o_ref, *, nh, n_past):
    pos = lax.broadcasted_iota(jnp.int32, (1, n_past), 1)
    cnt = jnp.zeros((1, n_past), F32)
    for window, dil in DILATIONS:
        hit = (pos >= n_past - window) & (lax.bitwise_and(pos, dil - 1) == 0)
        cnt = cnt + jnp.where(hit, 1.0, 0.0)
    live = cnt > 0.0
    nd = float(len(DILATIONS))
    lane_h = lax.broadcasted_iota(jnp.int32, (HEAD_DIM, nh), 1)
    out = jnp.zeros((HEAD_DIM, nh), F32)
    for h in range(nh):
        q = q_ref[0, :, h:h + 1] * (HEAD_DIM ** -0.5)
        s = jnp.where(live, jnp.sum(k_ref[0, h] * q, axis=0, keepdims=True), NEG)
        s0 = jnp.sum(q * kn_ref[0, :, h:h + 1], axis=0, keepdims=True)
        m = jnp.maximum(jnp.max(s, axis=-1, keepdims=True), s0)
        p = cnt * jnp.exp(s - m)
        p0 = nd * jnp.exp(s0 - m)
        den = p0 + jnp.sum(p, axis=-1, keepdims=True)
        num = p0 * vn_ref[0, :, h:h + 1] + jnp.sum(v_ref[0, h] * p, axis=-1, keepdims=True)
        out = jnp.where(lane_h == h, num / den, out)
    o_ref[0] = out


def _cache_attn(q_t, kn_t, vn_t, ck_t, cv_t):
    m, hd, nh = q_t.shape
    n_past = ck_t.shape[3]
    assert all(w // d == BAND and n_past >= w and d & (d - 1) == 0 and n_past % d == 0 for w, d in DILATIONS)
    col = pl.BlockSpec((1, hd, nh), lambda i: (i, 0, 0))
    cache = pl.BlockSpec((1, nh, hd, n_past), lambda i: (i, 0, 0, 0))
    return pl.pallas_call(
        functools.partial(_cache_attn_kernel, nh=nh, n_past=n_past),
        grid=(m,),
        in_specs=[col, col, col, cache, cache],
        out_specs=col,
        out_shape=jax.ShapeDtypeStruct((m, hd, nh), F32),
        compiler_params=_cparams(("arbitrary",)),
        name="cache_attn",
    )(q_t, kn_t, vn_t, ck_t, cv_t)


def _post_kernel(*refs, tm, precise):
    if precise:
        (x_ref, og_ref, z_ref, ob_ref, ga_ref, gb_ref, bones_ref, wo_ref, gf_ref, wrh_ref, wrl_ref, br_ref,
         wol_ref, h_ref, route_ref, rt_ref, cnt_ref, cnt) = refs
    else:
        (x_ref, og_ref, z_ref, ob_ref, ga_ref, gb_ref, bones_ref, wo_ref, gf_ref, wrh_ref, wrl_ref, br_ref,
         h_ref, route_ref, rt_ref, cnt_ref, cnt) = refs
    i = pl.program_id(0)

    @pl.when(i == 0)
    def _():
        cnt[...] = jnp.zeros_like(cnt)

    bones = bones_ref[...]
    oa = _head_rms(og_ref[...], bones, ga_ref[...], precise) * _silu(z_ref[...])
    ob = _head_rms(ob_ref[...], bones, gb_ref[...], precise)
    mixf = jnp.concatenate([oa, ob], axis=-1)
    mix = mixf.astype(BF16)
    proj = _dot(mix, wo_ref[...])
    if precise:
        proj = proj + _dot((mixf - mix.astype(F32)).astype(BF16), wo_ref[...]) + _dot(mix, wol_ref[...])
    h = x_ref[...] + proj
    h_ref[...] = h

    mrow = _rms(h, gf_ref[...])
    mh = mrow.astype(BF16)
    ml = (mrow - mh.astype(F32)).astype(BF16)
    logit = _dot(mh, wrh_ref[...]) + _dot(mh, wrl_ref[...]) + _dot(ml, wrh_ref[...]) + br_ref[...]
    lane = lax.broadcasted_iota(jnp.int32, logit.shape, 1).astype(F32)
    gl = jnp.where(lane < N_GROUPS, logit, NEG)
    gmax = jnp.max(gl, axis=-1, keepdims=True)
    grp = jnp.min(jnp.where(gl == gmax, lane, 1e9), axis=-1, keepdims=True)
    pg = 1.0 / jnp.sum(jnp.exp(gl - gmax), axis=-1, keepdims=True)
    lo = N_GROUPS + grp * EXPERTS_PER_GROUP
    el = jnp.where((lane >= lo) & (lane < lo + EXPERTS_PER_GROUP), logit, NEG)
    v1 = jnp.max(el, axis=-1, keepdims=True)
    i1 = jnp.min(jnp.where(el == v1, lane, 1e9), axis=-1, keepdims=True)
    el2 = jnp.where(lane == i1, NEG, el)
    v2 = jnp.max(el2, axis=-1, keepdims=True)
    i2 = jnp.min(jnp.where(el2 == v2, lane, 1e9), axis=-1, keepdims=True)
    e = jnp.exp(v2 - v1)
    g1 = pg / (1.0 + e)
    g2 = pg * e / (1.0 + e)
    e1 = i1 - N_GROUPS
    e2 = i2 - N_GROUPS

    oh1 = lane == e1
    oh2 = lane == e2
    onehot = jnp.where(oh1 | oh2, 1.0, 0.0)
    ri = lax.broadcasted_iota(jnp.int32, (tm, tm), 0)
    ci = lax.broadcasted_iota(jnp.int32, (tm, tm), 1)
    tri = jnp.where(ci < ri, 1.0, 0.0).astype(BF16)
    before = _dot(tri, onehot.astype(BF16)) + cnt[...]
    r1 = jnp.sum(jnp.where(oh1, before, 0.0), axis=-1, keepdims=True)
    r2 = jnp.sum(jnp.where(oh2, before, 0.0), axis=-1, keepdims=True)
    cnt[...] = cnt[...] + jnp.sum(onehot, axis=0, keepdims=True)
    cnt_ref[...] = cnt[...]
    route = jnp.zeros_like(logit)
    for j, val in enumerate((e1, e2, r1, r2, g1, g2)):
        route = jnp.where(lane == j, val, route)
    route_ref[...] = route
    rt_ref[...] = jnp.transpose(route)[:ROW_TILE]


def _post(x2, og, z, ob, ga, gb, bones, wo, gf, wrh, wrl, br, wo_lo, *, tm):
    m, d = x2.shape
    hw = og.shape[1]
    row = lambda w: pl.BlockSpec((tm, w), lambda i: (i, 0))
    const = lambda a: pl.BlockSpec(a.shape, lambda i: (0,) * a.ndim)
    consts = [ga, gb, bones, wo, gf, wrh, wrl, br] + ([] if wo_lo is None else [wo_lo])
    return pl.pallas_call(
        functools.partial(_post_kernel, tm=tm, precise=wo_lo is not None),
        grid=(m // tm,),
        in_specs=[row(d), row(hw), row(hw), row(hw)] + [const(a) for a in consts],
        out_specs=[row(d), row(LANES), pl.BlockSpec((ROW_TILE, tm), lambda i: (0, i)),
                   pl.BlockSpec((1, LANES), lambda i: (0, 0))],
        out_shape=[jax.ShapeDtypeStruct((m, d), F32), jax.ShapeDtypeStruct((m, LANES), F32),
                   jax.ShapeDtypeStruct((ROW_TILE, m), F32), jax.ShapeDtypeStruct((1, LANES), F32)],
        scratch_shapes=[pltpu.VMEM((1, LANES), F32)],
        compiler_params=_cparams(("arbitrary",)),
        name=f"post_{m}",
    )(x2, og, z, ob, *consts)


ROW_TILE = 8


def _rows_from_tiles(ref, n):
    return jnp.concatenate([ref[pl.ds(c, n, stride=ROW_TILE), :] for c in range(ROW_TILE)], axis=1)


def _rows_to_tiles(ref, rows):
    n = rows.shape[0]
    for c in range(ROW_TILE):
        ref[pl.ds(c, n, stride=ROW_TILE), :] = rows[:, c * LANES:(c + 1) * LANES]


def _dispatch_kernel(pstart_ref, pend_ref, h_ref, gf_ref, dest_ref, xbuf_ref, mrow, zbuf, dsm, sem, dsem, zsem,
                     *, tm, bm, nsteps, n_blk):
    i = pl.program_id(0)
    slot = i % 2
    dcp = pltpu.make_async_copy(dest_ref.at[i], dsm, dsem)
    dcp.start()

    def zero_copy(e):
        start = pl.multiple_of((pend_ref[e] - bm) * ROW_TILE, bm * ROW_TILE)
        return pltpu.make_async_copy(zbuf, xbuf_ref.at[pl.ds(start, bm * ROW_TILE)], zsem)

    def zero_block(b):
        start = pl.multiple_of(b * (bm * ROW_TILE), bm * ROW_TILE)
        return pltpu.make_async_copy(zbuf, xbuf_ref.at[pl.ds(start, bm * ROW_TILE)], zsem)

    @pl.when(i == 0)
    def _():
        zbuf[...] = jnp.zeros_like(zbuf)
        n_used = pend_ref[N_EXPERTS - 1] // bm
        for start_or_wait in (True, False):
            for e in range(N_EXPERTS):
                @pl.when(pend_ref[e] > pstart_ref[e])
                def _(e=e, start_or_wait=start_or_wait):
                    zero_copy(e).start() if start_or_wait else zero_copy(e).wait()

            def tail(b, carry, start_or_wait=start_or_wait):
                zero_block(b).start() if start_or_wait else zero_block(b).wait()
                return carry

            lax.fori_loop(n_used, n_blk, tail, 0)

    def wait_rows(s):
        for _ in range(2):
            pltpu.make_async_copy(mrow.at[s], xbuf_ref.at[pl.ds(0, tm * ROW_TILE)], sem.at[s]).wait()

    @pl.when(i >= 2)
    def _():
        wait_rows(slot)

    m = _rms(h_ref[...], gf_ref[...])
    dcp.wait()

    for s in range(2):
        @pl.when(slot == s)
        def _(s=s):
            _rows_to_tiles(mrow.at[s], m)

            def body(r, carry):
                src = mrow.at[s, pl.ds(pl.multiple_of(r * ROW_TILE, ROW_TILE), ROW_TILE)]
                for kk in range(2):
                    d = pl.multiple_of(dsm[kk * tm + r] * ROW_TILE, ROW_TILE)
                    pltpu.make_async_copy(src, xbuf_ref.at[pl.ds(d, ROW_TILE)], sem.at[s]).start()
                return carry

            lax.fori_loop(0, tm, body, 0, unroll=8)

    @pl.when(i == nsteps - 1)
    def _():
        wait_rows(slot)
        if nsteps >= 2:
            wait_rows(1 - slot)


def _dispatch(h, gf, dest, pstart, pend, n_slots, *, tm, bm):
    m, d = h.shape
    assert d == ROW_TILE * LANES
    nsteps = m // tm
    grid_spec = pltpu.PrefetchScalarGridSpec(
        num_scalar_prefetch=2,
        grid=(nsteps,),
        in_specs=[pl.BlockSpec((tm, d), lambda i, ps, pe: (i, 0)), pl.BlockSpec((1, d), lambda i, ps, pe: (0, 0)),
                  pl.BlockSpec(memory_space=pl.ANY)],
        out_specs=pl.BlockSpec(memory_space=pl.ANY),
        scratch_shapes=[pltpu.VMEM((2, tm * ROW_TILE, LANES), F32), pltpu.VMEM((bm * ROW_TILE, LANES), F32),
                        pltpu.SMEM((2 * tm,), jnp.int32),
                        pltpu.SemaphoreType.DMA((2,)), pltpu.SemaphoreType.DMA, pltpu.SemaphoreType.DMA],
    )
    return pl.pallas_call(
        functools.partial(_dispatch_kernel, tm=tm, bm=bm, nsteps=nsteps, n_blk=n_slots // bm),
        grid_spec=grid_spec,
        out_shape=jax.ShapeDtypeStruct((n_slots * ROW_TILE, LANES), F32),
        compiler_params=_cparams(("arbitrary",), row_dma=True),
        name=f"dispatch_{m}",
    )(pstart, pend, h, gf, dest)


def _expert_kernel(be_ref, nu_ref, x_ref, wg_ref, wu_ref, wd_ref, y_ref, *, bm):
    b = pl.program_id(0)

    @pl.when(b < nu_ref[0])
    def _():
        x = _rows_from_tiles(x_ref, bm).astype(BF16)
        hid = _silu(_dot(x, wg_ref[0])) * _dot(x, wu_ref[0])
        _rows_to_tiles(y_ref, _dot(hid.astype(BF16), wd_ref[0]))

    @pl.when(b >= nu_ref[0])
    def _():
        y_ref[...] = jnp.zeros_like(y_ref)


def _experts(xbuf, blk_expert, n_used, wg, wu, wd, *, bm):
    d, de = wg.shape[1], wg.shape[2]
    n_slots = xbuf.shape[0] // ROW_TILE
    blk = (bm * ROW_TILE, LANES)
    grid_spec = pltpu.PrefetchScalarGridSpec(
        num_scalar_prefetch=2,
        grid=(n_slots // bm,),
        in_specs=[pl.BlockSpec(blk, lambda b, be, nu: (jnp.minimum(b, nu[0] - 1), 0)),
                  pl.BlockSpec((1, d, de), lambda b, be, nu: (be[b], 0, 0)),
                  pl.BlockSpec((1, d, de), lambda b, be, nu: (be[b], 0, 0)),
                  pl.BlockSpec((1, de, d), lambda b, be, nu: (be[b], 0, 0))],
        out_specs=pl.BlockSpec(blk, lambda b, be, nu: (b, 0)),
    )
    return pl.pallas_call(
        functools.partial(_expert_kernel, bm=bm),
        grid_spec=grid_spec,
        out_shape=jax.ShapeDtypeStruct(xbuf.shape, F32),
        compiler_params=_cparams(("arbitrary",)),
        name=f"experts_{n_slots}",
    )(blk_expert, n_used, xbuf, wg, wu, wd)


def _combine_kernel(h_ref, route_ref, p_ref, wpg_ref, wpp_ref, gfin_ref, dest_ref, ybuf_ref, y_ref,
                    ys, dsm0, dsm1, sem, dsem, *, tm, nsteps):
    i = pl.program_id(0)
    slot = i % 2

    def issue(j, s):
        dsm = (dsm0, dsm1)[s]
        cp = pltpu.make_async_copy(dest_ref.at[j], dsm, dsem)
        cp.start()
        cp.wait()

        def body(r, carry):
            for kk in range(2):
                d = pl.multiple_of(dsm[kk * tm + r] * ROW_TILE, ROW_TILE)
                dst = ys.at[s, kk, pl.ds(pl.multiple_of(r * ROW_TILE, ROW_TILE), ROW_TILE)]
                pltpu.make_async_copy(ybuf_ref.at[pl.ds(d, ROW_TILE)], dst, sem.at[s]).start()
            return carry

        lax.fori_loop(0, tm, body, 0, unroll=8)

    @pl.when(i == 0)
    def _():
        issue(0, 0)

    for s in range(2):
        @pl.when(jnp.logical_and(i + 1 < nsteps, slot == 1 - s))
        def _(s=s):
            issue(i + 1, s)

    for kk in range(2):
        pltpu.make_async_copy(ybuf_ref.at[pl.ds(0, tm * ROW_TILE)], ys.at[slot, kk], sem.at[slot]).wait()

    route = route_ref[...]
    g1 = route[:, 4:5]
    g2 = route[:, 5:6]
    h = h_ref[...] + (g1 * _rows_from_tiles(ys.at[slot, 0], tm) + g2 * _rows_from_tiles(ys.at[slot, 1], tm))
    gate = _sigmoid(_dot(h.astype(BF16), wpg_ref[...]))
    out = h + gate * _dot(p_ref[...].astype(BF16), wpp_ref[...])
    y_ref[...] = _rms(out, gfin_ref[...])


def _combine(h, route, p, wpg, wpp, gfin, dest, ybuf, *, tm):
    m, d = h.shape
    row = lambda w: pl.BlockSpec((tm, w), lambda i: (i, 0))
    const = lambda a: pl.BlockSpec(a.shape, lambda i: (0,) * a.ndim)
    return pl.pallas_call(
        functools.partial(_combine_kernel, tm=tm, nsteps=m // tm),
        grid=(m // tm,),
        in_specs=[row(d), row(LANES), row(p.shape[1]), const(wpg), const(wpp), const(gfin),
                  pl.BlockSpec(memory_space=pl.ANY), pl.BlockSpec(memory_space=pl.ANY)],
        out_specs=row(d),
        out_shape=jax.ShapeDtypeStruct((m, d), F32),
        scratch_shapes=[pltpu.VMEM((2, 2, tm * ROW_TILE, LANES), F32), pltpu.SMEM((2 * tm,), jnp.int32),
                        pltpu.SMEM((2 * tm,), jnp.int32), pltpu.SemaphoreType.DMA((2,)), pltpu.SemaphoreType.DMA],
        compiler_params=_cparams(("arbitrary",), row_dma=True),
        name=f"combine_{m}",
    )(h, route, p, wpg, wpp, gfin, dest, ybuf)


def _tail(x2, og, z, ob, p2, wts, *, tm, bm, precise):
    m, d = x2.shape
    h, route, route_t, counts = _post(x2, og, z, ob, wts["ga"], wts["gb"], wts["bones"],
                                      wts["wo_hi"] if precise else wts["wo"], wts["gf"], wts["wrh"], wts["wrl"],
                                      wts["br"], wts["wo_lo"] if precise else None, tm=tm)
    eid = route_t[0:2].astype(jnp.int32)
    rank = route_t[2:4].astype(jnp.int32)
    sizes = counts[0, :N_EXPERTS].astype(jnp.int32)
    padded = (sizes + bm - 1) // bm * bm
    pend = jnp.cumsum(padded)
    pstart = pend - padded
    first_slot = jnp.sum(jnp.where(eid[..., None] == jnp.arange(N_EXPERTS, dtype=jnp.int32), pstart, 0), axis=-1)
    dest = (first_slot + rank).reshape(2, m // tm, tm).transpose(1, 0, 2).reshape(m // tm, 2 * tm)
    n_blk = (2 * m) // bm + N_EXPERTS
    blk_start = jnp.arange(n_blk, dtype=jnp.int32) * bm
    blk_expert = jnp.minimum(jnp.sum(pend[None, :] <= blk_start[:, None], axis=1), N_EXPERTS - 1).astype(jnp.int32)
    n_used = (pend[-1:] // bm).astype(jnp.int32)
    xbuf = _dispatch(h, wts["gf"], dest, pstart.astype(jnp.int32), pend.astype(jnp.int32), n_blk * bm,
                     tm=tm, bm=bm)
    ybuf = _experts(xbuf, blk_expert, n_used, wts["wg"], wts["wu"], wts["wd"], bm=bm)
    return _combine(h, route, p2, wts["wpg"], wts["wpp"], wts["gfin"], dest, ybuf, tm=tm)


def _split_hi_lo(w):
    bits = lax.bitcast_convert_type(w.astype(F32), jnp.uint32)
    hi32 = lax.bitcast_convert_type(bits & jnp.uint32(0xFFFF0000), F32)
    return hi32.astype(BF16), (w - hi32).astype(BF16)


def _rope_tables(pos):
    half = ROT_DIM // 2
    inv = ROPE_THETA ** (-jnp.arange(half, dtype=F32) * (2.0 / ROT_DIM))
    ang = pos.astype(F32)[:, None] * inv[None, :]
    cos, sin = jnp.cos(ang), jnp.sin(ang)
    j = jnp.arange(LANES) % HEAD_DIM
    first, second = j < half, (j >= half) & (j < ROT_DIM)
    jj = jnp.where(second, j - half, jnp.where(first, j, 0))
    c = jnp.where((first | second)[None, :], cos[:, jj], 1.0)
    s1 = jnp.where(first[None, :], -sin[:, jj], 0.0)
    s2 = jnp.where(second[None, :], sin[:, jj], 0.0)
    return c, s1, s2


def kernel(x_prompt, x_sample, cache_win_k, cache_win_v, state_conv, state_delta, p_prompt, p_sample, g_attn_norm, w_in, w_conv, a_log, dt_bias, g_a_out, g_b_out, w_out, g_ffn_norm, w_router_group, b_router_group, w_router_expert, b_router_expert, w_exp_gate, w_exp_up, w_exp_down, w_ple_gate, w_ple_proj, g_final):
    n, t, d = x_prompt.shape
    ns = x_sample.shape[0]
    assert w_in.shape[0] == 1 and x_sample.shape[1] == 1
    ha = a_log.shape[1]
    aw = ha * HEAD_DIM
    off_a = 4 * aw
    off_win = off_a + 2 * ha
    keep =min(DILATIONS[-1][0], t)
    hi = lambda a: a.astype(F32)

    w = w_in[0]
    w_ab = jnp.pad(w[:, off_a:off_win], ((0, 0), (0, LANES - 2 * ha)))
    w_cat32 = jnp.concatenate([w[:, :off_a], w[:, off_win:], w_ab], axis=1)
    w_cat = w_cat32.astype(BF16)
    w_cat_hi, w_cat_lo = _split_hi_lo(w_cat32)
    pad_l = lambda v: jnp.pad(hi(v), (0, LANES - v.shape[0]))[None, :]
    alog = pad_l(a_log[0])
    dtb = pad_l(dt_bias[0])
    lane_head = jnp.arange(2 * aw) // HEAD_DIM
    esel = (jnp.arange(LANES)[:, None] == lane_head[None, :]).astype(BF16)
    hd = jnp.arange(aw) // HEAD_DIM
    bones = (hd[:, None] == hd[None, :]).astype(BF16)
    wr = jnp.concatenate([hi(w_router_group[0]), hi(w_router_expert[0]).reshape(d, N_EXPERTS)], axis=1)
    wr = jnp.pad(wr, ((0, 0), (0, LANES - wr.shape[1])))
    wrh, wrl = _split_hi_lo(wr)
    wo_hi, wo_lo = _split_hi_lo(w_out[0])
    br = jnp.concatenate([hi(b_router_group[0]), hi(b_router_expert[0]).reshape(N_EXPERTS)])
    wts = dict(
        ga=jnp.tile(hi(g_a_out[0]), ha)[None, :], gb=jnp.tile(hi(g_b_out[0]), ha)[None, :], bones=bones,
        wo=w_out[0].astype(BF16), wo_hi=wo_hi, wo_lo=wo_lo,
        gf=hi(g_ffn_norm[0])[None, :], wrh=wrh, wrl=wrl, br=pad_l(br),
        wg=w_exp_gate[0].astype(BF16), wu=w_exp_up[0].astype(BF16), wd=w_exp_down[0].astype(BF16),
        wpg=w_ple_gate[0].astype(BF16), wpp=w_ple_proj[0].astype(BF16), gfin=hi(g_final)[None, :])
    g_attn = hi(g_attn_norm[0])[None, :]
    wconv = hi(w_conv[0])

    tm_p = min(256, t)
    cos, s1, s2 = _rope_tables(jnp.arange(t, dtype=jnp.int32))
    qa, ka, va, z, la, bt, qb, kb, vb, conv_p = _proj(
        x_prompt, g_attn, w_cat, wconv, alog, dtb, esel, cos, s1, s2, bones, None, tm=tm_p, sample=False)
    og, s_fin = _gdn_prompt(qa, ka, va, la, bt, tt=min(512, t), unroll=4)
    ob = _band(qb, kb, vb, tt=DILATIONS[-1][0], unroll=8)
    flat = lambda a: a.reshape(n * t, a.shape[-1])
    y_prompt = _tail(flat(x_prompt), flat(og), flat(z), flat(ob), flat(p_prompt[0]), wts,
                     tm=min(512, t), bm=256, precise=False).reshape(n, t, d)

    cos, s1, s2 = _rope_tables(jnp.full((1,), PAST_LEN, jnp.int32))
    st = [state_conv[0][None, :, j, :] for j in range(state_conv.shape[2])]
    xs3 = x_sample.reshape(1, ns, d)
    qa_s, ka_s, va_s, z_s, la_s, bt_s, qb_s, kb_s, vb_s, ua_s = [
        a[0] for a in _proj(xs3, g_attn, w_cat_hi, wconv, alog, dtb, esel, cos, s1, s2, bones, [w_cat_lo] + st,
                            tm=ns, sample=True)]
    s_packed = state_delta[0].transpose(0, 2, 1, 3).reshape(ns, HEAD_DIM, aw)
    og_s, s_new = _gdn_sample(qa_s, ka_s, va_s, la_s, bt_s, s_packed, bones, gs=8)
    cols = lambda a: a.reshape(ns, ha, HEAD_DIM).transpose(0, 2, 1)
    pos_minor = lambda c: c[0].transpose(0, 2, 3, 1)
    ob_s = _cache_attn(cols(qb_s), cols(kb_s), cols(vb_s), pos_minor(cache_win_k), pos_minor(cache_win_v))
    ob_s = ob_s.transpose(0, 2, 1).reshape(ns, aw)
    y_sample = _tail(x_sample.reshape(ns, d), og_s, z_s, ob_s, p_sample[0].reshape(ns, -1), wts,
                     tm=ns, bm=128, precise=True).reshape(ns, 1, d)

    unpack = lambda s: s.reshape(-1, HEAD_DIM, ha, HEAD_DIM).transpose(0, 2, 1, 3)[None]
    heads = lambda a: a.reshape(a.shape[0], -1, ha, HEAD_DIM)
    return (y_prompt, y_sample,
            heads(kb[:, t - keep:])[None], heads(vb[:, t - keep:])[None],
            conv_p[:, 8 - state_conv.shape[2]:][None], unpack(s_fin),
            heads(kb_s[:, None])[None], heads(vb_s[:, None])[None],
            jnp.concatenate([state_conv[0][:, 1:], ua_s[:, None]], axis=1)[None], unpack(s_new))
```

```python
import functools
import math

import jax
import jax.numpy as jnp
from jax import lax
from jax.experimental import pallas as pl
from jax.experimental.pallas import tpu as pltpu

F32 = jnp.float32
BF16 = jnp.bfloat16

HEAD_DIM = 64
GDN_CHUNK = 64
ROT_DIM = HEAD_DIM // 4
ROPE_THETA = 500000.0
PAST_LEN = 8192
DILATIONS = ((128, 1), (512, 4), (2048, 16))
BAND = 128
N_GROUPS = 4
EXPERTS_PER_GROUP = 8
N_EXPERTS = N_GROUPS * EXPERTS_PER_GROUP
NORM_EPS = 1e-6
NEG = -1e30
LANES = 128
VMEM_LIMIT = 56 * 1024 * 1024


def _cparams(sem, row_dma=False):
    return pltpu.CompilerParams(dimension_semantics=sem, vmem_limit_bytes=VMEM_LIMIT,
                                disable_bounds_checks=row_dma)


def _dot(a, b):
    return jnp.dot(a, b, preferred_element_type=F32)


def _dot_nt(a, b):
    return lax.dot_general(a, b, (((1,), (1,)), ((), ())), preferred_element_type=F32)


def _dot_tn(a, b):
    return lax.dot_general(a, b, (((0,), (0,)), ((), ())), preferred_element_type=F32)


def _split3(x):
    hi = x.astype(BF16)
    r = x - hi.astype(F32)
    mid = r.astype(BF16)
    lo = (r - mid.astype(F32)).astype(BF16)
    return hi, mid, lo


def _dot_f32_lhs(x, w_bf16):
    hi, mid, lo = _split3(x)
    return _dot(hi, w_bf16) + _dot(mid, w_bf16) + _dot(lo, w_bf16)


def _dot_f32_rhs(w_bf16, x):
    hi, mid, lo = _split3(x)
    return _dot(w_bf16, hi) + _dot(w_bf16, mid) + _dot(w_bf16, lo)


HEAD_SHIFT = HEAD_DIM.bit_length() - 1


def _lane_head(i):
    return lax.shift_right_logical(i, HEAD_SHIFT)


def _lane_in_head(i):
    return lax.bitwise_and(i, HEAD_DIM - 1)


def _sigmoid(x):
    return 1.0 / (1.0 + jnp.exp(-x))


def _silu(x):
    return x * _sigmoid(x)


def _softplus(x):
    return jnp.maximum(x, 0.0) + jnp.log1p(jnp.exp(-jnp.abs(x)))


def _rms(x, g):
    return x * lax.rsqrt(jnp.mean(x * x, axis=-1, keepdims=True) + NORM_EPS) * g


MXU_TILE = 256


def _head_sums(x2, bones):
    tile = bones[:MXU_TILE, :MXU_TILE]
    xb = x2.astype(BF16)
    return jnp.concatenate([_dot(xb[:, c:c + MXU_TILE], tile) for c in range(0, x2.shape[1], MXU_TILE)], axis=1)


def _head_rms(x, bones, g, precise=False):
    ss = _dot_f32_lhs(x * x, bones) if precise else _head_sums(x * x, bones)
    return x * lax.rsqrt(ss * (1.0 / HEAD_DIM) + NORM_EPS) * g


def _proj_kernel(*refs, tm, aw, sample):
    if sample:
        (x_ref, g_ref, w_ref, wc_ref, alog_ref, dtb_ref, esel_ref, cos_ref, s1_ref, s2_ref, bones_ref,
         wl_ref, st0_ref, st1_ref, st2_ref,
         qa_ref, ka_ref, va_ref, z_ref, la_ref, bt_ref, qb_ref, kb_ref, vb_ref, conv_ref) = refs
        buf = None
    else:
        (x_ref, g_ref, w_ref, wc_ref, alog_ref, dtb_ref, esel_ref, cos_ref, s1_ref, s2_ref, bones_ref,
         qa_ref, ka_ref, va_ref, z_ref, la_ref, bt_ref, qb_ref, kb_ref, vb_ref, conv_ref, buf) = refs
        t = pl.program_id(1)

        @pl.when(t == 0)
        def _():
            buf[0:8, :] = jnp.zeros((8, 3 * aw), F32)

    xf = _rms(x_ref[0], g_ref[...])
    xn = xf.astype(BF16)
    bones = bones_ref[...]
    if sample:
        xlo = (xf - xn.astype(F32)).astype(BF16)
        proj = lambda cs: _dot(xn, w_ref[:, cs]) + _dot(xlo, w_ref[:, cs]) + _dot(xn, wl_ref[:, cs])
        sumsq = lambda y: _dot_f32_lhs(y * y, bones)
    else:
        proj = lambda cs: _dot(xn, w_ref[:, cs])
        sumsq = lambda y: _head_sums(y * y, bones)

    outs_a = (qa_ref, ka_ref, va_ref)
    for c in range(3):
        cs = slice(c * aw, (c + 1) * aw)
        u = proj(cs)
        if sample:
            conv_ref[0, :, cs] = u
            y = (wc_ref[3:4, cs] * u + wc_ref[2:3, cs] * st2_ref[0, :, cs]
                 + wc_ref[1:2, cs] * st1_ref[0, :, cs] + wc_ref[0:1, cs] * st0_ref[0, :, cs])
        else:
            buf[8:8 + tm, cs] = u
            y = (wc_ref[3:4, cs] * u + wc_ref[2:3, cs] * buf[7:7 + tm, cs]
                 + wc_ref[1:2, cs] * buf[6:6 + tm, cs] + wc_ref[0:1, cs] * buf[5:5 + tm, cs])
        y = _silu(y)
        if c < 2:
            y = y * lax.rsqrt(sumsq(y) + NORM_EPS)
        outs_a[c][0] = y
    if not sample:
        tail = buf[tm:tm + 8, :]
        conv_ref[0] = tail
        buf[0:8, :] = tail

    z_ref[0] = proj(slice(3 * aw, 4 * aw))

    ab = proj(slice(7 * aw, 7 * aw + LANES))
    lane = lax.broadcasted_iota(jnp.int32, ab.shape, 1)
    log_a = -jnp.exp(alog_ref[...]) * _softplus(ab + dtb_ref[...])
    comb = jnp.where(lane < aw // HEAD_DIM, log_a, _sigmoid(ab))
    if sample:
        ex = _dot_f32_lhs(comb, esel_ref[...])
    else:
        c_hi = comb.astype(BF16)
        ex = _dot(c_hi, esel_ref[...]) + _dot((comb - c_hi.astype(F32)).astype(BF16), esel_ref[...])
    la_ref[0] = ex[:, :aw]
    bt_ref[0] = ex[:, aw:]

    cosv, s1v, s2v = cos_ref[...], s1_ref[...], s2_ref[...]
    for c, oref in ((0, qb_ref), (1, kb_ref)):
        for gq in range(aw // LANES):
            cs = slice(4 * aw + c * aw + gq * LANES, 4 * aw + c * aw + (gq + 1) * LANES)
            u = proj(cs)
            r = (u * cosv + pltpu.roll(u, LANES - ROT_DIM // 2, 1) * s1v
                 + pltpu.roll(u, ROT_DIM // 2, 1) * s2v)
            oref[0, :, gq * LANES:(gq + 1) * LANES] = r
    vb_ref[0] = proj(slice(6 * aw, 7 * aw))


def _proj(x3, g, w_cat, wconv, alog, dtb, esel, cos, s1, s2, bones, states, *, tm, sample):
    n, t, d = x3.shape
    aw = bones.shape[0]
    nt = t // tm
    const = lambda shape: pl.BlockSpec(shape, lambda i, j: (0,) * len(shape))
    row = lambda width: pl.BlockSpec((1, tm, width), lambda i, j: (i, j, 0))
    tab = (pl.BlockSpec((1, LANES), lambda i, j: (0, 0)) if sample
           else pl.BlockSpec((tm, LANES), lambda i, j: (j, 0)))
    in_specs = [row(d), const((1, d)), const(w_cat.shape), const(wconv.shape), const((1, LANES)),
                const((1, LANES)), const(esel.shape), tab, tab, tab, const(bones.shape)]
    args = [x3, g, w_cat, wconv, alog, dtb, esel, cos, s1, s2, bones]
    if sample:
        w_lo, states = states[0], states[1:]
        in_specs += [const(w_lo.shape)] + [row(3 * aw)] * 3
        args += [w_lo] + list(states)
        conv_shape = jax.ShapeDtypeStruct((n, t, 3 * aw), F32)
        conv_spec = row(3 * aw)
        scratch = []
    else:
        conv_shape = jax.ShapeDtypeStruct((n, 8, 3 * aw), F32)
        conv_spec = pl.BlockSpec((1, 8, 3 * aw), lambda i, j: (i, 0, 0))
        scratch = [pltpu.VMEM((tm + 8, 3 * aw), F32)]
    o = jax.ShapeDtypeStruct((n, t, aw), F32)
    return pl.pallas_call(
        functools.partial(_proj_kernel, tm=tm, aw=aw, sample=sample),
        grid=(n, nt),
        in_specs=in_specs,
        out_specs=[row(aw)] * 9 + [conv_spec],
        out_shape=[o] * 9 + [conv_shape],
        scratch_shapes=scratch,
        compiler_params=_cparams(("arbitrary", "arbitrary")),
        name="proj_sample" if sample else "proj_prompt",
    )(*args)


def _rowstack(x, hms):
    return jnp.concatenate([jnp.where(hm, x, 0.0) for hm in hms], axis=0)


def _gdn_kernel(q_ref, k_ref, v_ref, la_ref, bt_ref, ltri_ref, ones_ref, o_ref, s_out_ref,
                S, KW, NN, QP, OU, AL, *, tt, hw, unroll):
    C = GDN_CHUNK
    qw = 4 * HEAD_DIM
    nq = hw // qw
    t = pl.program_id(1)

    @pl.when(t == 0)
    def _():
        S[...] = jnp.zeros_like(S)

    rowi = lax.broadcasted_iota(jnp.int32, (C, hw), 0)
    colj = _lane_in_head(lax.broadcasted_iota(jnp.int32, (C, hw), 1))
    m_incl = colj <= rowi
    m_strict = colj < rowi
    eye = (colj == rowi).astype(F32)
    lane_q = _lane_head(lax.broadcasted_iota(jnp.int32, (1, qw), 1))
    hms = [lane_q == h for h in range(4)]
    ltri = ltri_ref[...]
    ones = ones_ref[...]

    def pm(l, r):
        return _dot(l.astype(BF16), _rowstack(r, hms).astype(BF16))

    def diag_blocks(full):
        return sum(jnp.where(hms[h], full[h * HEAD_DIM:(h + 1) * HEAD_DIM, :], 0.0) for h in range(4))

    sls = [slice(qi * qw, (qi + 1) * qw) for qi in range(nq)]

    def intra(it, carry):
        cs = [it * unroll + uu for uu in range(unroll)]
        rws = [pl.ds(pl.multiple_of(c * C, C), C) for c in cs]
        qs = [q_ref[0, r, :] * (HEAD_DIM ** -0.5) for r in rws]
        ks = [k_ref[0, r, :] for r in rws]
        bs = [bt_ref[0, r, :] for r in rws]
        gs = [_dot_f32_rhs(ltri, la_ref[0, r, :]) for r in rws]
        grs = [_dot_f32_rhs(ones, g * eye) for g in gs]
        decs = [jnp.exp(jnp.where(m_incl, g - gr, NEG)) for g, gr in zip(gs, grs)]
        egs = [jnp.exp(g) for g in gs]
        kbs = [k * b for k, b in zip(ks, bs)]
        for c, g in zip(cs, gs):
            AL[pl.ds(pl.multiple_of(c * 8, 8), 8), :] = jnp.broadcast_to(jnp.exp(g[C - 1:C, :]), (8, hw))
        ch = [(ui, sl) for ui in range(unroll) for sl in sls]
        kst = [_rowstack(ks[ui][:, sl], hms).astype(BF16) for ui, sl in ch]
        kq = [_dot_nt(jnp.concatenate([kbs[ui][:, sl], qs[ui][:, sl]], axis=0).astype(BF16), kst[i])
              for i, (ui, sl) in enumerate(ch)]
        x = [-jnp.where(m_strict[:, sl], kq[i][:C] * decs[ui][:, sl], 0.0) for i, (ui, sl) in enumerate(ch)]
        xm = [pm(xi, xi) for xi in x]
        p = [eye[:, sl] + x[i] for i, (ui, sl) in enumerate(ch)]
        for _ in range(int(math.log2(C)) - 2):
            r = [pm(jnp.concatenate([pi, xi], axis=0), xi) for pi, xi in zip(p, xm)]
            p = [pi + ri[:C] for pi, ri in zip(p, r)]
            xm = [ri[C:] for ri in r]
        p = [pi + pm(pi, xi) for pi, xi in zip(p, xm)]
        u = [pm(p[i], (v_ref[0, rws[ui], sl] * bs[ui][:, sl])) for i, (ui, sl) in enumerate(ch)]
        w = [pm(p[i], kbs[ui][:, sl] * egs[ui][:, sl]) for i, (ui, sl) in enumerate(ch)]
        qk = [jnp.where(m_incl[:, sl], kq[i][C:] * decs[ui][:, sl], 0.0).astype(BF16)
              for i, (ui, sl) in enumerate(ch)]
        kuw = []
        for i, (ui, sl) in enumerate(ch):
            g = gs[ui][:, sl]
            kd = (ks[ui][:, sl] * jnp.exp(g[C - 1:C, :] - g)).astype(BF16)
            uw = jnp.concatenate([u[i], w[i]], axis=1).astype(BF16)
            kuw.append(_dot_tn(kd, uw))
        ou = [_dot(qk[i], _rowstack(u[i], hms).astype(BF16)) for i in range(len(ch))]
        qw_ = [_dot(qk[i], _rowstack(w[i], hms).astype(BF16)) for i in range(len(ch))]
        for i, (ui, sl) in enumerate(ch):
            krows = pl.ds(pl.multiple_of(cs[ui] * HEAD_DIM, HEAD_DIM), HEAD_DIM)
            NN[krows, sl] = diag_blocks(kuw[i][:, :qw])
            KW[krows, sl] = diag_blocks(kuw[i][:, qw:])
            OU[rws[ui], sl] = ou[i]
            QP[rws[ui], sl] = qs[ui][:, sl] * egs[ui][:, sl] - qw_[i]
        return carry

    lax.fori_loop(0, tt // C // unroll, intra, 0)

    def inter(c, carry):
        rows = pl.ds(pl.multiple_of(c * C, C), C)
        krows = pl.ds(pl.multiple_of(c * HEAD_DIM, HEAD_DIM), HEAD_DIM)
        al = AL[pl.ds(pl.multiple_of(c * 8, 8), 1), :]
        ss = [S[:, sl] for sl in sls]
        lhs = [jnp.concatenate([KW[krows, sl], QP[rows, sl]], axis=0).astype(BF16) for sl in sls]
        rs = [_dot(l, _rowstack(s, hms).astype(BF16)) for l, s in zip(lhs, ss)]
        for sl, s, r in zip(sls, ss, rs):
            S[:, sl] = s * al[:, sl] - r[:HEAD_DIM] + NN[krows, sl]
            o_ref[0, rows, sl] = r[HEAD_DIM:] + OU[rows, sl]
        return carry

    lax.fori_loop(0, tt // C, inter, 0)
    s_out_ref[0] = S[...]


def _gdn_prompt(q, k, v, la, bt, *, tt, unroll):
    n, t, hw = q.shape
    C = GDN_CHUNK
    ltri = jnp.tril(jnp.ones((C, C), F32)).astype(BF16)
    ones = jnp.ones((C, C), BF16)
    row = pl.BlockSpec((1, tt, hw), lambda i, j: (i, j, 0))
    cst = pl.BlockSpec((C, C), lambda i, j: (0, 0))
    nch = tt // C
    return pl.pallas_call(
        functools.partial(_gdn_kernel, tt=tt, hw=hw, unroll=unroll),
        grid=(n, t // tt),
        in_specs=[row] * 5 + [cst, cst],
        out_specs=[row, pl.BlockSpec((1, HEAD_DIM, hw), lambda i, j: (i, 0, 0))],
        out_shape=[jax.ShapeDtypeStruct((n, t, hw), F32), jax.ShapeDtypeStruct((n, HEAD_DIM, hw), F32)],
        scratch_shapes=[pltpu.VMEM((HEAD_DIM, hw), F32), pltpu.VMEM((nch * HEAD_DIM, hw), F32),
                        pltpu.VMEM((nch * HEAD_DIM, hw), F32), pltpu.VMEM((tt, hw), F32),
                        pltpu.VMEM((tt, hw), F32), pltpu.VMEM((nch * 8, hw), F32)],
        compiler_params=_cparams(("arbitrary", "arbitrary")),
        name="gdn_prompt",
    )(q, k, v, la, bt, ltri, ones)


def _gdn_step_kernel(q_ref, k_ref, v_ref, la_ref, bt_ref, s_ref, bones_ref, o_ref, s_out_ref, *, gs, hw):
    rowi = lax.broadcasted_iota(jnp.int32, (HEAD_DIM, hw), 0)
    colj = _lane_in_head(lax.broadcasted_iota(jnp.int32, (HEAD_DIM, hw), 1))
    eye = (colj == rowi).astype(F32)
    bones = bones_ref[...]
    for i in range(gs):
        r = slice(i, i + 1)
        q = q_ref[r, :] * (HEAD_DIM ** -0.5)
        k = k_ref[r, :]
        v = v_ref[r, :]
        eg = jnp.exp(la_ref[r, :])
        b = bt_ref[r, :]
        s = s_ref[i]
        kbc = _dot_f32_lhs(eye * k, bones)
        qbc = _dot_f32_lhs(eye * q, bones)
        ks = jnp.sum(kbc * s, axis=0, keepdims=True)
        qs = jnp.sum(qbc * s, axis=0, keepdims=True)
        qk = jnp.sum(kbc * qbc, axis=0, keepdims=True)
        vn = b * v - b * eg * ks
        o_ref[r, :] = eg * qs + qk * vn
        s_out_ref[i] = s * eg + kbc * vn


def _gdn_sample(q, k, v, la, bt, s_packed, bones, *, gs):
    m, hw = q.shape
    row = pl.BlockSpec((gs, hw), lambda i: (i, 0))
    st = pl.BlockSpec((gs, HEAD_DIM, hw), lambda i: (i, 0, 0))
    return pl.pallas_call(
        functools.partial(_gdn_step_kernel, gs=gs, hw=hw),
        grid=(m // gs,),
        in_specs=[row] * 5 + [st, pl.BlockSpec(bones.shape, lambda i: (0, 0))],
        out_specs=[row, st],
        out_shape=[jax.ShapeDtypeStruct((m, hw), F32), jax.ShapeDtypeStruct(s_packed.shape, F32)],
        compiler_params=_cparams(("arbitrary",)),
        name="gdn_sample",
    )(q, k, v, la, bt, s_packed, bones)


def _band_kernel(q_ref, k_ref, v_ref, o_ref, kf, vf, oc, lc, *, tt, unroll):
    j = pl.program_id(2)

    @pl.when(j == 0)
    def _():
        kf[0:tt, :] = jnp.zeros((tt, LANES), F32)
        vf[0:tt, :] = jnp.zeros((tt, LANES), F32)

    kf[tt:, :] = k_ref[0]
    vf[tt:, :] = v_ref[0]
    a = lax.bitwise_and(lax.broadcasted_iota(jnp.int32, (2 * BAND, 2 * BAND), 0), BAND - 1)
    c = lax.broadcasted_iota(jnp.int32, (2 * BAND, 2 * BAND), 1)
    band = (c >= a) & (c <= a + BAND)
    bias = jnp.where(band, 0.0, NEG)
    bias_first = jnp.where(band & (c >= BAND), 0.0, NEG)
    row_h1 = lax.broadcasted_iota(jnp.int32, (2 * BAND, LANES), 0) >= BAND
    lane_h1 = lax.broadcasted_iota(jnp.int32, (2 * BAND, LANES), 1) >= HEAD_DIM
    own = row_h1 == lane_h1
    out_h1 = lax.broadcasted_iota(jnp.int32, (BAND, LANES), 1) >= HEAD_DIM
    ones_kv = jnp.ones((2 * BAND, LANES), BF16)

    for ci, (_, d) in enumerate(DILATIONS):
        nblk = tt // (BAND * d)

        def body(it, carry, d=d, ci=ci, nblk=nblk):
            rows, krows, firsts = [], [], []
            for uu in range(unroll):
                idx = it * unroll + uu
                r = idx // nblk
                b = idx % nblk
                start = b * (BAND * d) + r
                if d == 1:
                    rows.append(pl.ds(start, BAND))
                    krows.append(pl.ds(tt + start - BAND, 2 * BAND))
                else:
                    rows.append(pl.ds(start, BAND, stride=d))
                    krows.append(pl.ds(tt + start - BAND * d, 2 * BAND, stride=d))
                firsts.append(jnp.logical_and(j == 0, b == 0))
            ss = []
            for rw, kr in zip(rows, krows):
                qb = q_ref[0, rw, :] * (HEAD_DIM ** -0.5)
                qs = jnp.where(own, jnp.concatenate([qb, qb], axis=0), 0.0).astype(BF16)
                ss.append(_dot_nt(qs, kf[kr, :].astype(BF16)))
            ps, ms = [], []
            for s, first in zip(ss, firsts):
                s = s + jnp.where(first, bias_first, bias)
                m = jnp.max(s, axis=-1, keepdims=True)
                ps.append(jnp.exp((s - m).astype(BF16)))
                ms.append(m)
            pvl = [_dot(p, jnp.concatenate([vf[kr, :].astype(BF16), ones_kv], axis=1)) for p, kr in zip(ps, krows)]
            for rw, r, m in zip(rows, pvl, ms):
                l = r[:, LANES:]
                on = r[:, :LANES] / l
                lse = m + jnp.log(l)
                oc[ci, rw, :] = jnp.where(out_h1, on[BAND:], on[:BAND])
                lc[ci, rw, :] = jnp.where(out_h1, lse[BAND:], lse[:BAND])
            return carry

        lax.fori_loop(0, tt // BAND // unroll, body, 0)

    mrows = 256
    for ch in range(tt // mrows):
        rs = slice(ch * mrows, (ch + 1) * mrows)
        ls = [lc[ci, rs, :] for ci in range(len(DILATIONS))]
        mx = functools.reduce(jnp.maximum, ls)
        es = [jnp.exp(l - mx) for l in ls]
        o_ref[0, rs, :] = sum(e * oc[ci, rs, :] for ci, e in enumerate(es)) / sum(es)

    kf[0:tt, :] = kf[tt:, :]
    vf[0:tt, :] = vf[tt:, :]


def _band(q, k, v, *, tt, unroll):
    n, t, hw = q.shape
    assert all(w // d == BAND and tt % w == 0 for w, d in DILATIONS) and t % tt == 0
    blk = pl.BlockSpec((1, tt, LANES), lambda i, c, j: (i, j, c))
    nd = len(DILATIONS)
    return pl.pallas_call(
        functools.partial(_band_kernel, tt=tt, unroll=unroll),
        grid=(n, hw // LANES, t // tt),
        in_specs=[blk, blk, blk],
        out_specs=blk,
        out_shape=jax.ShapeDtypeStruct((n, t, hw), F32),
        scratch_shapes=[pltpu.VMEM((2 * tt, LANES), F32), pltpu.VMEM((2 * tt, LANES), F32),
                        pltpu.VMEM((nd, tt, LANES), F32), pltpu.VMEM((nd, tt, LANES), F32)],
        compiler_params=_cparams(("arbitrary", "arbitrary", "arbitrary")),
        name="band",
    )(q, k, v)


def _cache_attn_kernel(q_ref, kn_ref, vn_ref, k_ref, v_ref, o_ref, *, nh, n_past):
    pos = lax.broadcasted_iota(jnp.int32, (1, n_past), 1)
    cnt = jnp.zeros((1, n_past), F32)
    for window, dil in DILATIONS:
        hit = (pos >= n_past - window) & (lax.bitwise_and(pos, dil - 1) == 0)
        cnt = cnt + jnp.where(hit, 1.0, 0.0)
    live = cnt > 0.0
    nd = float(len(DILATIONS))
    lane_h = lax.broadcasted_iota(jnp.int32, (HEAD_DIM, nh), 1)
    out = jnp.zeros((HEAD_DIM, nh), F32)
    for h in range(nh):
        q = q_ref[0, :, h:h + 1] * (HEAD_DIM ** -0.5)
        s = jnp.where(live, jnp.sum(k_ref[0, h] * q, axis=0, keepdims=True), NEG)
        s0 = jnp.sum(q * kn_ref[0, :, h:h + 1], axis=0, keepdims=True)
        m = jnp.maximum(jnp.max(s, axis=-1, keepdims=True), s0)
        p = cnt * jnp.exp(s - m)
        p0 = nd * jnp.exp(s0 - m)
        den = p0 + jnp.sum(p, axis=-1, keepdims=True)
        num = p0 * vn_ref[0, :, h:h + 1] + jnp.sum(v_ref[0, h] * p, axis=-1, keepdims=True)
        out = jnp.where(lane_h == h, num / den, out)
    o_ref[0] = out


def _cache_attn(q_t, kn_t, vn_t, ck_t, cv_t):
    m, hd, nh = q_t.shape
    n_past = ck_t.shape[3]
    assert all(w // d == BAND and n_past >= w and d & (d - 1) == 0 and n_past % d == 0 for w, d in DILATIONS)
    col = pl.BlockSpec((1, hd, nh), lambda i: (i, 0, 0))
    cache = pl.BlockSpec((1, nh, hd, n_past), lambda i: (i, 0, 0, 0))
    return pl.pallas_call(
        functools.partial(_cache_attn_kernel, nh=nh, n_past=n_past),
        grid=(m,),
        in_specs=[col, col, col, cache, cache],
        out_specs=col,
        out_shape=jax.ShapeDtypeStruct((m, hd, nh), F32),
        compiler_params=_cparams(("arbitrary",)),
        name="cache_attn",
    )(q_t, kn_t, vn_t, ck_t, cv_t)


def _post_kernel(*refs, tm, precise):
    if precise:
        (x_ref, og_ref, z_ref, ob_ref, ga_ref, gb_ref, bones_ref, wo_ref, gf_ref, wrh_ref, wrl_ref, br_ref,
         wol_ref, h_ref, route_ref, rt_ref, cnt_ref, cnt) = refs
    else:
        (x_ref, og_ref, z_ref, ob_ref, ga_ref, gb_ref, bones_ref, wo_ref, gf_ref, wrh_ref, wrl_ref, br_ref,
         h_ref, route_ref, rt_ref, cnt_ref, cnt) = refs
    i = pl.program_id(0)

    @pl.when(i == 0)
    def _():
        cnt[...] = jnp.zeros_like(cnt)

    bones = bones_ref[...]
    oa = _head_rms(og_ref[...], bones, ga_ref[...], precise) * _silu(z_ref[...])
    ob = _head_rms(ob_ref[...], bones, gb_ref[...], precise)
    mixf = jnp.concatenate([oa, ob], axis=-1)
    mix = mixf.astype(BF16)
    proj = _dot(mix, wo_ref[...])
    if precise:
        proj = proj + _dot((mixf - mix.astype(F32)).astype(BF16), wo_ref[...]) + _dot(mix, wol_ref[...])
    h = x_ref[...] + proj
    h_ref[...] = h

    mrow = _rms(h, gf_ref[...])
    mh = mrow.astype(BF16)
    ml = (mrow - mh.astype(F32)).astype(BF16)
    logit = _dot(mh, wrh_ref[...]) + _dot(mh, wrl_ref[...]) + _dot(ml, wrh_ref[...]) + br_ref[...]
    lane = lax.broadcasted_iota(jnp.int32, logit.shape, 1).astype(F32)
    gl = jnp.where(lane < N_GROUPS, logit, NEG)
    gmax = jnp.max(gl, axis=-1, keepdims=True)
    grp = jnp.min(jnp.where(gl == gmax, lane, 1e9), axis=-1, keepdims=True)
    pg = 1.0 / jnp.sum(jnp.exp(gl - gmax), axis=-1, keepdims=True)
    lo = N_GROUPS + grp * EXPERTS_PER_GROUP
    el = jnp.where((lane >= lo) & (lane < lo + EXPERTS_PER_GROUP), logit, NEG)
    v1 = jnp.max(el, axis=-1, keepdims=True)
    i1 = jnp.min(jnp.where(el == v1, lane, 1e9), axis=-1, keepdims=True)
    el2 = jnp.where(lane == i1, NEG, el)
    v2 = jnp.max(el2, axis=-1, keepdims=True)
    i2 = jnp.min(jnp.where(el2 == v2, lane, 1e9), axis=-1, keepdims=True)
    e = jnp.exp(v2 - v1)
    g1 = pg / (1.0 + e)
    g2 = pg * e / (1.0 + e)
    e1 = i1 - N_GROUPS
    e2 = i2 - N_GROUPS

    oh1 = lane == e1
    oh2 = lane == e2
    onehot = jnp.where(oh1 | oh2, 1.0, 0.0)
    ri = lax.broadcasted_iota(jnp.int32, (tm, tm), 0)
    ci = lax.broadcasted_iota(jnp.int32, (tm, tm), 1)
    tri = jnp.where(ci < ri, 1.0, 0.0).astype(BF16)
    before = _dot(tri, onehot.astype(BF16)) + cnt[...]
    r1 = jnp.sum(jnp.where(oh1, before, 0.0), axis=-1, keepdims=True)
    r2 = jnp.sum(jnp.where(oh2, before, 0.0), axis=-1, keepdims=True)
    cnt[...] = cnt[...] + jnp.sum(onehot, axis=0, keepdims=True)
    cnt_ref[...] = cnt[...]
    route = jnp.zeros_like(logit)
    for j, val in enumerate((e1, e2, r1, r2, g1, g2)):
        route = jnp.where(lane == j, val, route)
    route_ref[...] = route
    rt_ref[...] = jnp.transpose(route)[:ROW_TILE]


def _post(x2, og, z, ob, ga, gb, bones, wo, gf, wrh, wrl, br, wo_lo, *, tm):
    m, d = x2.shape
    hw = og.shape[1]
    row = lambda w: pl.BlockSpec((tm, w), lambda i: (i, 0))
    const = lambda a: pl.BlockSpec(a.shape, lambda i: (0,) * a.ndim)
    consts = [ga, gb, bones, wo, gf, wrh, wrl, br] + ([] if wo_lo is None else [wo_lo])
    return pl.pallas_call(
        functools.partial(_post_kernel, tm=tm, precise=wo_lo is not None),
        grid=(m // tm,),
        in_specs=[row(d), row(hw), row(hw), row(hw)] + [const(a) for a in consts],
        out_specs=[row(d), row(LANES), pl.BlockSpec((ROW_TILE, tm), lambda i: (0, i)),
                   pl.BlockSpec((1, LANES), lambda i: (0, 0))],
        out_shape=[jax.ShapeDtypeStruct((m, d), F32), jax.ShapeDtypeStruct((m, LANES), F32),
                   jax.ShapeDtypeStruct((ROW_TILE, m), F32), jax.ShapeDtypeStruct((1, LANES), F32)],
        scratch_shapes=[pltpu.VMEM((1, LANES), F32)],
        compiler_params=_cparams(("arbitrary",)),
        name=f"post_{m}",
    )(x2, og, z, ob, *consts)


ROW_TILE = 8


def _rows_from_tiles(ref, n):
    return jnp.concatenate([ref[pl.ds(c, n, stride=ROW_TILE), :] for c in range(ROW_TILE)], axis=1)


def _rows_to_tiles(ref, rows):
    n = rows.shape[0]
    for c in range(ROW_TILE):
        ref[pl.ds(c, n, stride=ROW_TILE), :] = rows[:, c * LANES:(c + 1) * LANES]


def _dispatch_kernel(pstart_ref, pend_ref, h_ref, gf_ref, dest_ref, xbuf_ref, mrow, zbuf, dsm, sem, dsem, zsem,
                     *, tm, bm, nsteps, n_blk):
    i = pl.program_id(0)
    slot = i % 2
    dcp = pltpu.make_async_copy(dest_ref.at[i], dsm, dsem)
    dcp.start()

    def zero_copy(e):
        start = pl.multiple_of((pend_ref[e] - bm) * ROW_TILE, bm * ROW_TILE)
        return pltpu.make_async_copy(zbuf, xbuf_ref.at[pl.ds(start, bm * ROW_TILE)], zsem)

    def zero_block(b):
        start = pl.multiple_of(b * (bm * ROW_TILE), bm * ROW_TILE)
        return pltpu.make_async_copy(zbuf, xbuf_ref.at[pl.ds(start, bm * ROW_TILE)], zsem)

    @pl.when(i == 0)
    def _():
        zbuf[...] = jnp.zeros_like(zbuf)
        n_used = pend_ref[N_EXPERTS - 1] // bm
        for start_or_wait in (True, False):
            for e in range(N_EXPERTS):
                @pl.when(pend_ref[e] > pstart_ref[e])
                def _(e=e, start_or_wait=start_or_wait):
                    zero_copy(e).start() if start_or_wait else zero_copy(e).wait()

            def tail(b, carry, start_or_wait=start_or_wait):
                zero_block(b).start() if start_or_wait else zero_block(b).wait()
                return carry

            lax.fori_loop(n_used, n_blk, tail, 0)

    def wait_rows(s):
        for _ in range(2):
            pltpu.make_async_copy(mrow.at[s], xbuf_ref.at[pl.ds(0, tm * ROW_TILE)], sem.at[s]).wait()

    @pl.when(i >= 2)
    def _():
        wait_rows(slot)

    m = _rms(h_ref[...], gf_ref[...])
    dcp.wait()

    for s in range(2):
        @pl.when(slot == s)
        def _(s=s):
            _rows_to_tiles(mrow.at[s], m)

            def body(r, carry):
                src = mrow.at[s, pl.ds(pl.multiple_of(r * ROW_TILE, ROW_TILE), ROW_TILE)]
                for kk in range(2):
                    d = pl.multiple_of(dsm[kk * tm + r] * ROW_TILE, ROW_TILE)
                    pltpu.make_async_copy(src, xbuf_ref.at[pl.ds(d, ROW_TILE)], sem.at[s]).start()
                return carry

            lax.fori_loop(0, tm, body, 0, unroll=8)

    @pl.when(i == nsteps - 1)
    def _():
        wait_rows(slot)
        if nsteps >= 2:
            wait_rows(1 - slot)


def _dispatch(h, gf, dest, pstart, pend, n_slots, *, tm, bm):
    m, d = h.shape
    assert d == ROW_TILE * LANES
    nsteps = m // tm
    grid_spec = pltpu.PrefetchScalarGridSpec(
        num_scalar_prefetch=2,
        grid=(nsteps,),
        in_specs=[pl.BlockSpec((tm, d), lambda i, ps, pe: (i, 0)), pl.BlockSpec((1, d), lambda i, ps, pe: (0, 0)),
                  pl.BlockSpec(memory_space=pl.ANY)],
        out_specs=pl.BlockSpec(memory_space=pl.ANY),
        scratch_shapes=[pltpu.VMEM((2, tm * ROW_TILE, LANES), F32), pltpu.VMEM((bm * ROW_TILE, LANES), F32),
                        pltpu.SMEM((2 * tm,), jnp.int32),
                        pltpu.SemaphoreType.DMA((2,)), pltpu.SemaphoreType.DMA, pltpu.SemaphoreType.DMA],
    )
    return pl.pallas_call(
        functools.partial(_dispatch_kernel, tm=tm, bm=bm, nsteps=nsteps, n_blk=n_slots // bm),
        grid_spec=grid_spec,
        out_shape=jax.ShapeDtypeStruct((n_slots * ROW_TILE, LANES), F32),
        compiler_params=_cparams(("arbitrary",), row_dma=True),
        name=f"dispatch_{m}",
    )(pstart, pend, h, gf, dest)


def _expert_kernel(be_ref, nu_ref, x_ref, wg_ref, wu_ref, wd_ref, y_ref, *, bm):
    b = pl.program_id(0)

    @pl.when(b < nu_ref[0])
    def _():
        x = _rows_from_tiles(x_ref, bm).astype(BF16)
        hid = _silu(_dot(x, wg_ref[0])) * _dot(x, wu_ref[0])
        _rows_to_tiles(y_ref, _dot(hid.astype(BF16), wd_ref[0]))

    @pl.when(b >= nu_ref[0])
    def _():
        y_ref[...] = jnp.zeros_like(y_ref)


def _experts(xbuf, blk_expert, n_used, wg, wu, wd, *, bm):
    d, de = wg.shape[1], wg.shape[2]
    n_slots = xbuf.shape[0] // ROW_TILE
    blk = (bm * ROW_TILE, LANES)
    grid_spec = pltpu.PrefetchScalarGridSpec(
        num_scalar_prefetch=2,
        grid=(n_slots // bm,),
        in_specs=[pl.BlockSpec(blk, lambda b, be, nu: (jnp.minimum(b, nu[0] - 1), 0)),
                  pl.BlockSpec((1, d, de), lambda b, be, nu: (be[b], 0, 0)),
                  pl.BlockSpec((1, d, de), lambda b, be, nu: (be[b], 0, 0)),
                  pl.BlockSpec((1, de, d), lambda b, be, nu: (be[b], 0, 0))],
        out_specs=pl.BlockSpec(blk, lambda b, be, nu: (b, 0)),
    )
    return pl.pallas_call(
        functools.partial(_expert_kernel, bm=bm),
        grid_spec=grid_spec,
        out_shape=jax.ShapeDtypeStruct(xbuf.shape, F32),
        compiler_params=_cparams(("arbitrary",)),
        name=f"experts_{n_slots}",
    )(blk_expert, n_used, xbuf, wg, wu, wd)


def _combine_kernel(h_ref, route_ref, p_ref, wpg_ref, wpp_ref, gfin_ref, dest_ref, ybuf_ref, y_ref,
                    ys, dsm0, dsm1, sem, dsem, *, tm, nsteps):
    i = pl.program_id(0)
    slot = i % 2

    def issue(j, s):
        dsm = (dsm0, dsm1)[s]
        cp = pltpu.make_async_copy(dest_ref.at[j], dsm, dsem)
        cp.start()
        cp.wait()

        def body(r, carry):
            for kk in range(2):
                d = pl.multiple_of(dsm[kk * tm + r] * ROW_TILE, ROW_TILE)
                dst = ys.at[s, kk, pl.ds(pl.multiple_of(r * ROW_TILE, ROW_TILE), ROW_TILE)]
                pltpu.make_async_copy(ybuf_ref.at[pl.ds(d, ROW_TILE)], dst, sem.at[s]).start()
            return carry

        lax.fori_loop(0, tm, body, 0, unroll=8)

    @pl.when(i == 0)
    def _():
        issue(0, 0)

    for s in range(2):
        @pl.when(jnp.logical_and(i + 1 < nsteps, slot == 1 - s))
        def _(s=s):
            issue(i + 1, s)

    for kk in range(2):
        pltpu.make_async_copy(ybuf_ref.at[pl.ds(0, tm * ROW_TILE)], ys.at[slot, kk], sem.at[slot]).wait()

    route = route_ref[...]
    g1 = route[:, 4:5]
    g2 = route[:, 5:6]
    h = h_ref[...] + (g1 * _rows_from_tiles(ys.at[slot, 0], tm) + g2 * _rows_from_tiles(ys.at[slot, 1], tm))
    gate = _sigmoid(_dot(h.astype(BF16), wpg_ref[...]))
    out = h + gate * _dot(p_ref[...].astype(BF16), wpp_ref[...])
    y_ref[...] = _rms(out, gfin_ref[...])


def _combine(h, route, p, wpg, wpp, gfin, dest, ybuf, *, tm):
    m, d = h.shape
    row = lambda w: pl.BlockSpec((tm, w), lambda i: (i, 0))
    const = lambda a: pl.BlockSpec(a.shape, lambda i: (0,) * a.ndim)
    return pl.pallas_call(
        functools.partial(_combine_kernel, tm=tm, nsteps=m // tm),
        grid=(m // tm,),
        in_specs=[row(d), row(LANES), row(p.shape[1]), const(wpg), const(wpp), const(gfin),
                  pl.BlockSpec(memory_space=pl.ANY), pl.BlockSpec(memory_space=pl.ANY)],
        out_specs=row(d),
        out_shape=jax.ShapeDtypeStruct((m, d), F32),
        scratch_shapes=[pltpu.VMEM((2, 2, tm * ROW_TILE, LANES), F32), pltpu.SMEM((2 * tm,), jnp.int32),
                        pltpu.SMEM((2 * tm,), jnp.int32), pltpu.SemaphoreType.DMA((2,)), pltpu.SemaphoreType.DMA],
        compiler_params=_cparams(("arbitrary",), row_dma=True),
        name=f"combine_{m}",
    )(h, route, p, wpg, wpp, gfin, dest, ybuf)


def _tail(x2, og, z, ob, p2, wts, *, tm, bm, precise):
    m, d = x2.shape
    h, route, route_t, counts = _post(x2, og, z, ob, wts["ga"], wts["gb"], wts["bones"],
                                      wts["wo_hi"] if precise else wts["wo"], wts["gf"], wts["wrh"], wts["wrl"],
                                      wts["br"], wts["wo_lo"] if precise else None, tm=tm)
    eid = route_t[0:2].astype(jnp.int32)
    rank = route_t[2:4].astype(jnp.int32)
    sizes = counts[0, :N_EXPERTS].astype(jnp.int32)
    padded = (sizes + bm - 1) // bm * bm
    pend = jnp.cumsum(padded)
    pstart = pend - padded
    first_slot = jnp.sum(jnp.where(eid[..., None] == jnp.arange(N_EXPERTS, dtype=jnp.int32), pstart, 0), axis=-1)
    dest = (first_slot + rank).reshape(2, m // tm, tm).transpose(1, 0, 2).reshape(m // tm, 2 * tm)
    n_blk = (2 * m) // bm + N_EXPERTS
    blk_start = jnp.arange(n_blk, dtype=jnp.int32) * bm
    blk_expert = jnp.minimum(jnp.sum(pend[None, :] <= blk_start[:, None], axis=1), N_EXPERTS - 1).astype(jnp.int32)
    n_used = (pend[-1:] // bm).astype(jnp.int32)
    xbuf = _dispatch(h, wts["gf"], dest, pstart.astype(jnp.int32), pend.astype(jnp.int32), n_blk * bm,
                     tm=tm, bm=bm)
    ybuf = _experts(xbuf, blk_expert, n_used, wts["wg"], wts["wu"], wts["wd"], bm=bm)
    return _combine(h, route, p2, wts["wpg"], wts["wpp"], wts["gfin"], dest, ybuf, tm=tm)


def _split_hi_lo(w):
    bits = lax.bitcast_convert_type(w.astype(F32), jnp.uint32)
    hi32 = lax.bitcast_convert_type(bits & jnp.uint32(0xFFFF0000), F32)
    return hi32.astype(BF16), (w - hi32).astype(BF16)


def _rope_tables(pos):
    half = ROT_DIM // 2
    inv = ROPE_THETA ** (-jnp.arange(half, dtype=F32) * (2.0 / ROT_DIM))
    ang = pos.astype(F32)[:, None] * inv[None, :]
    cos, sin = jnp.cos(ang), jnp.sin(ang)
    j = jnp.arange(LANES) % HEAD_DIM
    first, second = j < half, (j >= half) & (j < ROT_DIM)
    jj = jnp.where(second, j - half, jnp.where(first, j, 0))
    c = jnp.where((first | second)[None, :], cos[:, jj], 1.0)
    s1 = jnp.where(first[None, :], -sin[:, jj], 0.0)
    s2 = jnp.where(second[None, :], sin[:, jj], 0.0)
    return c, s1, s2


def kernel(x_prompt, x_sample, cache_win_k, cache_win_v, state_conv, state_delta, p_prompt, p_sample, g_attn_norm, w_in, w_conv, a_log, dt_bias, g_a_out, g_b_out, w_out, g_ffn_norm, w_router_group, b_router_group, w_router_expert, b_router_expert, w_exp_gate, w_exp_up, w_exp_down, w_ple_gate, w_ple_proj, g_final):
    n, t, d = x_prompt.shape
    ns = x_sample.shape[0]
    assert w_in.shape[0] == 1 and x_sample.shape[1] == 1
    ha = a_log.shape[1]
    aw = ha * HEAD_DIM
    off_a = 4 * aw
    off_win = off_a + 2 * ha
    keep =min(DILATIONS[-1][0], t)
    hi = lambda a: a.astype(F32)

    w = w_in[0]
    w_ab = jnp.pad(w[:, off_a:off_win], ((0, 0), (0, LANES - 2 * ha)))
    w_cat32 = jnp.concatenate([w[:, :off_a], w[:, off_win:], w_ab], axis=1)
    w_cat = w_cat32.astype(BF16)
    w_cat_hi, w_cat_lo = _split_hi_lo(w_cat32)
    pad_l = lambda v: jnp.pad(hi(v), (0, LANES - v.shape[0]))[None, :]
    alog = pad_l(a_log[0])
    dtb = pad_l(dt_bias[0])
    lane_head = jnp.arange(2 * aw) // HEAD_DIM
    esel = (jnp.arange(LANES)[:, None] == lane_head[None, :]).astype(BF16)
    hd = jnp.arange(aw) // HEAD_DIM
    bones = (hd[:, None] == hd[None, :]).astype(BF16)
    wr = jnp.concatenate([hi(w_router_group[0]), hi(w_router_expert[0]).reshape(d, N_EXPERTS)], axis=1)
    wr = jnp.pad(wr, ((0, 0), (0, LANES - wr.shape[1])))
    wrh, wrl = _split_hi_lo(wr)
    wo_hi, wo_lo = _split_hi_lo(w_out[0])
    br = jnp.concatenate([hi(b_router_group[0]), hi(b_router_expert[0]).reshape(N_EXPERTS)])
    wts = dict(
        ga=jnp.tile(hi(g_a_out[0]), ha)[None, :], gb=jnp.tile(hi(g_b_out[0]), ha)[None, :], bones=bones,
        wo=w_out[0].astype(BF16), wo_hi=wo_hi, wo_lo=wo_lo,
        gf=hi(g_ffn_norm[0])[None, :], wrh=wrh, wrl=wrl, br=pad_l(br),
        wg=w_exp_gate[0].astype(BF16), wu=w_exp_up[0].astype(BF16), wd=w_exp_down[0].astype(BF16),
        wpg=w_ple_gate[0].astype(BF16), wpp=w_ple_proj[0].astype(BF16), gfin=hi(g_final)[None, :])
    g_attn = hi(g_attn_norm[0])[None, :]
    wconv = hi(w_conv[0])

    tm_p = min(256, t)
    cos, s1, s2 = _rope_tables(jnp.arange(t, dtype=jnp.int32))
    qa, ka, va, z, la, bt, qb, kb, vb, conv_p = _proj(
        x_prompt, g_attn, w_cat, wconv, alog, dtb, esel, cos, s1, s2, bones, None, tm=tm_p, sample=False)
    og, s_fin = _gdn_prompt(qa, ka, va, la, bt, tt=min(512, t), unroll=4)
    ob = _band(qb, kb, vb, tt=DILATIONS[-1][0], unroll=8)
    flat = lambda a: a.reshape(n * t, a.shape[-1])
    y_prompt = _tail(flat(x_prompt), flat(og), flat(z), flat(ob), flat(p_prompt[0]), wts,
                     tm=min(512, t), bm=256, precise=False).reshape(n, t, d)

    cos, s1, s2 = _rope_tables(jnp.full((1,), PAST_LEN, jnp.int32))
    st = [state_conv[0][None, :, j, :] for j in range(state_conv.shape[2])]
    xs3 = x_sample.reshape(1, ns, d)
    qa_s, ka_s, va_s, z_s, la_s, bt_s, qb_s, kb_s, vb_s, ua_s = [
        a[0] for a in _proj(xs3, g_attn, w_cat_hi, wconv, alog, dtb, esel, cos, s1, s2, bones, [w_cat_lo] + st,
                            tm=ns, sample=True)]
    s_packed = state_delta[0].transpose(0, 2, 1, 3).reshape(ns, HEAD_DIM, aw)
    og_s, s_new = _gdn_sample(qa_s, ka_s, va_s, la_s, bt_s, s_packed, bones, gs=8)
    cols = lambda a: a.reshape(ns, ha, HEAD_DIM).transpose(0, 2, 1)
    pos_minor = lambda c: c[0].transpose(0, 2, 3, 1)
    ob_s = _cache_attn(cols(qb_s), cols(kb_s), cols(vb_s), pos_minor(cache_win_k), pos_minor(cache_win_v))
    ob_s = ob_s.transpose(0, 2, 1).reshape(ns, aw)
    y_sample = _tail(x_sample.reshape(ns, d), og_s, z_s, ob_s, p_sample[0].reshape(ns, -1), wts,
                     tm=ns, bm=128, precise=True).reshape(ns, 1, d)

    unpack = lambda s: s.reshape(-1, HEAD_DIM, ha, HEAD_DIM).transpose(0, 2, 1, 3)[None]
    heads = lambda a: a.reshape(a.shape[0], -1, ha, HEAD_DIM)
    return (y_prompt, y_sample,
            heads(kb[:, t - keep:])[None], heads(vb[:, t - keep:])[None],
            conv_p[:, 8 - state_conv.shape[2]:][None], unpack(s_fin),
            heads(kb_s[:, None])[None], heads(vb_s[:, None])[None],
            jnp.concatenate([state_conv[0][:, 1:], ua_s[:, None]], axis=1)[None], unpack(s_new))
```

```python
import functools
import math

import jax
import jax.numpy as jnp
from jax import lax
from jax.experimental import pallas as pl
from jax.experimental.pallas import tpu as pltpu

F32 = jnp.float32
BF16 = jnp.bfloat16

HEAD_DIM = 64
GDN_CHUNK = 64
ROT_DIM = HEAD_DIM // 4
ROPE_THETA = 500000.0
PAST_LEN = 8192
DILATIONS = ((128, 1), (512, 4), (2048, 16))
BAND = 128
N_GROUPS = 4
EXPERTS_PER_GROUP = 8
N_EXPERTS = N_GROUPS * EXPERTS_PER_GROUP
NORM_EPS = 1e-6
NEG = -1e30
LANES = 128
VMEM_LIMIT = 56 * 1024 * 1024


def _cparams(sem, row_dma=False):
    return pltpu.CompilerParams(dimension_semantics=sem, vmem_limit_bytes=VMEM_LIMIT,
                                disable_bounds_checks=row_dma)


def _dot(a, b):
    return jnp.dot(a, b, preferred_element_type=F32)


def _dot_nt(a, b):
    return lax.dot_general(a, b, (((1,), (1,)), ((), ())), preferred_element_type=F32)


def _dot_tn(a, b):
    return lax.dot_general(a, b, (((0,), (0,)), ((), ())), preferred_element_type=F32)


def _split3(x):
    hi = x.astype(BF16)
    r = x - hi.astype(F32)
    mid = r.astype(BF16)
    lo = (r - mid.astype(F32)).astype(BF16)
    return hi, mid, lo


def _dot_f32_lhs(x, w_bf16):
    hi, mid, lo = _split3(x)
    return _dot(hi, w_bf16) + _dot(mid, w_bf16) + _dot(lo, w_bf16)


def _dot_f32_rhs(w_bf16, x):
    hi, mid, lo = _split3(x)
    return _dot(w_bf16, hi) + _dot(w_bf16, mid) + _dot(w_bf16, lo)


HEAD_SHIFT = HEAD_DIM.bit_length() - 1


def _lane_head(i):
    return lax.shift_right_logical(i, HEAD_SHIFT)


def _lane_in_head(i):
    return lax.bitwise_and(i, HEAD_DIM - 1)


def _sigmoid(x):
    return 1.0 / (1.0 + jnp.exp(-x))


def _silu(x):
    return x * _sigmoid(x)


def _softplus(x):
    return jnp.maximum(x, 0.0) + jnp.log1p(jnp.exp(-jnp.abs(x)))


def _rms(x, g):
    return x * lax.rsqrt(jnp.mean(x * x, axis=-1, keepdims=True) + NORM_EPS) * g


MXU_TILE = 256


def _head_sums(x2, bones):
    tile = bones[:MXU_TILE, :MXU_TILE]
    xb = x2.astype(BF16)
    return jnp.concatenate([_dot(xb[:, c:c + MXU_TILE], tile) for c in range(0, x2.shape[1], MXU_TILE)], axis=1)


def _head_rms(x, bones, g, precise=False):
    ss = _dot_f32_lhs(x * x, bones) if precise else _head_sums(x * x, bones)
    return x * lax.rsqrt(ss * (1.0 / HEAD_DIM) + NORM_EPS) * g


def _proj_kernel(*refs, tm, aw, sample):
    if sample:
        (x_ref, g_ref, w_ref, wc_ref, alog_ref, dtb_ref, esel_ref, cos_ref, s1_ref, s2_ref, bones_ref,
         wl_ref, st0_ref, st1_ref, st2_ref,
         qa_ref, ka_ref, va_ref, z_ref, la_ref, bt_ref, qb_ref, kb_ref, vb_ref, conv_ref) = refs
        buf = None
    else:
        (x_ref, g_ref, w_ref, wc_ref, alog_ref, dtb_ref, esel_ref, cos_ref, s1_ref, s2_ref, bones_ref,
         qa_ref, ka_ref, va_ref, z_ref, la_ref, bt_ref, qb_ref, kb_ref, vb_ref, conv_ref, buf) = refs
        t = pl.program_id(1)

        @pl.when(t == 0)
        def _():
            buf[0:8, :] = jnp.zeros((8, 3 * aw), F32)

    xf = _rms(x_ref[0], g_ref[...])
    xn = xf.astype(BF16)
    bones = bones_ref[...]
    if sample:
        xlo = (xf - xn.astype(F32)).astype(BF16)
        proj = lambda cs: _dot(xn, w_ref[:, cs]) + _dot(xlo, w_ref[:, cs]) + _dot(xn, wl_ref[:, cs])
        sumsq = lambda y: _dot_f32_lhs(y * y, bones)
    else:
        proj = lambda cs: _dot(xn, w_ref[:, cs])
        sumsq = lambda y: _head_sums(y * y, bones)

    outs_a = (qa_ref, ka_ref, va_ref)
    for c in range(3):
        cs = slice(c * aw, (c + 1) * aw)
        u = proj(cs)
        if sample:
            conv_ref[0, :, cs] = u
            y = (wc_ref[3:4, cs] * u + wc_ref[2:3, cs] * st2_ref[0, :, cs]
                 + wc_ref[1:2, cs] * st1_ref[0, :, cs] + wc_ref[0:1, cs] * st0_ref[0, :, cs])
        else:
            buf[8:8 + tm, cs] = u
            y = (wc_ref[3:4, cs] * u + wc_ref[2:3, cs] * buf[7:7 + tm, cs]
                 + wc_ref[1:2, cs] * buf[6:6 + tm, cs] + wc_ref[0:1, cs] * buf[5:5 + tm, cs])
        y = _silu(y)
        if c < 2:
            y = y * lax.rsqrt(sumsq(y) + NORM_EPS)
        outs_a[c][0] = y
    if not sample:
        tail = buf[tm:tm + 8, :]
        conv_ref[0] = tail
        buf[0:8, :] = tail

    z_ref[0] = proj(slice(3 * aw, 4 * aw))

    ab = proj(slice(7 * aw, 7 * aw + LANES))
    lane = lax.broadcasted_iota(jnp.int32, ab.shape, 1)
    log_a = -jnp.exp(alog_ref[...]) * _softplus(ab + dtb_ref[...])
    comb = jnp.where(lane < aw // HEAD_DIM, log_a, _sigmoid(ab))
    if sample:
        ex = _dot_f32_lhs(comb, esel_ref[...])
    else:
        c_hi = comb.astype(BF16)
        ex = _dot(c_hi, esel_ref[...]) + _dot((comb - c_hi.astype(F32)).astype(BF16), esel_ref[...])
    la_ref[0] = ex[:, :aw]
    bt_ref[0] = ex[:, aw:]

    cosv, s1v, s2v = cos_ref[...], s1_ref[...], s2_ref[...]
    for c, oref in ((0, qb_ref), (1, kb_ref)):
        for gq in range(aw // LANES):
            cs = slice(4 * aw + c * aw + gq * LANES, 4 * aw + c * aw + (gq + 1) * LANES)
            u = proj(cs)
            r = (u * cosv + pltpu.roll(u, LANES - ROT_DIM // 2, 1) * s1v
                 + pltpu.roll(u, ROT_DIM // 2, 1) * s2v)
            oref[0, :, gq * LANES:(gq + 1) * LANES] = r
    vb_ref[0] = proj(slice(6 * aw, 7 * aw))


def _proj(x3, g, w_cat, wconv, alog, dtb, esel, cos, s1, s2, bones, states, *, tm, sample):
    n, t, d = x3.shape
    aw = bones.shape[0]
    nt = t // tm
    const = lambda shape: pl.BlockSpec(shape, lambda i, j: (0,) * len(shape))
    row = lambda width: pl.BlockSpec((1, tm, width), lambda i, j: (i, j, 0))
    tab = (pl.BlockSpec((1, LANES), lambda i, j: (0, 0)) if sample
           else pl.BlockSpec((tm, LANES), lambda i, j: (j, 0)))
    in_specs = [row(d), const((1, d)), const(w_cat.shape), const(wconv.shape), const((1, LANES)),
                const((1, LANES)), const(esel.shape), tab, tab, tab, const(bones.shape)]
    args = [x3, g, w_cat, wconv, alog, dtb, esel, cos, s1, s2, bones]
    if sample:
        w_lo, states = states[0], states[1:]
        in_specs += [const(w_lo.shape)] + [row(3 * aw)] * 3
        args += [w_lo] + list(states)
        conv_shape = jax.ShapeDtypeStruct((n, t, 3 * aw), F32)
        conv_spec = row(3 * aw)
        scratch = []
    else:
        conv_shape = jax.ShapeDtypeStruct((n, 8, 3 * aw), F32)
        conv_spec = pl.BlockSpec((1, 8, 3 * aw), lambda i, j: (i, 0, 0))
        scratch = [pltpu.VMEM((tm + 8, 3 * aw), F32)]
    o = jax.ShapeDtypeStruct((n, t, aw), F32)
    return pl.pallas_call(
        functools.partial(_proj_kernel, tm=tm, aw=aw, sample=sample),
        grid=(n, nt),
        in_specs=in_specs,
        out_specs=[row(aw)] * 9 + [conv_spec],
        out_shape=[o] * 9 + [conv_shape],
        scratch_shapes=scratch,
        compiler_params=_cparams(("arbitrary", "arbitrary")),
        name="proj_sample" if sample else "proj_prompt",
    )(*args)


def _rowstack(x, hms):
    return jnp.concatenate([jnp.where(hm, x, 0.0) for hm in hms], axis=0)


def _gdn_kernel(q_ref, k_ref, v_ref, la_ref, bt_ref, ltri_ref, ones_ref, o_ref, s_out_ref,
                S, KW, NN, QP, OU, AL, *, tt, hw, unroll):
    C = GDN_CHUNK
    qw = 4 * HEAD_DIM
    nq = hw // qw
    t = pl.program_id(1)

    @pl.when(t == 0)
    def _():
        S[...] = jnp.zeros_like(S)

    rowi = lax.broadcasted_iota(jnp.int32, (C, hw), 0)
    colj = _lane_in_head(lax.broadcasted_iota(jnp.int32, (C, hw), 1))
    m_incl = colj <= rowi
    m_strict = colj < rowi
    eye = (colj == rowi).astype(F32)
    lane_q = _lane_head(lax.broadcasted_iota(jnp.int32, (1, qw), 1))
    hms = [lane_q == h for h in range(4)]
    ltri = ltri_ref[...]
    ones = ones_ref[...]

    def pm(l, r):
        return _dot(l.astype(BF16), _rowstack(r, hms).astype(BF16))

    def diag_blocks(full):
        return sum(jnp.where(hms[h], full[h * HEAD_DIM:(h + 1) * HEAD_DIM, :], 0.0) for h in range(4))

    sls = [slice(qi * qw, (qi + 1) * qw) for qi in range(nq)]

    def intra(it, carry):
        cs = [it * unroll + uu for uu in range(unroll)]
        rws = [pl.ds(pl.multiple_of(c * C, C), C) for c in cs]
        qs = [q_ref[0, r, :] * (HEAD_DIM ** -0.5) for r in rws]
        ks = [k_ref[0, r, :] for r in rws]
        bs = [bt_ref[0, r, :] for r in rws]
        gs = [_dot_f32_rhs(ltri, la_ref[0, r, :]) for r in rws]
        grs = [_dot_f32_rhs(ones, g * eye) for g in gs]
        decs = [jnp.exp(jnp.where(m_incl, g - gr, NEG)) for g, gr in zip(gs, grs)]
        egs = [jnp.exp(g) for g in gs]
        kbs = [k * b for k, b in zip(ks, bs)]
        for c, g in zip(cs, gs):
            AL[pl.ds(pl.multiple_of(c * 8, 8), 8), :] = jnp.broadcast_to(jnp.exp(g[C - 1:C, :]), (8, hw))
        ch = [(ui, sl) for ui in range(unroll) for sl in sls]
        kst = [_rowstack(ks[ui][:, sl], hms).astype(BF16) for ui, sl in ch]
        kq = [_dot_nt(jnp.concatenate([kbs[ui][:, sl], qs[ui][:, sl]], axis=0).astype(BF16), kst[i])
              for i, (ui, sl) in enumerate(ch)]
        x = [-jnp.where(m_strict[:, sl], kq[i][:C] * decs[ui][:, sl], 0.0) for i, (ui, sl) in enumerate(ch)]
        xm = [pm(xi, xi) for xi in x]
        p = [eye[:, sl] + x[i] for i, (ui, sl) in enumerate(ch)]
        for _ in range(int(math.log2(C)) - 2):
            r = [pm(jnp.concatenate([pi, xi], axis=0), xi) for pi, xi in zip(p, xm)]
            p = [pi + ri[:C] for pi, ri in zip(p, r)]
            xm = [ri[C:] for ri in r]
        p = [pi + pm(pi, xi) for pi, xi in zip(p, xm)]
        u = [pm(p[i], (v_ref[0, rws[ui], sl] * bs[ui][:, sl])) for i, (ui, sl) in enumerate(ch)]
        w = [pm(p[i], kbs[ui][:, sl] * egs[ui][:, sl]) for i, (ui, sl) in enumerate(ch)]
        qk = [jnp.where(m_incl[:, sl], kq[i][C:] * decs[ui][:, sl], 0.0).astype(BF16)
              for i, (ui, sl) in enumerate(ch)]
        kuw = []
        for i, (ui, sl) in enumerate(ch):
            g = gs[ui][:, sl]
            kd = (ks[ui][:, sl] * jnp.exp(g[C - 1:C, :] - g)).astype(BF16)
            uw = jnp.concatenate([u[i], w[i]], axis=1).astype(BF16)
            kuw.append(_dot_tn(kd, uw))
        ou = [_dot(qk[i], _rowstack(u[i], hms).astype(BF16)) for i in range(len(ch))]
        qw_ = [_dot(qk[i], _rowstack(w[i], hms).astype(BF16)) for i in range(len(ch))]
        for i, (ui, sl) in enumerate(ch):
            krows = pl.ds(pl.multiple_of(cs[ui] * HEAD_DIM, HEAD_DIM), HEAD_DIM)
            NN[krows, sl] = diag_blocks(kuw[i][:, :qw])
            KW[krows, sl] = diag_blocks(kuw[i][:, qw:])
            OU[rws[ui], sl] = ou[i]
            QP[rws[ui], sl] = qs[ui][:, sl] * egs[ui][:, sl] - qw_[i]
        return carry

    lax.fori_loop(0, tt // C // unroll, intra, 0)

    def inter(c, carry):
        rows = pl.ds(pl.multiple_of(c * C, C), C)
        krows = pl.ds(pl.multiple_of(c * HEAD_DIM, HEAD_DIM), HEAD_DIM)
        al = AL[pl.ds(pl.multiple_of(c * 8, 8), 1), :]
        ss = [S[:, sl] for sl in sls]
        lhs = [jnp.concatenate([KW[krows, sl], QP[rows, sl]], axis=0).astype(BF16) for sl in sls]
        rs = [_dot(l, _rowstack(s, hms).astype(BF16)) for l, s in zip(lhs, ss)]
        for sl, s, r in zip(sls, ss, rs):
            S[:, sl] = s * al[:, sl] - r[:HEAD_DIM] + NN[krows, sl]
            o_ref[0, rows, sl] = r[HEAD_DIM:] + OU[rows, sl]
        return carry

    lax.fori_loop(0, tt // C, inter, 0)
    s_out_ref[0] = S[...]


def _gdn_prompt(q, k, v, la, bt, *, tt, unroll):
    n, t, hw = q.shape
    C = GDN_CHUNK
    ltri = jnp.tril(jnp.ones((C, C), F32)).astype(BF16)
    ones = jnp.ones((C, C), BF16)
    row = pl.BlockSpec((1, tt, hw), lambda i, j: (i, j, 0))
    cst = pl.BlockSpec((C, C), lambda i, j: (0, 0))
    nch = tt // C
    return pl.pallas_call(
        functools.partial(_gdn_kernel, tt=tt, hw=hw, unroll=unroll),
        grid=(n, t // tt),
        in_specs=[row] * 5 + [cst, cst],
        out_specs=[row, pl.BlockSpec((1, HEAD_DIM, hw), lambda i, j: (i, 0, 0))],
        out_shape=[jax.ShapeDtypeStruct((n, t, hw), F32), jax.ShapeDtypeStruct((n, HEAD_DIM, hw), F32)],
        scratch_shapes=[pltpu.VMEM((HEAD_DIM, hw), F32), pltpu.VMEM((nch * HEAD_DIM, hw), F32),
                        pltpu.VMEM((nch * HEAD_DIM, hw), F32), pltpu.VMEM((tt, hw), F32),
                        pltpu.VMEM((tt, hw), F32), pltpu.VMEM((nch * 8, hw), F32)],
        compiler_params=_cparams(("arbitrary", "arbitrary")),
        name="gdn_prompt",
    )(q, k, v, la, bt, ltri, ones)


def _gdn_step_kernel(q_ref, k_ref, v_ref, la_ref, bt_ref, s_ref, bones_ref, o_ref, s_out_ref, *, gs, hw):
    rowi = lax.broadcasted_iota(jnp.int32, (HEAD_DIM, hw), 0)
    colj = _lane_in_head(lax.broadcasted_iota(jnp.int32, (HEAD_DIM, hw), 1))
    eye = (colj == rowi).astype(F32)
    bones = bones_ref[...]
    for i in range(gs):
        r = slice(i, i + 1)
        q = q_ref[r, :] * (HEAD_DIM ** -0.5)
        k = k_ref[r, :]
        v = v_ref[r, :]
        eg = jnp.exp(la_ref[r, :])
        b = bt_ref[r, :]
        s = s_ref[i]
        kbc = _dot_f32_lhs(eye * k, bones)
        qbc = _dot_f32_lhs(eye * q, bones)
        ks = jnp.sum(kbc * s, axis=0, keepdims=True)
        qs = jnp.sum(qbc * s, axis=0, keepdims=True)
        qk = jnp.sum(kbc * qbc, axis=0, keepdims=True)
        vn = b * v - b * eg * ks
        o_ref[r, :] = eg * qs + qk * vn
        s_out_ref[i] = s * eg + kbc * vn


def _gdn_sample(q, k, v, la, bt, s_packed, bones, *, gs):
    m, hw = q.shape
    row = pl.BlockSpec((gs, hw), lambda i: (i, 0))
    st = pl.BlockSpec((gs, HEAD_DIM, hw), lambda i: (i, 0, 0))
    return pl.pallas_call(
        functools.partial(_gdn_step_kernel, gs=gs, hw=hw),
        grid=(m // gs,),
        in_specs=[row] * 5 + [st, pl.BlockSpec(bones.shape, lambda i: (0, 0))],
        out_specs=[row, st],
        out_shape=[jax.ShapeDtypeStruct((m, hw), F32), jax.ShapeDtypeStruct(s_packed.shape, F32)],
        compiler_params=_cparams(("arbitrary",)),
        name="gdn_sample",
    )(q, k, v, la, bt, s_packed, bones)


def _band_kernel(q_ref, k_ref, v_ref, o_ref, kf, vf, oc, lc, *, tt, unroll):
    j = pl.program_id(2)

    @pl.when(j == 0)
    def _():
        kf[0:tt, :] = jnp.zeros((tt, LANES), F32)
        vf[0:tt, :] = jnp.zeros((tt, LANES), F32)

    kf[tt:, :] = k_ref[0]
    vf[tt:, :] = v_ref[0]
    a = lax.bitwise_and(lax.broadcasted_iota(jnp.int32, (2 * BAND, 2 * BAND), 0), BAND - 1)
    c = lax.broadcasted_iota(jnp.int32, (2 * BAND, 2 * BAND), 1)
    band = (c >= a) & (c <= a + BAND)
    bias = jnp.where(band, 0.0, NEG)
    bias_first = jnp.where(band & (c >= BAND), 0.0, NEG)
    row_h1 = lax.broadcasted_iota(jnp.int32, (2 * BAND, LANES), 0) >= BAND
    lane_h1 = lax.broadcasted_iota(jnp.int32, (2 * BAND, LANES), 1) >= HEAD_DIM
    own = row_h1 == lane_h1
    out_h1 = lax.broadcasted_iota(jnp.int32, (BAND, LANES), 1) >= HEAD_DIM
    ones_kv = jnp.ones((2 * BAND, LANES), BF16)

    for ci, (_, d) in enumerate(DILATIONS):
        nblk = tt // (BAND * d)

        def body(it, carry, d=d, ci=ci, nblk=nblk):
            rows, krows, firsts = [], [], []
            for uu in range(unroll):
                idx = it * unroll + uu
                r = idx // nblk
                b = idx % nblk
                start = b * (BAND * d) + r
                if d == 1:
                    rows.append(pl.ds(start, BAND))
                    krows.append(pl.ds(tt + start - BAND, 2 * BAND))
                else:
                    rows.append(pl.ds(start, BAND, stride=d))
                    krows.append(pl.ds(tt + start - BAND * d, 2 * BAND, stride=d))
                firsts.append(jnp.logical_and(j == 0, b == 0))
            ss = []
            for rw, kr in zip(rows, krows):
                qb = q_ref[0, rw, :] * (HEAD_DIM ** -0.5)
                qs = jnp.where(own, jnp.concatenate([qb, qb], axis=0), 0.0).astype(BF16)
                ss.append(_dot_nt(qs, kf[kr, :].astype(BF16)))
            ps, ms = [], []
            for s, first in zip(ss, firsts):
                s = s + jnp.where(first, bias_first, bias)
                m = jnp.max(s, axis=-1, keepdims=True)
                ps.append(jnp.exp((s - m).astype(BF16)))
                ms.append(m)
            pvl = [_dot(p, jnp.concatenate([vf[kr, :].astype(BF16), ones_kv], axis=1)) for p, kr in zip(ps, krows)]
            for rw, r, m in zip(rows, pvl, ms):
                l = r[:, LANES:]
                on = r[:, :LANES] / l
                lse = m + jnp.log(l)
                oc[ci, rw, :] = jnp.where(out_h1, on[BAND:], on[:BAND])
                lc[ci, rw, :] = jnp.where(out_h1, lse[BAND:], lse[:BAND])
            return carry

        lax.fori_loop(0, tt // BAND // unroll, body, 0)

    mrows = 256
    for ch in range(tt // mrows):
        rs = slice(ch * mrows, (ch + 1) * mrows)
        ls = [lc[ci, rs, :] for ci in range(len(DILATIONS))]
        mx = functools.reduce(jnp.maximum, ls)
        es = [jnp.exp(l - mx) for l in ls]
        o_ref[0, rs, :] = sum(e * oc[ci, rs, :] for ci, e in enumerate(es)) / sum(es)

    kf[0:tt, :] = kf[tt:, :]
    vf[0:tt, :] = vf[tt:, :]


def _band(q, k, v, *, tt, unroll):
    n, t, hw = q.shape
    assert all(w // d == BAND and tt % w == 0 for w, d in DILATIONS) and t % tt == 0
    blk = pl.BlockSpec((1, tt, LANES), lambda i, c, j: (i, j, c))
    nd = len(DILATIONS)
    return pl.pallas_call(
        functools.partial(_band_kernel, tt=tt, unroll=unroll),
        grid=(n, hw // LANES, t // tt),
        in_specs=[blk, blk, blk],
        out_specs=blk,
        out_shape=jax.ShapeDtypeStruct((n, t, hw), F32),
        scratch_shapes=[pltpu.VMEM((2 * tt, LANES), F32), pltpu.VMEM((2 * tt, LANES), F32),
                        pltpu.VMEM((nd, tt, LANES), F32), pltpu.VMEM((nd, tt, LANES), F32)],
        compiler_params=_cparams(("arbitrary", "arbitrary", "arbitrary")),
        name="band",
    )(q, k, v)


def _cache_attn_kernel(q_ref, kn_ref, vn_ref, k_ref, v_ref, o_ref, *, nh, n_past):
    pos = lax.broadcasted_iota(jnp.int32, (1, n_past), 1)
    cnt = jnp.zeros((1, n_past), F32)
    for window, dil in DILATIONS:
        hit = (pos >= n_past - window) & (lax.bitwise_and(pos, dil - 1) == 0)
        cnt = cnt + jnp.where(hit, 1.0, 0.0)
    live = cnt > 0.0
    nd = float(len(DILATIONS))
    lane_h = lax.broadcasted_iota(jnp.int32, (HEAD_DIM, nh), 1)
    out = jnp.zeros((HEAD_DIM, nh), F32)
    for h in range(nh):
        q = q_ref[0, :, h:h + 1] * (HEAD_DIM ** -0.5)
        s = jnp.where(live, jnp.sum(k_ref[0, h] * q, axis=0, keepdims=True), NEG)
        s0 = jnp.sum(q * kn_ref[0, :, h:h + 1], axis=0, keepdims=True)
        m = jnp.maximum(jnp.max(s, axis=-1, keepdims=True), s0)
        p = cnt * jnp.exp(s - m)
        p0 = nd * jnp.exp(s0 - m)
        den = p0 + jnp.sum(p, axis=-1, keepdims=True)
        num = p0 * vn_ref[0, :, h:h + 1] + jnp.sum(v_ref[0, h] * p, axis=-1, keepdims=True)
        out = jnp.where(lane_h == h, num / den, out)
    o_ref[0] = out


def _cache_attn(q_t, kn_t, vn_t, ck_t, cv_t):
    m, hd, nh = q_t.shape
    n_past = ck_t.shape[3]
    assert all(w // d == BAND and n_past >= w and d & (d - 1) == 0 and n_past % d == 0 for w, d in DILATIONS)
    col = pl.BlockSpec((1, hd, nh), lambda i: (i, 0, 0))
    cache = pl.BlockSpec((1, nh, hd, n_past), lambda i: (i, 0, 0, 0))
    return pl.pallas_call(
        functools.partial(_cache_attn_kernel, nh=nh, n_past=n_past),
        grid=(m,),
        in_specs=[col, col, col, cache, cache],
        out_specs=col,
        out_shape=jax.ShapeDtypeStruct((m, hd, nh), F32),
        compiler_params=_cparams(("arbitrary",)),
        name="cache_attn",
    )(q_t, kn_t, vn_t, ck_t, cv_t)


def _post_kernel(*refs, tm, precise):
    if precise:
        (x_ref, og_ref, z_ref, ob_ref, ga_ref, gb_ref, bones_ref, wo_ref, gf_ref, wrh_ref, wrl_ref, br_ref,
         wol_ref, h_ref, route_ref, rt_ref, cnt_ref, cnt) = refs
    else:
        (x_ref, og_ref, z_ref, ob_ref, ga_ref, gb_ref, bones_ref, wo_ref, gf_ref, wrh_ref, wrl_ref, br_ref,
         h_ref, route_ref, rt_ref, cnt_ref, cnt) = refs
    i = pl.program_id(0)

    @pl.when(i == 0)
    def _():
        cnt[...] = jnp.zeros_like(cnt)

    bones = bones_ref[...]
    oa = _head_rms(og_ref[...], bones, ga_ref[...], precise) * _silu(z_ref[...])
    ob = _head_rms(ob_ref[...], bones, gb_ref[...], precise)
    mixf = jnp.concatenate([oa, ob], axis=-1)
    mix = mixf.astype(BF16)
    proj = _dot(mix, wo_ref[...])
    if precise:
        proj = proj + _dot((mixf - mix.astype(F32)).astype(BF16), wo_ref[...]) + _dot(mix, wol_ref[...])
    h = x_ref[...] + proj
    h_ref[...] = h

    mrow = _rms(h, gf_ref[...])
    mh = mrow.astype(BF16)
    ml = (mrow - mh.astype(F32)).astype(BF16)
    logit = _dot(mh, wrh_ref[...]) + _dot(mh, wrl_ref[...]) + _dot(ml, wrh_ref[...]) + br_ref[...]
    lane = lax.broadcasted_iota(jnp.int32, logit.shape, 1).astype(F32)
    gl = jnp.where(lane < N_GROUPS, logit, NEG)
    gmax = jnp.max(gl, axis=-1, keepdims=True)
    grp = jnp.min(jnp.where(gl == gmax, lane, 1e9), axis=-1, keepdims=True)
    pg = 1.0 / jnp.sum(jnp.exp(gl - gmax), axis=-1, keepdims=True)
    lo = N_GROUPS + grp * EXPERTS_PER_GROUP
    el = jnp.where((lane >= lo) & (lane < lo + EXPERTS_PER_GROUP), logit, NEG)
    v1 = jnp.max(el, axis=-1, keepdims=True)
    i1 = jnp.min(jnp.where(el == v1, lane, 1e9), axis=-1, keepdims=True)
    el2 = jnp.where(lane == i1, NEG, el)
    v2 = jnp.max(el2, axis=-1, keepdims=True)
    i2 = jnp.min(jnp.where(el2 == v2, lane, 1e9), axis=-1, keepdims=True)
    e = jnp.exp(v2 - v1)
    g1 = pg / (1.0 + e)
    g2 = pg * e / (1.0 + e)
    e1 = i1 - N_GROUPS
    e2 = i2 - N_GROUPS

    oh1 = lane == e1
    oh2 = lane == e2
    onehot = jnp.where(oh1 | oh2, 1.0, 0.0)
    ri = lax.broadcasted_iota(jnp.int32, (tm, tm), 0)
    ci = lax.broadcasted_iota(jnp.int32, (tm, tm), 1)
    tri = jnp.where(ci < ri, 1.0, 0.0).astype(BF16)
    before = _dot(tri, onehot.astype(BF16)) + cnt[...]
    r1 = jnp.sum(jnp.where(oh1, before, 0.0), axis=-1, keepdims=True)
    r2 = jnp.sum(jnp.where(oh2, before, 0.0), axis=-1, keepdims=True)
    cnt[...] = cnt[...] + jnp.sum(onehot, axis=0, keepdims=True)
    cnt_ref[...] = cnt[...]
    route = jnp.zeros_like(logit)
    for j, val in enumerate((e1, e2, r1, r2, g1, g2)):
        route = jnp.where(lane == j, val, route)
    route_ref[...] = route
    rt_ref[...] = jnp.transpose(route)[:ROW_TILE]


def _post(x2, og, z, ob, ga, gb, bones, wo, gf, wrh, wrl, br, wo_lo, *, tm):
    m, d = x2.shape
    hw = og.shape[1]
    row = lambda w: pl.BlockSpec((tm, w), lambda i: (i, 0))
    const = lambda a: pl.BlockSpec(a.shape, lambda i: (0,) * a.ndim)
    consts = [ga, gb, bones, wo, gf, wrh, wrl, br] + ([] if wo_lo is None else [wo_lo])
    return pl.pallas_call(
        functools.partial(_post_kernel, tm=tm, precise=wo_lo is not None),
        grid=(m // tm,),
        in_specs=[row(d), row(hw), row(hw), row(hw)] + [const(a) for a in consts],
        out_specs=[row(d), row(LANES), pl.BlockSpec((ROW_TILE, tm), lambda i: (0, i)),
                   pl.BlockSpec((1, LANES), lambda i: (0, 0))],
        out_shape=[jax.ShapeDtypeStruct((m, d), F32), jax.ShapeDtypeStruct((m, LANES), F32),
                   jax.ShapeDtypeStruct((ROW_TILE, m), F32), jax.ShapeDtypeStruct((1, LANES), F32)],
        scratch_shapes=[pltpu.VMEM((1, LANES), F32)],
        compiler_params=_cparams(("arbitrary",)),
        name=f"post_{m}",
    )(x2, og, z, ob, *consts)


ROW_TILE = 8


def _rows_from_tiles(ref, n):
    return jnp.concatenate([ref[pl.ds(c, n, stride=ROW_TILE), :] for c in range(ROW_TILE)], axis=1)


def _rows_to_tiles(ref, rows):
    n = rows.shape[0]
    for c in range(ROW_TILE):
        ref[pl.ds(c, n, stride=ROW_TILE), :] = rows[:, c * LANES:(c + 1) * LANES]


def _dispatch_kernel(pstart_ref, pend_ref, h_ref, gf_ref, dest_ref, xbuf_ref, mrow, zbuf, dsm, sem, dsem, zsem,
                     *, tm, bm, nsteps, n_blk):
    i = pl.program_id(0)
    slot = i % 2
    dcp = pltpu.make_async_copy(dest_ref.at[i], dsm, dsem)
    dcp.start()

    def zero_copy(e):
        start = pl.multiple_of((pend_ref[e] - bm) * ROW_TILE, bm * ROW_TILE)
        return pltpu.make_async_copy(zbuf, xbuf_ref.at[pl.ds(start, bm * ROW_TILE)], zsem)

    def zero_block(b):
        start = pl.multiple_of(b * (bm * ROW_TILE), bm * ROW_TILE)
        return pltpu.make_async_copy(zbuf, xbuf_ref.at[pl.ds(start, bm * ROW_TILE)], zsem)

    @pl.when(i == 0)
    def _():
        zbuf[...] = jnp.zeros_like(zbuf)
        n_used = pend_ref[N_EXPERTS - 1] // bm
        for start_or_wait in (True, False):
            for e in range(N_EXPERTS):
                @pl.when(pend_ref[e] > pstart_ref[e])
                def _(e=e, start_or_wait=start_or_wait):
                    zero_copy(e).start() if start_or_wait else zero_copy(e).wait()

            def tail(b, carry, start_or_wait=start_or_wait):
                zero_block(b).start() if start_or_wait else zero_block(b).wait()
                return carry

            lax.fori_loop(n_used, n_blk, tail, 0)

    def wait_rows(s):
        for _ in range(2):
            pltpu.make_async_copy(mrow.at[s], xbuf_ref.at[pl.ds(0, tm * ROW_TILE)], sem.at[s]).wait()

    @pl.when(i >= 2)
    def _():
        wait_rows(slot)

    m = _rms(h_ref[...], gf_ref[...])
    dcp.wait()

    for s in range(2):
        @pl.when(slot == s)
        def _(s=s):
            _rows_to_tiles(mrow.at[s], m)

            def body(r, carry):
                src = mrow.at[s, pl.ds(pl.multiple_of(r * ROW_TILE, ROW_TILE), ROW_TILE)]
                for kk in range(2):
                    d = pl.multiple_of(dsm[kk * tm + r] * ROW_TILE, ROW_TILE)
                    pltpu.make_async_copy(src, xbuf_ref.at[pl.ds(d, ROW_TILE)], sem.at[s]).start()
                return carry

            lax.fori_loop(0, tm, body, 0, unroll=8)

    @pl.when(i == nsteps - 1)
    def _():
        wait_rows(slot)
        if nsteps >= 2:
            wait_rows(1 - slot)


def _dispatch(h, gf, dest, pstart, pend, n_slots, *, tm, bm):
    m, d = h.shape
    assert d == ROW_TILE * LANES
    nsteps = m // tm
    grid_spec = pltpu.PrefetchScalarGridSpec(
        num_scalar_prefetch=2,
        grid=(nsteps,),
        in_specs=[pl.BlockSpec((tm, d), lambda i, ps, pe: (i, 0)), pl.BlockSpec((1, d), lambda i, ps, pe: (0, 0)),
                  pl.BlockSpec(memory_space=pl.ANY)],
        out_specs=pl.BlockSpec(memory_space=pl.ANY),
        scratch_shapes=[pltpu.VMEM((2, tm * ROW_TILE, LANES), F32), pltpu.VMEM((bm * ROW_TILE, LANES), F32),
                        pltpu.SMEM((2 * tm,), jnp.int32),
                        pltpu.SemaphoreType.DMA((2,)), pltpu.SemaphoreType.DMA, pltpu.SemaphoreType.DMA],
    )
    return pl.pallas_call(
        functools.partial(_dispatch_kernel, tm=tm, bm=bm, nsteps=nsteps, n_blk=n_slots // bm),
        grid_spec=grid_spec,
        out_shape=jax.ShapeDtypeStruct((n_slots * ROW_TILE, LANES), F32),
        compiler_params=_cparams(("arbitrary",), row_dma=True),
        name=f"dispatch_{m}",
    )(pstart, pend, h, gf, dest)


def _expert_kernel(be_ref, nu_ref, x_ref, wg_ref, wu_ref, wd_ref, y_ref, *, bm):
    b = pl.program_id(0)

    @pl.when(b < nu_ref[0])
    def _():
        x = _rows_from_tiles(x_ref, bm).astype(BF16)
        hid = _silu(_dot(x, wg_ref[0])) * _dot(x, wu_ref[0])
        _rows_to_tiles(y_ref, _dot(hid.astype(BF16), wd_ref[0]))

    @pl.when(b >= nu_ref[0])
    def _():
        y_ref[...] = jnp.zeros_like(y_ref)


def _experts(xbuf, blk_expert, n_used, wg, wu, wd, *, bm):
    d, de = wg.shape[1], wg.shape[2]
    n_slots = xbuf.shape[0] // ROW_TILE
    blk = (bm * ROW_TILE, LANES)
    grid_spec = pltpu.PrefetchScalarGridSpec(
        num_scalar_prefetch=2,
        grid=(n_slots // bm,),
        in_specs=[pl.BlockSpec(blk, lambda b, be, nu: (jnp.minimum(b, nu[0] - 1), 0)),
                  pl.BlockSpec((1, d, de), lambda b, be, nu: (be[b], 0, 0)),
                  pl.BlockSpec((1, d, de), lambda b, be, nu: (be[b], 0, 0)),
                  pl.BlockSpec((1, de, d), lambda b, be, nu: (be[b], 0, 0))],
        out_specs=pl.BlockSpec(blk, lambda b, be, nu: (b, 0)),
    )
    return pl.pallas_call(
        functools.partial(_expert_kernel, bm=bm),
        grid_spec=grid_spec,
        out_shape=jax.ShapeDtypeStruct(xbuf.shape, F32),
        compiler_params=_cparams(("arbitrary",)),
        name=f"experts_{n_slots}",
    )(blk_expert, n_used, xbuf, wg, wu, wd)


def _combine_kernel(h_ref, route_ref, p_ref, wpg_ref, wpp_ref, gfin_ref, dest_ref, ybuf_ref, y_ref,
                    ys, dsm0, dsm1, sem, dsem, *, tm, nsteps):
    i = pl.program_id(0)
    slot = i % 2

    def issue(j, s):
        dsm = (dsm0, dsm1)[s]
        cp = pltpu.make_async_copy(dest_ref.at[j], dsm, dsem)
        cp.start()
        cp.wait()

        def body(r, carry):
            for kk in range(2):
                d = pl.multiple_of(dsm[kk * tm + r] * ROW_TILE, ROW_TILE)
                dst = ys.at[s, kk, pl.ds(pl.multiple_of(r * ROW_TILE, ROW_TILE), ROW_TILE)]
                pltpu.make_async_copy(ybuf_ref.at[pl.ds(d, ROW_TILE)], dst, sem.at[s]).start()
            return carry

        lax.fori_loop(0, tm, body, 0, unroll=8)

    @pl.when(i == 0)
    def _():
        issue(0, 0)

    for s in range(2):
        @pl.when(jnp.logical_and(i + 1 < nsteps, slot == 1 - s))
        def _(s=s):
            issue(i + 1, s)

    for kk in range(2):
        pltpu.make_async_copy(ybuf_ref.at[pl.ds(0, tm * ROW_TILE)], ys.at[slot, kk], sem.at[slot]).wait()

    route = route_ref[...]
    g1 = route[:, 4:5]
    g2 = route[:, 5:6]
    h = h_ref[...] + (g1 * _rows_from_tiles(ys.at[slot, 0], tm) + g2 * _rows_from_tiles(ys.at[slot, 1], tm))
    gate = _sigmoid(_dot(h.astype(BF16), wpg_ref[...]))
    out = h + gate * _dot(p_ref[...].astype(BF16), wpp_ref[...])
    y_ref[...] = _rms(out, gfin_ref[...])


def _combine(h, route, p, wpg, wpp, gfin, dest, ybuf, *, tm):
    m, d = h.shape
    row = lambda w: pl.BlockSpec((tm, w), lambda i: (i, 0))
    const = lambda a: pl.BlockSpec(a.shape, lambda i: (0,) * a.ndim)
    return pl.pallas_call(
        functools.partial(_combine_kernel, tm=tm, nsteps=m // tm),
        grid=(m // tm,),
        in_specs=[row(d), row(LANES), row(p.shape[1]), const(wpg), const(wpp), const(gfin),
                  pl.BlockSpec(memory_space=pl.ANY), pl.BlockSpec(memory_space=pl.ANY)],
        out_specs=row(d),
        out_shape=jax.ShapeDtypeStruct((m, d), F32),
        scratch_shapes=[pltpu.VMEM((2, 2, tm * ROW_TILE, LANES), F32), pltpu.SMEM((2 * tm,), jnp.int32),
                        pltpu.SMEM((2 * tm,), jnp.int32), pltpu.SemaphoreType.DMA((2,)), pltpu.SemaphoreType.DMA],
        compiler_params=_cparams(("arbitrary",), row_dma=True),
        name=f"combine_{m}",
    )(h, route, p, wpg, wpp, gfin, dest, ybuf)


def _tail(x2, og, z, ob, p2, wts, *, tm, bm, precise):
    m, d = x2.shape
    h, route, route_t, counts = _post(x2, og, z, ob, wts["ga"], wts["gb"], wts["bones"],
                                      wts["wo_hi"] if precise else wts["wo"], wts["gf"], wts["wrh"], wts["wrl"],
                                      wts["br"], wts["wo_lo"] if precise else None, tm=tm)
    eid = route_t[0:2].astype(jnp.int32)
    rank = route_t[2:4].astype(jnp.int32)
    sizes = counts[0, :N_EXPERTS].astype(jnp.int32)
    padded = (sizes + bm - 1) // bm * bm
    pend = jnp.cumsum(padded)
    pstart = pend - padded
    first_slot = jnp.sum(jnp.where(eid[..., None] == jnp.arange(N_EXPERTS, dtype=jnp.int32), pstart, 0), axis=-1)
    dest = (first_slot + rank).reshape(2, m // tm, tm).transpose(1, 0, 2).reshape(m // tm, 2 * tm)
    n_blk = (2 * m) // bm + N_EXPERTS
    blk_start = jnp.arange(n_blk, dtype=jnp.int32) * bm
    blk_expert = jnp.minimum(jnp.sum(pend[None, :] <= blk_start[:, None], axis=1), N_EXPERTS - 1).astype(jnp.int32)
    n_used = (pend[-1:] // bm).astype(jnp.int32)
    xbuf = _dispatch(h, wts["gf"], dest, pstart.astype(jnp.int32), pend.astype(jnp.int32), n_blk * bm,
                     tm=tm, bm=bm)
    ybuf = _experts(xbuf, blk_expert, n_used, wts["wg"], wts["wu"], wts["wd"], bm=bm)
    return _combine(h, route, p2, wts["wpg"], wts["wpp"], wts["gfin"], dest, ybuf, tm=tm)


def _split_hi_lo(w):
    bits = lax.bitcast_convert_type(w.astype(F32), jnp.uint32)
    hi32 = lax.bitcast_convert_type(bits & jnp.uint32(0xFFFF0000), F32)
    return hi32.astype(BF16), (w - hi32).astype(BF16)


def _rope_tables(pos):
    half = ROT_DIM // 2
    inv = ROPE_THETA ** (-jnp.arange(half, dtype=F32) * (2.0 / ROT_DIM))
    ang = pos.astype(F32)[:, None] * inv[None, :]
    cos, sin = jnp.cos(ang), jnp.sin(ang)
    j = jnp.arange(LANES) % HEAD_DIM
    first, second = j < half, (j >= half) & (j < ROT_DIM)
    jj = jnp.where(second, j - half, jnp.where(first, j, 0))
    c = jnp.where((first | second)[None, :], cos[:, jj], 1.0)
    s1 = jnp.where(first[None, :], -sin[:, jj], 0.0)
    s2 = jnp.where(second[None, :], sin[:, jj], 0.0)
    return c, s1, s2


def kernel(x_prompt, x_sample, cache_win_k, cache_win_v, state_conv, state_delta, p_prompt, p_sample, g_attn_norm, w_in, w_conv, a_log, dt_bias, g_a_out, g_b_out, w_out, g_ffn_norm, w_router_group, b_router_group, w_router_expert, b_router_expert, w_exp_gate, w_exp_up, w_exp_down, w_ple_gate, w_ple_proj, g_final):
    n, t, d = x_prompt.shape
    ns = x_sample.shape[0]
    assert w_in.shape[0] == 1 and x_sample.shape[1] == 1
    ha = a_log.shape[1]
    aw = ha * HEAD_DIM
    off_a = 4 * aw
    off_win = off_a + 2 * ha
    keep =min(DILATIONS[-1][0], t)
    hi = lambda a: a.astype(F32)

    w = w_in[0]
    w_ab = jnp.pad(w[:, off_a:off_win], ((0, 0), (0, LANES - 2 * ha)))
    w_cat32 = jnp.concatenate([w[:, :off_a], w[:, off_win:], w_ab], axis=1)
    w_cat = w_cat32.astype(BF16)
    w_cat_hi, w_cat_lo = _split_hi_lo(w_cat32)
    pad_l = lambda v: jnp.pad(hi(v), (0, LANES - v.shape[0]))[None, :]
    alog = pad_l(a_log[0])
    dtb = pad_l(dt_bias[0])
    lane_head = jnp.arange(2 * aw) // HEAD_DIM
    esel = (jnp.arange(LANES)[:, None] == lane_head[None, :]).astype(BF16)
    hd = jnp.arange(aw) // HEAD_DIM
    bones = (hd[:, None] == hd[None, :]).astype(BF16)
    wr = jnp.concatenate([hi(w_router_group[0]), hi(w_router_expert[0]).reshape(d, N_EXPERTS)], axis=1)
    wr = jnp.pad(wr, ((0, 0), (0, LANES - wr.shape[1])))
    wrh, wrl = _split_hi_lo(wr)
    wo_hi, wo_lo = _split_hi_lo(w_out[0])
    br = jnp.concatenate([hi(b_router_group[0]), hi(b_router_expert[0]).reshape(N_EXPERTS)])
    wts = dict(
        ga=jnp.tile(hi(g_a_out[0]), ha)[None, :], gb=jnp.tile(hi(g_b_out[0]), ha)[None, :], bones=bones,
        wo=w_out[0].astype(BF16), wo_hi=wo_hi, wo_lo=wo_lo,
        gf=hi(g_ffn_norm[0])[None, :], wrh=wrh, wrl=wrl, br=pad_l(br),
        wg=w_exp_gate[0].astype(BF16), wu=w_exp_up[0].astype(BF16), wd=w_exp_down[0].astype(BF16),
        wpg=w_ple_gate[0].astype(BF16), wpp=w_ple_proj[0].astype(BF16), gfin=hi(g_final)[None, :])
    g_attn = hi(g_attn_norm[0])[None, :]
    wconv = hi(w_conv[0])

    tm_p = min(512, t)
    cos, s1, s2 = _rope_tables(jnp.arange(t, dtype=jnp.int32))
    qa, ka, va, z, la, bt, qb, kb, vb, conv_p = _proj(
        x_prompt, g_attn, w_cat, wconv, alog, dtb, esel, cos, s1, s2, bones, None, tm=tm_p, sample=False)
    og, s_fin = _gdn_prompt(qa, ka, va, la, bt, tt=min(512, t), unroll=4)
    ob = _band(qb, kb, vb, tt=DILATIONS[-1][0], unroll=8)
    flat = lambda a: a.reshape(n * t, a.shape[-1])
    y_prompt = _tail(flat(x_prompt), flat(og), flat(z), flat(ob), flat(p_prompt[0]), wts,
                     tm=min(512, t), bm=256, precise=False).reshape(n, t, d)

    cos, s1, s2 = _rope_tables(jnp.full((1,), PAST_LEN, jnp.int32))
    st = [state_conv[0][None, :, j, :] for j in range(state_conv.shape[2])]
    xs3 = x_sample.reshape(1, ns, d)
    qa_s, ka_s, va_s, z_s, la_s, bt_s, qb_s, kb_s, vb_s, ua_s = [
        a[0] for a in _proj(xs3, g_attn, w_cat_hi, wconv, alog, dtb, esel, cos, s1, s2, bones, [w_cat_lo] + st,
                            tm=ns, sample=True)]
    s_packed = state_delta[0].transpose(0, 2, 1, 3).reshape(ns, HEAD_DIM, aw)
    og_s, s_new = _gdn_sample(qa_s, ka_s, va_s, la_s, bt_s, s_packed, bones, gs=8)
    cols = lambda a: a.reshape(ns, ha, HEAD_DIM).transpose(0, 2, 1)
    pos_minor = lambda c: c[0].transpose(0, 2, 3, 1)
    ob_s = _cache_attn(cols(qb_s), cols(kb_s), cols(vb_s), pos_minor(cache_win_k), pos_minor(cache_win_v))
    ob_s = ob_s.transpose(0, 2, 1).reshape(ns, aw)
    y_sample = _tail(x_sample.reshape(ns, d), og_s, z_s, ob_s, p_sample[0].reshape(ns, -1), wts,
                     tm=ns, bm=128, precise=True).reshape(ns, 1, d)

    unpack = lambda s: s.reshape(-1, HEAD_DIM, ha, HEAD_DIM).transpose(0, 2, 1, 3)[None]
    heads = lambda a: a.reshape(a.shape[0], -1, ha, HEAD_DIM)
    return (y_prompt, y_sample,
            heads(kb[:, t - keep:])[None], heads(vb[:, t - keep:])[None],
            conv_p[:, 8 - state_conv.shape[2]:][None], unpack(s_fin),
            heads(kb_s[:, None])[None], heads(vb_s[:, None])[None],
            jnp.concatenate([state_conv[0][:, 1:], ua_s[:, None]], axis=1)[None], unpack(s_new))
```

```python
import functools
import math

import jax
import jax.numpy as jnp
from jax import lax
from jax.experimental import pallas as pl
from jax.experimental.pallas import tpu as pltpu

F32 = jnp.float32
BF16 = jnp.bfloat16

HEAD_DIM = 64
GDN_CHUNK = 64
ROT_DIM = HEAD_DIM // 4
ROPE_THETA = 500000.0
PAST_LEN = 8192
DILATIONS = ((128, 1), (512, 4), (2048, 16))
BAND = 128
N_GROUPS = 4
EXPERTS_PER_GROUP = 8
N_EXPERTS = N_GROUPS * EXPERTS_PER_GROUP
NORM_EPS = 1e-6
NEG = -1e30
LANES = 128
VMEM_LIMIT = 56 * 1024 * 1024


def _cparams(sem, row_dma=False):
    return pltpu.CompilerParams(dimension_semantics=sem, vmem_limit_bytes=VMEM_LIMIT,
                                disable_bounds_checks=row_dma)


def _dot(a, b):
    return jnp.dot(a, b, preferred_element_type=F32)


def _dot_nt(a, b):
    return lax.dot_general(a, b, (((1,), (1,)), ((), ())), preferred_element_type=F32)


def _dot_tn(a, b):
    return lax.dot_general(a, b, (((0,), (0,)), ((), ())), preferred_element_type=F32)


def _split3(x):
    hi = x.astype(BF16)
    r = x - hi.astype(F32)
    mid = r.astype(BF16)
    lo = (r - mid.astype(F32)).astype(BF16)
    return hi, mid, lo


def _dot_f32_lhs(x, w_bf16):
    hi, mid, lo = _split3(x)
    return _dot(hi, w_bf16) + _dot(mid, w_bf16) + _dot(lo, w_bf16)


def _dot_f32_rhs(w_bf16, x):
    hi, mid, lo = _split3(x)
    return _dot(w_bf16, hi) + _dot(w_bf16, mid) + _dot(w_bf16, lo)


HEAD_SHIFT = HEAD_DIM.bit_length() - 1


def _lane_head(i):
    return lax.shift_right_logical(i, HEAD_SHIFT)


def _lane_in_head(i):
    return lax.bitwise_and(i, HEAD_DIM - 1)


def _sigmoid(x):
    return 1.0 / (1.0 + jnp.exp(-x))


def _silu(x):
    return x * _sigmoid(x)


def _softplus(x):
    return jnp.maximum(x, 0.0) + jnp.log1p(jnp.exp(-jnp.abs(x)))


def _rms(x, g):
    return x * lax.rsqrt(jnp.mean(x * x, axis=-1, keepdims=True) + NORM_EPS) * g


MXU_TILE = 256


def _head_sums(x2, bones):
    tile = bones[:MXU_TILE, :MXU_TILE]
    xb = x2.astype(BF16)
    return jnp.concatenate([_dot(xb[:, c:c + MXU_TILE], tile) for c in range(0, x2.shape[1], MXU_TILE)], axis=1)


def _head_rms(x, bones, g, precise=False):
    ss = _dot_f32_lhs(x * x, bones) if precise else _head_sums(x * x, bones)
    return x * lax.rsqrt(ss * (1.0 / HEAD_DIM) + NORM_EPS) * g


def _proj_kernel(*refs, tm, aw, sample):
    if sample:
        (x_ref, g_ref, w_ref, wc_ref, alog_ref, dtb_ref, esel_ref, cos_ref, s1_ref, s2_ref, bones_ref,
         wl_ref, st0_ref, st1_ref, st2_ref,
         qa_ref, ka_ref, va_ref, z_ref, la_ref, bt_ref, qb_ref, kb_ref, vb_ref, conv_ref) = refs
        buf = None
    else:
        (x_ref, g_ref, w_ref, wc_ref, alog_ref, dtb_ref, esel_ref, cos_ref, s1_ref, s2_ref, bones_ref,
         qa_ref, ka_ref, va_ref, z_ref, la_ref, bt_ref, qb_ref, kb_ref, vb_ref, conv_ref, buf) = refs
        t = pl.program_id(1)

        @pl.when(t == 0)
        def _():
            buf[0:8, :] = jnp.zeros((8, 3 * aw), F32)

    xf = _rms(x_ref[0], g_ref[...])
    xn = xf.astype(BF16)
    bones = bones_ref[...]
    if sample:
        xlo = (xf - xn.astype(F32)).astype(BF16)
        proj = lambda cs: _dot(xn, w_ref[:, cs]) + _dot(xlo, w_ref[:, cs]) + _dot(xn, wl_ref[:, cs])
        sumsq = lambda y: _dot_f32_lhs(y * y, bones)
    else:
        proj = lambda cs: _dot(xn, w_ref[:, cs])
        sumsq = lambda y: _head_sums(y * y, bones)

    outs_a = (qa_ref, ka_ref, va_ref)
    for c in range(3):
        cs = slice(c * aw, (c + 1) * aw)
        u = proj(cs)
        if sample:
            conv_ref[0, :, cs] = u
            y = (wc_ref[3:4, cs] * u + wc_ref[2:3, cs] * st2_ref[0, :, cs]
                 + wc_ref[1:2, cs] * st1_ref[0, :, cs] + wc_ref[0:1, cs] * st0_ref[0, :, cs])
        else:
            buf[8:8 + tm, cs] = u
            y = (wc_ref[3:4, cs] * u + wc_ref[2:3, cs] * buf[7:7 + tm, cs]
                 + wc_ref[1:2, cs] * buf[6:6 + tm, cs] + wc_ref[0:1, cs] * buf[5:5 + tm, cs])
        y = _silu(y)
        if c < 2:
            y = y * lax.rsqrt(sumsq(y) + NORM_EPS)
        outs_a[c][0] = y
    if not sample:
        tail = buf[tm:tm + 8, :]
        conv_ref[0] = tail
        buf[0:8, :] = tail

    z_ref[0] = proj(slice(3 * aw, 4 * aw))

    ab = proj(slice(7 * aw, 7 * aw + LANES))
    lane = lax.broadcasted_iota(jnp.int32, ab.shape, 1)
    log_a = -jnp.exp(alog_ref[...]) * _softplus(ab + dtb_ref[...])
    comb = jnp.where(lane < aw // HEAD_DIM, log_a, _sigmoid(ab))
    if sample:
        ex = _dot_f32_lhs(comb, esel_ref[...])
    else:
        c_hi = comb.astype(BF16)
        ex = _dot(c_hi, esel_ref[...]) + _dot((comb - c_hi.astype(F32)).astype(BF16), esel_ref[...])
    la_ref[0] = ex[:, :aw]
    bt_ref[0] = ex[:, aw:]

    cosv, s1v, s2v = cos_ref[...], s1_ref[...], s2_ref[...]
    for c, oref in ((0, qb_ref), (1, kb_ref)):
        for gq in range(aw // LANES):
            cs = slice(4 * aw + c * aw + gq * LANES, 4 * aw + c * aw + (gq + 1) * LANES)
            u = proj(cs)
            r = (u * cosv + pltpu.roll(u, LANES - ROT_DIM // 2, 1) * s1v
                 + pltpu.roll(u, ROT_DIM // 2, 1) * s2v)
            oref[0, :, gq * LANES:(gq + 1) * LANES] = r
    vb_ref[0] = proj(slice(6 * aw, 7 * aw))


def _proj(x3, g, w_cat, wconv, alog, dtb, esel, cos, s1, s2, bones, states, *, tm, sample):
    n, t, d = x3.shape
    aw = bones.shape[0]
    nt = t // tm
    const = lambda shape: pl.BlockSpec(shape, lambda i, j: (0,) * len(shape))
    row = lambda width: pl.BlockSpec((1, tm, width), lambda i, j: (i, j, 0))
    tab = (pl.BlockSpec((1, LANES), lambda i, j: (0, 0)) if sample
           else pl.BlockSpec((tm, LANES), lambda i, j: (j, 0)))
    in_specs = [row(d), const((1, d)), const(w_cat.shape), const(wconv.shape), const((1, LANES)),
                const((1, LANES)), const(esel.shape), tab, tab, tab, const(bones.shape)]
    args = [x3, g, w_cat, wconv, alog, dtb, esel, cos, s1, s2, bones]
    if sample:
        w_lo, states = states[0], states[1:]
        in_specs += [const(w_lo.shape)] + [row(3 * aw)] * 3
        args += [w_lo] + list(states)
        conv_shape = jax.ShapeDtypeStruct((n, t, 3 * aw), F32)
        conv_spec = row(3 * aw)
        scratch = []
    else:
        conv_shape = jax.ShapeDtypeStruct((n, 8, 3 * aw), F32)
        conv_spec = pl.BlockSpec((1, 8, 3 * aw), lambda i, j: (i, 0, 0))
        scratch = [pltpu.VMEM((tm + 8, 3 * aw), F32)]
    o = jax.ShapeDtypeStruct((n, t, aw), F32)
    return pl.pallas_call(
        functools.partial(_proj_kernel, tm=tm, aw=aw, sample=sample),
        grid=(n, nt),
        in_specs=in_specs,
        out_specs=[row(aw)] * 9 + [conv_spec],
        out_shape=[o] * 9 + [conv_shape],
        scratch_shapes=scratch,
        compiler_params=_cparams(("arbitrary", "arbitrary")),
        name="proj_sample" if sample else "proj_prompt",
    )(*args)


def _rowstack(x, hms):
    return jnp.concatenate([jnp.where(hm, x, 0.0) for hm in hms], axis=0)


def _gdn_kernel(q_ref, k_ref, v_ref, la_ref, bt_ref, ltri_ref, ones_ref, o_ref, s_out_ref,
                S, KW, NN, QP, OU, AL, *, tt, hw, unroll):
    C = GDN_CHUNK
    qw = 4 * HEAD_DIM
    nq = hw // qw
    t = pl.program_id(1)

    @pl.when(t == 0)
    def _():
        S[...] = jnp.zeros_like(S)

    rowi = lax.broadcasted_iota(jnp.int32, (C, hw), 0)
    colj = _lane_in_head(lax.broadcasted_iota(jnp.int32, (C, hw), 1))
    m_incl = colj <= rowi
    m_strict = colj < rowi
    eye = (colj == rowi).astype(F32)
    lane_q = _lane_head(lax.broadcasted_iota(jnp.int32, (1, qw), 1))
    hms = [lane_q == h for h in range(4)]
    ltri = ltri_ref[...]
    ones = ones_ref[...]

    def pm(l, r):
        return _dot(l.astype(BF16), _rowstack(r, hms).astype(BF16))

    def diag_blocks(full):
        return sum(jnp.where(hms[h], full[h * HEAD_DIM:(h + 1) * HEAD_DIM, :], 0.0) for h in range(4))

    sls = [slice(qi * qw, (qi + 1) * qw) for qi in range(nq)]

    def intra(it, carry):
        cs = [it * unroll + uu for uu in range(unroll)]
        rws = [pl.ds(pl.multiple_of(c * C, C), C) for c in cs]
        qs = [q_ref[0, r, :] * (HEAD_DIM ** -0.5) for r in rws]
        ks = [k_ref[0, r, :] for r in rws]
        bs = [bt_ref[0, r, :] for r in rws]
        gs = [_dot_f32_rhs(ltri, la_ref[0, r, :]) for r in rws]
        grs = [_dot_f32_rhs(ones, g * eye) for g in gs]
        decs = [jnp.exp(jnp.where(m_incl, g - gr, NEG)) for g, gr in zip(gs, grs)]
        egs = [jnp.exp(g) for g in gs]
        kbs = [k * b for k, b in zip(ks, bs)]
        for c, g in zip(cs, gs):
            AL[pl.ds(pl.multiple_of(c * 8, 8), 8), :] = jnp.broadcast_to(jnp.exp(g[C - 1:C, :]), (8, hw))
        ch = [(ui, sl) for ui in range(unroll) for sl in sls]
        kst = [_rowstack(ks[ui][:, sl], hms).astype(BF16) for ui, sl in ch]
        kq = [_dot_nt(jnp.concatenate([kbs[ui][:, sl], qs[ui][:, sl]], axis=0).astype(BF16), kst[i])
              for i, (ui, sl) in enumerate(ch)]
        x = [-jnp.where(m_strict[:, sl], kq[i][:C] * decs[ui][:, sl], 0.0) for i, (ui, sl) in enumerate(ch)]
        xm = [pm(xi, xi) for xi in x]
        p = [eye[:, sl] + x[i] for i, (ui, sl) in enumerate(ch)]
        for _ in range(int(math.log2(C)) - 2):
            r = [pm(jnp.concatenate([pi, xi], axis=0), xi) for pi, xi in zip(p, xm)]
            p = [pi + ri[:C] for pi, ri in zip(p, r)]
            xm = [ri[C:] for ri in r]
        p = [pi + pm(pi, xi) for pi, xi in zip(p, xm)]
        u = [pm(p[i], (v_ref[0, rws[ui], sl] * bs[ui][:, sl])) for i, (ui, sl) in enumerate(ch)]
        w = [pm(p[i], kbs[ui][:, sl] * egs[ui][:, sl]) for i, (ui, sl) in enumerate(ch)]
        qk = [jnp.where(m_incl[:, sl], kq[i][C:] * decs[ui][:, sl], 0.0).astype(BF16)
              for i, (ui, sl) in enumerate(ch)]
        kuw = []
        for i, (ui, sl) in enumerate(ch):
            g = gs[ui][:, sl]
            kd = (ks[ui][:, sl] * jnp.exp(g[C - 1:C, :] - g)).astype(BF16)
            uw = jnp.concatenate([u[i], w[i]], axis=1).astype(BF16)
            kuw.append(_dot_tn(kd, uw))
        ou = [_dot(qk[i], _rowstack(u[i], hms).astype(BF16)) for i in range(len(ch))]
        qw_ = [_dot(qk[i], _rowstack(w[i], hms).astype(BF16)) for i in range(len(ch))]
        for i, (ui, sl) in enumerate(ch):
            krows = pl.ds(pl.multiple_of(cs[ui] * HEAD_DIM, HEAD_DIM), HEAD_DIM)
            NN[krows, sl] = diag_blocks(kuw[i][:, :qw])
            KW[krows, sl] = diag_blocks(kuw[i][:, qw:])
            OU[rws[ui], sl] = ou[i]
            QP[rws[ui], sl] = qs[ui][:, sl] * egs[ui][:, sl] - qw_[i]
        return carry

    lax.fori_loop(0, tt // C // unroll, intra, 0)

    def inter(c, carry):
        rows = pl.ds(pl.multiple_of(c * C, C), C)
        krows = pl.ds(pl.multiple_of(c * HEAD_DIM, HEAD_DIM), HEAD_DIM)
        al = AL[pl.ds(pl.multiple_of(c * 8, 8), 1), :]
        ss = [S[:, sl] for sl in sls]
        lhs = [jnp.concatenate([KW[krows, sl], QP[rows, sl]], axis=0).astype(BF16) for sl in sls]
        rs = [_dot(l, _rowstack(s, hms).astype(BF16)) for l, s in zip(lhs, ss)]
        for sl, s, r in zip(sls, ss, rs):
            S[:, sl] = s * al[:, sl] - r[:HEAD_DIM] + NN[krows, sl]
            o_ref[0, rows, sl] = r[HEAD_DIM:] + OU[rows, sl]
        return carry

    lax.fori_loop(0, tt // C, inter, 0)
    s_out_ref[0] = S[...]


def _gdn_prompt(q, k, v, la, bt, *, tt, unroll):
    n, t, hw = q.shape
    C = GDN_CHUNK
    ltri = jnp.tril(jnp.ones((C, C), F32)).astype(BF16)
    ones = jnp.ones((C, C), BF16)
    row = pl.BlockSpec((1, tt, hw), lambda i, j: (i, j, 0))
    cst = pl.BlockSpec((C, C), lambda i, j: (0, 0))
    nch = tt // C
    return pl.pallas_call(
        functools.partial(_gdn_kernel, tt=tt, hw=hw, unroll=unroll),
        grid=(n, t // tt),
        in_specs=[row] * 5 + [cst, cst],
        out_specs=[row, pl.BlockSpec((1, HEAD_DIM, hw), lambda i, j: (i, 0, 0))],
        out_shape=[jax.ShapeDtypeStruct((n, t, hw), F32), jax.ShapeDtypeStruct((n, HEAD_DIM, hw), F32)],
        scratch_shapes=[pltpu.VMEM((HEAD_DIM, hw), F32), pltpu.VMEM((nch * HEAD_DIM, hw), F32),
                        pltpu.VMEM((nch * HEAD_DIM, hw), F32), pltpu.VMEM((tt, hw), F32),
                        pltpu.VMEM((tt, hw), F32), pltpu.VMEM((nch * 8, hw), F32)],
        compiler_params=_cparams(("arbitrary", "arbitrary")),
        name="gdn_prompt",
    )(q, k, v, la, bt, ltri, ones)


def _gdn_step_kernel(q_ref, k_ref, v_ref, la_ref, bt_ref, s_ref, bones_ref, o_ref, s_out_ref, *, gs, hw):
    rowi = lax.broadcasted_iota(jnp.int32, (HEAD_DIM, hw), 0)
    colj = _lane_in_head(lax.broadcasted_iota(jnp.int32, (HEAD_DIM, hw), 1))
    eye = (colj == rowi).astype(F32)
    bones = bones_ref[...]
    for i in range(gs):
        r = slice(i, i + 1)
        q = q_ref[r, :] * (HEAD_DIM ** -0.5)
        k = k_ref[r, :]
        v = v_ref[r, :]
        eg = jnp.exp(la_ref[r, :])
        b = bt_ref[r, :]
        s = s_ref[i]
        kbc = _dot_f32_lhs(eye * k, bones)
        qbc = _dot_f32_lhs(eye * q, bones)
        ks = jnp.sum(kbc * s, axis=0, keepdims=True)
        qs = jnp.sum(qbc * s, axis=0, keepdims=True)
        qk = jnp.sum(kbc * qbc, axis=0, keepdims=True)
        vn = b * v - b * eg * ks
        o_ref[r, :] = eg * qs + qk * vn
        s_out_ref[i] = s * eg + kbc * vn


def _gdn_sample(q, k, v, la, bt, s_packed, bones, *, gs):
    m, hw = q.shape
    row = pl.BlockSpec((gs, hw), lambda i: (i, 0))
    st = pl.BlockSpec((gs, HEAD_DIM, hw), lambda i: (i, 0, 0))
    return pl.pallas_call(
        functools.partial(_gdn_step_kernel, gs=gs, hw=hw),
        grid=(m // gs,),
        in_specs=[row] * 5 + [st, pl.BlockSpec(bones.shape, lambda i: (0, 0))],
        out_specs=[row, st],
        out_shape=[jax.ShapeDtypeStruct((m, hw), F32), jax.ShapeDtypeStruct(s_packed.shape, F32)],
        compiler_params=_cparams(("arbitrary",)),
        name="gdn_sample",
    )(q, k, v, la, bt, s_packed, bones)


def _band_kernel(q_ref, k_ref, v_ref, o_ref, kf, vf, oc, lc, *, tt, unroll):
    j = pl.program_id(2)

    @pl.when(j == 0)
    def _():
        kf[0:tt, :] = jnp.zeros((tt, LANES), F32)
        vf[0:tt, :] = jnp.zeros((tt, LANES), F32)

    kf[tt:, :] = k_ref[0]
    vf[tt:, :] = v_ref[0]
    a = lax.bitwise_and(lax.broadcasted_iota(jnp.int32, (2 * BAND, 2 * BAND), 0), BAND - 1)
    c = lax.broadcasted_iota(jnp.int32, (2 * BAND, 2 * BAND), 1)
    band = (c >= a) & (c <= a + BAND)
    bias = jnp.where(band, 0.0, NEG)
    bias_first = jnp.where(band & (c >= BAND), 0.0, NEG)
    row_h1 = lax.broadcasted_iota(jnp.int32, (2 * BAND, LANES), 0) >= BAND
    lane_h1 = lax.broadcasted_iota(jnp.int32, (2 * BAND, LANES), 1) >= HEAD_DIM
    own = row_h1 == lane_h1
    out_h1 = lax.broadcasted_iota(jnp.int32, (BAND, LANES), 1) >= HEAD_DIM
    ones_kv = jnp.ones((2 * BAND, LANES), BF16)

    for ci, (_, d) in enumerate(DILATIONS):
        nblk = tt // (BAND * d)

        def body(it, carry, d=d, ci=ci, nblk=nblk):
            rows, krows, firsts = [], [], []
            for uu in range(unroll):
                idx = it * unroll + uu
                r = idx // nblk
                b = idx % nblk
                start = b * (BAND * d) + r
                if d == 1:
                    rows.append(pl.ds(start, BAND))
                    krows.append(pl.ds(tt + start - BAND, 2 * BAND))
                else:
                    rows.append(pl.ds(start, BAND, stride=d))
                    krows.append(pl.ds(tt + start - BAND * d, 2 * BAND, stride=d))
                firsts.append(jnp.logical_and(j == 0, b == 0))
            ss = []
            for rw, kr in zip(rows, krows):
                qb = q_ref[0, rw, :] * (HEAD_DIM ** -0.5)
                qs = jnp.where(own, jnp.concatenate([qb, qb], axis=0), 0.0).astype(BF16)
                ss.append(_dot_nt(qs, kf[kr, :].astype(BF16)))
            ps, ms = [], []
            for s, first in zip(ss, firsts):
                s = s + jnp.where(first, bias_first, bias)
                m = jnp.max(s, axis=-1, keepdims=True)
                ps.append(jnp.exp((s - m).astype(BF16)))
                ms.append(m)
            pvl = [_dot(p, jnp.concatenate([vf[kr, :].astype(BF16), ones_kv], axis=1)) for p, kr in zip(ps, krows)]
            for rw, r, m in zip(rows, pvl, ms):
                l = r[:, LANES:]
                on = r[:, :LANES] / l
                lse = m + jnp.log(l)
                oc[ci, rw, :] = jnp.where(out_h1, on[BAND:], on[:BAND])
                lc[ci, rw, :] = jnp.where(out_h1, lse[BAND:], lse[:BAND])
            return carry

        lax.fori_loop(0, tt // BAND // unroll, body, 0)

    mrows = 256
    for ch in range(tt // mrows):
        rs = slice(ch * mrows, (ch + 1) * mrows)
        ls = [lc[ci, rs, :] for ci in range(len(DILATIONS))]
        mx = functools.reduce(jnp.maximum, ls)
        es = [jnp.exp(l - mx) for l in ls]
        o_ref[0, rs, :] = sum(e * oc[ci, rs, :] for ci, e in enumerate(es)) / sum(es)

    kf[0:tt, :] = kf[tt:, :]
    vf[0:tt, :] = vf[tt:, :]


def _band(q, k, v, *, tt, unroll):
    n, t, hw = q.shape
    assert all(w // d == BAND and tt % w == 0 for w, d in DILATIONS) and t % tt == 0
    blk = pl.BlockSpec((1, tt, LANES), lambda i, c, j: (i, j, c))
    nd = len(DILATIONS)
    return pl.pallas_call(
        functools.partial(_band_kernel, tt=tt, unroll=unroll),
        grid=(n, hw // LANES, t // tt),
        in_specs=[blk, blk, blk],
        out_specs=blk,
        out_shape=jax.ShapeDtypeStruct((n, t, hw), F32),
        scratch_shapes=[pltpu.VMEM((2 * tt, LANES), F32), pltpu.VMEM((2 * tt, LANES), F32),
                        pltpu.VMEM((nd, tt, LANES), F32), pltpu.VMEM((nd, tt, LANES), F32)],
        compiler_params=_cparams(("arbitrary", "arbitrary", "arbitrary")),
        name="band",
    )(q, k, v)


def _cache_attn_kernel(q_ref, kn_ref, vn_ref, k_ref, v_ref, o_ref, *, nh, n_past):
    pos = lax.broadcasted_iota(jnp.int32, (1, n_past), 1)
    cnt = jnp.zeros((1, n_past), F32)
    for window, dil in DILATIONS:
        hit = (pos >= n_past - window) & (lax.bitwise_and(pos, dil - 1) == 0)
        cnt = cnt + jnp.where(hit, 1.0, 0.0)
    live = cnt > 0.0
    nd = float(len(DILATIONS))
    lane_h = lax.broadcasted_iota(jnp.int32, (HEAD_DIM, nh), 1)
    out = jnp.zeros((HEAD_DIM, nh), F32)
    for h in range(nh):
        q = q_ref[0, :, h:h + 1] * (HEAD_DIM ** -0.5)
        s = jnp.where(live, jnp.sum(k_ref[0, h] * q, axis=0, keepdims=True), NEG)
        s0 = jnp.sum(q * kn_ref[0, :, h:h + 1], axis=0, keepdims=True)
        m = jnp.maximum(jnp.max(s, axis=-1, keepdims=True), s0)
        p = cnt * jnp.exp(s - m)
        p0 = nd * jnp.exp(s0 - m)
        den = p0 + jnp.sum(p, axis=-1, keepdims=True)
        num = p0 * vn_ref[0, :, h:h + 1] + jnp.sum(v_ref[0, h] * p, axis=-1, keepdims=True)
        out = jnp.where(lane_h == h, num / den, out)
    o_ref[0] = out


def _cache_attn(q_t, kn_t, vn_t, ck_t, cv_t):
    m, hd, nh = q_t.shape
    n_past = ck_t.shape[3]
    assert all(w // d == BAND and n_past >= w and d & (d - 1) == 0 and n_past % d == 0 for w, d in DILATIONS)
    col = pl.BlockSpec((1, hd, nh), lambda i: (i, 0, 0))
    cache = pl.BlockSpec((1, nh, hd, n_past), lambda i: (i, 0, 0, 0))
    return pl.pallas_call(
        functools.partial(_cache_attn_kernel, nh=nh, n_past=n_past),
        grid=(m,),
        in_specs=[col, col, col, cache, cache],
        out_specs=col,
        out_shape=jax.ShapeDtypeStruct((m, hd, nh), F32),
        compiler_params=_cparams(("arbitrary",)),
        name="cache_attn",
    )(q_t, kn_t, vn_t, ck_t, cv_t)


def _post_kernel(*refs, tm, precise):
    if precise:
        (x_ref, og_ref, z_ref, ob_ref, ga_ref, gb_ref, bones_ref, wo_ref, gf_ref, wrh_ref, wrl_ref, br_ref,
         wol_ref, h_ref, route_ref, rt_ref, cnt_ref, cnt) = refs
    else:
        (x_ref, og_ref, z_ref, ob_ref, ga_ref, gb_ref, bones_ref, wo_ref, gf_ref, wrh_ref, wrl_ref, br_ref,
         h_ref, route_ref, rt_ref, cnt_ref, cnt) = refs
    i = pl.program_id(0)

    @pl.when(i == 0)
    def _():
        cnt[...] = jnp.zeros_like(cnt)

    bones = bones_ref[...]
    oa = _head_rms(og_ref[...], bones, ga_ref[...], precise) * _silu(z_ref[...])
    ob = _head_rms(ob_ref[...], bones, gb_ref[...], precise)
    mixf = jnp.concatenate([oa, ob], axis=-1)
    mix = mixf.astype(BF16)
    proj = _dot(mix, wo_ref[...])
    if precise:
        proj = proj + _dot((mixf - mix.astype(F32)).astype(BF16), wo_ref[...]) + _dot(mix, wol_ref[...])
    h = x_ref[...] + proj
    h_ref[...] = h

    mrow = _rms(h, gf_ref[...])
    mh = mrow.astype(BF16)
    ml = (mrow - mh.astype(F32)).astype(BF16)
    logit = _dot(mh, wrh_ref[...]) + _dot(mh, wrl_ref[...]) + _dot(ml, wrh_ref[...]) + br_ref[...]
    lane = lax.broadcasted_iota(jnp.int32, logit.shape, 1).astype(F32)
    gl = jnp.where(lane < N_GROUPS, logit, NEG)
    gmax = jnp.max(gl, axis=-1, keepdims=True)
    grp = jnp.min(jnp.where(gl == gmax, lane, 1e9), axis=-1, keepdims=True)
    pg = 1.0 / jnp.sum(jnp.exp(gl - gmax), axis=-1, keepdims=True)
    lo = N_GROUPS + grp * EXPERTS_PER_GROUP
    el = jnp.where((lane >= lo) & (lane < lo + EXPERTS_PER_GROUP), logit, NEG)
    v1 = jnp.max(el, axis=-1, keepdims=True)
    i1 = jnp.min(jnp.where(el == v1, lane, 1e9), axis=-1, keepdims=True)
    el2 = jnp.where(lane == i1, NEG, el)
    v2 = jnp.max(el2, axis=-1, keepdims=True)
    i2 = jnp.min(jnp.where(el2 == v2, lane, 1e9), axis=-1, keepdims=True)
    e = jnp.exp(v2 - v1)
    g1 = pg / (1.0 + e)
    g2 = pg * e / (1.0 + e)
    e1 = i1 - N_GROUPS
    e2 = i2 - N_GROUPS

    oh1 = lane == e1
    oh2 = lane == e2
    onehot = jnp.where(oh1 | oh2, 1.0, 0.0)
    ri = lax.broadcasted_iota(jnp.int32, (tm, tm), 0)
    ci = lax.broadcasted_iota(jnp.int32, (tm, tm), 1)
    tri = jnp.where(ci < ri, 1.0, 0.0).astype(BF16)
    before = _dot(tri, onehot.astype(BF16)) + cnt[...]
    r1 = jnp.sum(jnp.where(oh1, before, 0.0), axis=-1, keepdims=True)
    r2 = jnp.sum(jnp.where(oh2, before, 0.0), axis=-1, keepdims=True)
    cnt[...] = cnt[...] + jnp.sum(onehot, axis=0, keepdims=True)
    cnt_ref[...] = cnt[...]
    route = jnp.zeros_like(logit)
    for j, val in enumerate((e1, e2, r1, r2, g1, g2)):
        route = jnp.where(lane == j, val, route)
    route_ref[...] = route
    rt_ref[...] = jnp.transpose(route)[:ROW_TILE]


def _post(x2, og, z, ob, ga, gb, bones, wo, gf, wrh, wrl, br, wo_lo, *, tm):
    m, d = x2.shape
    hw = og.shape[1]
    row = lambda w: pl.BlockSpec((tm, w), lambda i: (i, 0))
    const = lambda a: pl.BlockSpec(a.shape, lambda i: (0,) * a.ndim)
    consts = [ga, gb, bones, wo, gf, wrh, wrl, br] + ([] if wo_lo is None else [wo_lo])
    return pl.pallas_call(
        functools.partial(_post_kernel, tm=tm, precise=wo_lo is not None),
        grid=(m // tm,),
        in_specs=[row(d), row(hw), row(hw), row(hw)] + [const(a) for a in consts],
        out_specs=[row(d), row(LANES), pl.BlockSpec((ROW_TILE, tm), lambda i: (0, i)),
                   pl.BlockSpec((1, LANES), lambda i: (0, 0))],
        out_shape=[jax.ShapeDtypeStruct((m, d), F32), jax.ShapeDtypeStruct((m, LANES), F32),
                   jax.ShapeDtypeStruct((ROW_TILE, m), F32), jax.ShapeDtypeStruct((1, LANES), F32)],
        scratch_shapes=[pltpu.VMEM((1, LANES), F32)],
        compiler_params=_cparams(("arbitrary",)),
        name=f"post_{m}",
    )(x2, og, z, ob, *consts)


ROW_TILE = 8


def _rows_from_tiles(ref, n):
    return jnp.concatenate([ref[pl.ds(c, n, stride=ROW_TILE), :] for c in range(ROW_TILE)], axis=1)


def _rows_to_tiles(ref, rows):
    n = rows.shape[0]
    for c in range(ROW_TILE):
        ref[pl.ds(c, n, stride=ROW_TILE), :] = rows[:, c * LANES:(c + 1) * LANES]


def _dispatch_kernel(pstart_ref, pend_ref, h_ref, gf_ref, dest_ref, xbuf_ref, mrow, zbuf, dsm, sem, dsem, zsem,
                     *, tm, bm, nsteps, n_blk):
    i = pl.program_id(0)
    slot = i % 2
    dcp = pltpu.make_async_copy(dest_ref.at[i], dsm, dsem)
    dcp.start()

    def zero_copy(e):
        start = pl.multiple_of((pend_ref[e] - bm) * ROW_TILE, bm * ROW_TILE)
        return pltpu.make_async_copy(zbuf, xbuf_ref.at[pl.ds(start, bm * ROW_TILE)], zsem)

    def zero_block(b):
        start = pl.multiple_of(b * (bm * ROW_TILE), bm * ROW_TILE)
        return pltpu.make_async_copy(zbuf, xbuf_ref.at[pl.ds(start, bm * ROW_TILE)], zsem)

    @pl.when(i == 0)
    def _():
        zbuf[...] = jnp.zeros_like(zbuf)
        n_used = pend_ref[N_EXPERTS - 1] // bm
        for start_or_wait in (True, False):
            for e in range(N_EXPERTS):
                @pl.when(pend_ref[e] > pstart_ref[e])
                def _(e=e, start_or_wait=start_or_wait):
                    zero_copy(e).start() if start_or_wait else zero_copy(e).wait()

            def tail(b, carry, start_or_wait=start_or_wait):
                zero_block(b).start() if start_or_wait else zero_block(b).wait()
                return carry

            lax.fori_loop(n_used, n_blk, tail, 0)

    def wait_rows(s):
        for _ in range(2):
            pltpu.make_async_copy(mrow.at[s], xbuf_ref.at[pl.ds(0, tm * ROW_TILE)], sem.at[s]).wait()

    @pl.when(i >= 2)
    def _():
        wait_rows(slot)

    m = _rms(h_ref[...], gf_ref[...])
    dcp.wait()

    for s in range(2):
        @pl.when(slot == s)
        def _(s=s):
            _rows_to_tiles(mrow.at[s], m)

            def body(r, carry):
                src = mrow.at[s, pl.ds(pl.multiple_of(r * ROW_TILE, ROW_TILE), ROW_TILE)]
                for kk in range(2):
                    d = pl.multiple_of(dsm[kk * tm + r] * ROW_TILE, ROW_TILE)
                    pltpu.make_async_copy(src, xbuf_ref.at[pl.ds(d, ROW_TILE)], sem.at[s]).start()
                return carry

            lax.fori_loop(0, tm, body, 0, unroll=8)

    @pl.when(i == nsteps - 1)
    def _():
        wait_rows(slot)
        if nsteps >= 2:
            wait_rows(1 - slot)


def _dispatch(h, gf, dest, pstart, pend, n_slots, *, tm, bm):
    m, d = h.shape
    assert d == ROW_TILE * LANES
    nsteps = m // tm
    grid_spec = pltpu.PrefetchScalarGridSpec(
        num_scalar_prefetch=2,
        grid=(nsteps,),
        in_specs=[pl.BlockSpec((tm, d), lambda i, ps, pe: (i, 0)), pl.BlockSpec((1, d), lambda i, ps, pe: (0, 0)),
                  pl.BlockSpec(memory_space=pl.ANY)],
        out_specs=pl.BlockSpec(memory_space=pl.ANY),
        scratch_shapes=[pltpu.VMEM((2, tm * ROW_TILE, LANES), F32), pltpu.VMEM((bm * ROW_TILE, LANES), F32),
                        pltpu.SMEM((2 * tm,), jnp.int32),
                        pltpu.SemaphoreType.DMA((2,)), pltpu.SemaphoreType.DMA, pltpu.SemaphoreType.DMA],
    )
    return pl.pallas_call(
        functools.partial(_dispatch_kernel, tm=tm, bm=bm, nsteps=nsteps, n_blk=n_slots // bm),
        grid_spec=grid_spec,
        out_shape=jax.ShapeDtypeStruct((n_slots * ROW_TILE, LANES), F32),
        compiler_params=_cparams(("arbitrary",), row_dma=True),
        name=f"dispatch_{m}",
    )(pstart, pend, h, gf, dest)


def _expert_kernel(be_ref, nu_ref, x_ref, wg_ref, wu_ref, wd_ref, y_ref, wgb, wub, wdb, *, bm):
    b = pl.program_id(0)
    live = b < nu_ref[0]
    new_expert = jnp.logical_or(b == 0, be_ref[b] != be_ref[jnp.maximum(b - 1, 0)])

    @pl.when(jnp.logical_and(live, new_expert))
    def _():
        wgb[...] = wg_ref[0].astype(BF16)
        wub[...] = wu_ref[0].astype(BF16)
        wdb[...] = wd_ref[0].astype(BF16)

    @pl.when(live)
    def _():
        x = _rows_from_tiles(x_ref, bm).astype(BF16)
        hid = _silu(_dot(x, wgb[...])) * _dot(x, wub[...])
        _rows_to_tiles(y_ref, _dot(hid.astype(BF16), wdb[...]))

    @pl.when(b >= nu_ref[0])
    def _():
        y_ref[...] = jnp.zeros_like(y_ref)


def _experts(xbuf, blk_expert, n_used, wg, wu, wd, *, bm):
    d, de = wg.shape[1], wg.shape[2]
    n_slots = xbuf.shape[0] // ROW_TILE
    blk = (bm * ROW_TILE, LANES)
    grid_spec = pltpu.PrefetchScalarGridSpec(
        num_scalar_prefetch=2,
        grid=(n_slots // bm,),
        in_specs=[pl.BlockSpec(blk, lambda b, be, nu: (jnp.minimum(b, nu[0] - 1), 0)),
                  pl.BlockSpec((1, d, de), lambda b, be, nu: (be[b], 0, 0)),
                  pl.BlockSpec((1, d, de), lambda b, be, nu: (be[b], 0, 0)),
                  pl.BlockSpec((1, de, d), lambda b, be, nu: (be[b], 0, 0))],
        out_specs=pl.BlockSpec(blk, lambda b, be, nu: (b, 0)),
        scratch_shapes=[pltpu.VMEM((d, de), BF16), pltpu.VMEM((d, de), BF16), pltpu.VMEM((de, d), BF16)],
    )
    return pl.pallas_call(
        functools.partial(_expert_kernel, bm=bm),
        grid_spec=grid_spec,
        out_shape=jax.ShapeDtypeStruct(xbuf.shape, F32),
        compiler_params=_cparams(("arbitrary",)),
        name=f"experts_{n_slots}",
    )(blk_expert, n_used, xbuf, wg, wu, wd)


def _combine_kernel(h_ref, route_ref, p_ref, wpg_ref, wpp_ref, gfin_ref, dest_ref, ybuf_ref, y_ref,
                    ys, dsm0, dsm1, sem, dsem, *, tm, nsteps):
    i = pl.program_id(0)
    slot = i % 2

    def issue(j, s):
        dsm = (dsm0, dsm1)[s]
        cp = pltpu.make_async_copy(dest_ref.at[j], dsm, dsem)
        cp.start()
        cp.wait()

        def body(r, carry):
            for kk in range(2):
                d = pl.multiple_of(dsm[kk * tm + r] * ROW_TILE, ROW_TILE)
                dst = ys.at[s, kk, pl.ds(pl.multiple_of(r * ROW_TILE, ROW_TILE), ROW_TILE)]
                pltpu.make_async_copy(ybuf_ref.at[pl.ds(d, ROW_TILE)], dst, sem.at[s]).start()
            return carry

        lax.fori_loop(0, tm, body, 0, unroll=8)

    @pl.when(i == 0)
    def _():
        issue(0, 0)

    for s in range(2):
        @pl.when(jnp.logical_and(i + 1 < nsteps, slot == 1 - s))
        def _(s=s):
            issue(i + 1, s)

    for kk in range(2):
        pltpu.make_async_copy(ybuf_ref.at[pl.ds(0, tm * ROW_TILE)], ys.at[slot, kk], sem.at[slot]).wait()

    route = route_ref[...]
    g1 = route[:, 4:5]
    g2 = route[:, 5:6]
    h = h_ref[...] + (g1 * _rows_from_tiles(ys.at[slot, 0], tm) + g2 * _rows_from_tiles(ys.at[slot, 1], tm))
    gate = _sigmoid(_dot(h.astype(BF16), wpg_ref[...]))
    out = h + gate * _dot(p_ref[...].astype(BF16), wpp_ref[...])
    y_ref[...] = _rms(out, gfin_ref[...])


def _combine(h, route, p, wpg, wpp, gfin, dest, ybuf, *, tm):
    m, d = h.shape
    row = lambda w: pl.BlockSpec((tm, w), lambda i: (i, 0))
    const = lambda a: pl.BlockSpec(a.shape, lambda i: (0,) * a.ndim)
    return pl.pallas_call(
        functools.partial(_combine_kernel, tm=tm, nsteps=m // tm),
        grid=(m // tm,),
        in_specs=[row(d), row(LANES), row(p.shape[1]), const(wpg), const(wpp), const(gfin),
                  pl.BlockSpec(memory_space=pl.ANY), pl.BlockSpec(memory_space=pl.ANY)],
        out_specs=row(d),
        out_shape=jax.ShapeDtypeStruct((m, d), F32),
        scratch_shapes=[pltpu.VMEM((2, 2, tm * ROW_TILE, LANES), F32), pltpu.SMEM((2 * tm,), jnp.int32),
                        pltpu.SMEM((2 * tm,), jnp.int32), pltpu.SemaphoreType.DMA((2,)), pltpu.SemaphoreType.DMA],
        compiler_params=_cparams(("arbitrary",), row_dma=True),
        name=f"combine_{m}",
    )(h, route, p, wpg, wpp, gfin, dest, ybuf)


def _tail(x2, og, z, ob, p2, wts, *, tm, bm, precise):
    m, d = x2.shape
    h, route, route_t, counts = _post(x2, og, z, ob, wts["ga"], wts["gb"], wts["bones"],
                                      wts["wo_hi"] if precise else wts["wo"], wts["gf"], wts["wrh"], wts["wrl"],
                                      wts["br"], wts["wo_lo"] if precise else None, tm=tm)
    eid = route_t[0:2].astype(jnp.int32)
    rank = route_t[2:4].astype(jnp.int32)
    sizes = counts[0, :N_EXPERTS].astype(jnp.int32)
    padded = (sizes + bm - 1) // bm * bm
    pend = jnp.cumsum(padded)
    pstart = pend - padded
    first_slot = jnp.sum(jnp.where(eid[..., None] == jnp.arange(N_EXPERTS, dtype=jnp.int32), pstart, 0), axis=-1)
    dest = (first_slot + rank).reshape(2, m // tm, tm).transpose(1, 0, 2).reshape(m // tm, 2 * tm)
    n_blk = (2 * m) // bm + N_EXPERTS
    blk_start = jnp.arange(n_blk, dtype=jnp.int32) * bm
    blk_expert = jnp.minimum(jnp.sum(pend[None, :] <= blk_start[:, None], axis=1), N_EXPERTS - 1).astype(jnp.int32)
    n_used = (pend[-1:] // bm).astype(jnp.int32)
    xbuf = _dispatch(h, wts["gf"], dest, pstart.astype(jnp.int32), pend.astype(jnp.int32), n_blk * bm,
                     tm=tm, bm=bm)
    ybuf = _experts(xbuf, blk_expert, n_used, wts["wg"], wts["wu"], wts["wd"], bm=bm)
    return _combine(h, route, p2, wts["wpg"], wts["wpp"], wts["gfin"], dest, ybuf, tm=tm)


def _split_hi_lo(w):
    bits = lax.bitcast_convert_type(w.astype(F32), jnp.uint32)
    hi32 = lax.bitcast_convert_type(bits & jnp.uint32(0xFFFF0000), F32)
    return hi32.astype(BF16), (w - hi32).astype(BF16)


def _rope_tables(pos):
    half = ROT_DIM // 2
    inv = ROPE_THETA ** (-jnp.arange(half, dtype=F32) * (2.0 / ROT_DIM))
    ang = pos.astype(F32)[:, None] * inv[None, :]
    cos, sin = jnp.cos(ang), jnp.sin(ang)
    j = jnp.arange(LANES) % HEAD_DIM
    first, second = j < half, (j >= half) & (j < ROT_DIM)
    jj = jnp.where(second, j - half, jnp.where(first, j, 0))
    c = jnp.where((first | second)[None, :], cos[:, jj], 1.0)
    s1 = jnp.where(first[None, :], -sin[:, jj], 0.0)
    s2 = jnp.where(second[None, :], sin[:, jj], 0.0)
    return c, s1, s2


def kernel(x_prompt, x_sample, cache_win_k, cache_win_v, state_conv, state_delta, p_prompt, p_sample, g_attn_norm, w_in, w_conv, a_log, dt_bias, g_a_out, g_b_out, w_out, g_ffn_norm, w_router_group, b_router_group, w_router_expert, b_router_expert, w_exp_gate, w_exp_up, w_exp_down, w_ple_gate, w_ple_proj, g_final):
    n, t, d = x_prompt.shape
    ns = x_sample.shape[0]
    assert w_in.shape[0] == 1 and x_sample.shape[1] == 1
    ha = a_log.shape[1]
    aw = ha * HEAD_DIM
    off_a = 4 * aw
    off_win = off_a + 2 * ha
    keep =min(DILATIONS[-1][0], t)
    hi = lambda a: a.astype(F32)

    w = w_in[0]
    w_ab = jnp.pad(w[:, off_a:off_win], ((0, 0), (0, LANES - 2 * ha)))
    w_cat32 = jnp.concatenate([w[:, :off_a], w[:, off_win:], w_ab], axis=1)
    w_cat = w_cat32.astype(BF16)
    w_cat_hi, w_cat_lo = _split_hi_lo(w_cat32)
    pad_l = lambda v: jnp.pad(hi(v), (0, LANES - v.shape[0]))[None, :]
    alog = pad_l(a_log[0])
    dtb = pad_l(dt_bias[0])
    lane_head = jnp.arange(2 * aw) // HEAD_DIM
    esel = (jnp.arange(LANES)[:, None] == lane_head[None, :]).astype(BF16)
    hd = jnp.arange(aw) // HEAD_DIM
    bones = (hd[:, None] == hd[None, :]).astype(BF16)
    wr = jnp.concatenate([hi(w_router_group[0]), hi(w_router_expert[0]).reshape(d, N_EXPERTS)], axis=1)
    wr = jnp.pad(wr, ((0, 0), (0, LANES - wr.shape[1])))
    wrh, wrl = _split_hi_lo(wr)
    wo_hi, wo_lo = _split_hi_lo(w_out[0])
    br = jnp.concatenate([hi(b_router_group[0]), hi(b_router_expert[0]).reshape(N_EXPERTS)])
    wts = dict(
        ga=jnp.tile(hi(g_a_out[0]), ha)[None, :], gb=jnp.tile(hi(g_b_out[0]), ha)[None, :], bones=bones,
        wo=w_out[0].astype(BF16), wo_hi=wo_hi, wo_lo=wo_lo,
        gf=hi(g_ffn_norm[0])[None, :], wrh=wrh, wrl=wrl, br=pad_l(br),
        wg=hi(w_exp_gate[0]), wu=hi(w_exp_up[0]), wd=hi(w_exp_down[0]),
        wpg=w_ple_gate[0].astype(BF16), wpp=w_ple_proj[0].astype(BF16), gfin=hi(g_final)[None, :])
    g_attn = hi(g_attn_norm[0])[None, :]
    wconv = hi(w_conv[0])

    tm_p = min(512, t)
    cos, s1, s2 = _rope_tables(jnp.arange(t, dtype=jnp.int32))
    qa, ka, va, z, la, bt, qb, kb, vb, conv_p = _proj(
        x_prompt, g_attn, w_cat, wconv, alog, dtb, esel, cos, s1, s2, bones, None, tm=tm_p, sample=False)
    og, s_fin = _gdn_prompt(qa, ka, va, la, bt, tt=min(512, t), unroll=4)
    ob = _band(qb, kb, vb, tt=DILATIONS[-1][0], unroll=8)
    flat = lambda a: a.reshape(n * t, a.shape[-1])
    y_prompt = _tail(flat(x_prompt), flat(og), flat(z), flat(ob), flat(p_prompt[0]), wts,
                     tm=min(512, t), bm=256, precise=False).reshape(n, t, d)

    cos, s1, s2 = _rope_tables(jnp.full((1,), PAST_LEN, jnp.int32))
    st = [state_conv[0][None, :, j, :] for j in range(state_conv.shape[2])]
    xs3 = x_sample.reshape(1, ns, d)
    qa_s, ka_s, va_s, z_s, la_s, bt_s, qb_s, kb_s, vb_s, ua_s = [
        a[0] for a in _proj(xs3, g_attn, w_cat_hi, wconv, alog, dtb, esel, cos, s1, s2, bones, [w_cat_lo] + st,
                            tm=ns, sample=True)]
    s_packed = state_delta[0].transpose(0, 2, 1, 3).reshape(ns, HEAD_DIM, aw)
    og_s, s_new = _gdn_sample(qa_s, ka_s, va_s, la_s, bt_s, s_packed, bones, gs=8)
    cols = lambda a: a.reshape(ns, ha, HEAD_DIM).transpose(0, 2, 1)
    pos_minor = lambda c: c[0].transpose(0, 2, 3, 1)
    ob_s = _cache_attn(cols(qb_s), cols(kb_s), cols(vb_s), pos_minor(cache_win_k), pos_minor(cache_win_v))
    ob_s = ob_s.transpose(0, 2, 1).reshape(ns, aw)
    y_sample = _tail(x_sample.reshape(ns, d), og_s, z_s, ob_s, p_sample[0].reshape(ns, -1), wts,
                     tm=ns, bm=128, precise=True).reshape(ns, 1, d)

    unpack = lambda s: s.reshape(-1, HEAD_DIM, ha, HEAD_DIM).transpose(0, 2, 1, 3)[None]
    heads = lambda a: a.reshape(a.shape[0], -1, ha, HEAD_DIM)
    return (y_prompt, y_sample,
            heads(kb[:, t - keep:])[None], heads(vb[:, t - keep:])[None],
            conv_p[:, 8 - state_conv.shape[2]:][None], unpack(s_fin),
            heads(kb_s[:, None])[None], heads(vb_s[:, None])[None],
            jnp.concatenate([state_conv[0][:, 1:], ua_s[:, None]], axis=1)[None], unpack(s_new))
```
